```python
import functools
import jax, jax.numpy as jnp
from jax import lax
import numpy as np

D_MODEL = 1024
BATCH = 4
SEQ = 4096
DEPTH = 1
DEC_BATCH = 32
DEC_SEQ = 4
PAST_LEN = 8192
PAGE_SIZE = 128

HEAD_DIM = 64
D_RWKV = D_MODEL // 2
D_SB = D_MODEL // 2
H_RWKV = D_RWKV // HEAD_DIM
H_SB = D_SB // HEAD_DIM
N_BRANCH = 2
DECAY_LORA = max(32, int(round(1.8 * D_MODEL ** 0.5 / 32)) * 32)
AAA_LORA = max(32, int(round(1.8 * D_MODEL ** 0.5 / 32)) * 32)
GATE_LORA = max(32, int(round(0.6 * D_MODEL ** 0.8 / 32)) * 32)
D_FF = ((8 * D_MODEL // 3 + 127) // 128) * 128
W_IN_COLS = 3 * D_RWKV + 3 * D_SB + N_BRANCH * D_MODEL
Q_BLOCK = 128
SB_SCALE = HEAD_DIM ** -0.5
SB_BIAS_INIT = -6.0
LN_EPS = 1e-5
LNX_EPS = 64e-5
DN_ALPHA = (2.0 * DEPTH) ** 0.25
DN_BETA = (8.0 * DEPTH) ** -0.25

kernel_name = "rwkv7_stickbreaking_parallel_deepnorm_macaron_step"


def layer_norm(x, g, b):
    xf = x.astype(jnp.float32)
    mu = jnp.mean(xf, -1, keepdims=True)
    var = jnp.mean(jnp.square(xf - mu), -1, keepdims=True)
    return ((xf - mu) * lax.rsqrt(var + LN_EPS) * g + b).astype(x.dtype)


def swiglu(x, w_in, w_out):
    gate, up = jnp.split(x @ w_in, 2, axis=-1)
    return (jax.nn.silu(gate) * up) @ w_out


def wkv7_scan(s0, r, w, k, a_vec, b_vec, v):
    def step(S, inp):
        r_t, w_t, k_t, a_t, b_t, v_t = inp
        sa = jnp.einsum('bhij,bhj->bhi', S, a_t)
        S = (S * w_t[:, :, None, :] + sa[..., :, None] * b_t[:, :, None, :]
             + v_t[..., :, None] * k_t[:, :, None, :])
        return S, jnp.einsum('bhij,bhj->bhi', S, r_t)
    xs = tuple(jnp.moveaxis(t, 1, 0) for t in (r, w, k, a_vec, b_vec, v))
    S, ys = lax.scan(step, s0.astype(jnp.float32), xs)
    return S, jnp.moveaxis(ys, 0, 1)


def rwkv7_branch(h, p_rkv, h_last, wkv0, w_rkv, mu_rkv, mu_wag, w0, w_w1, w_w2,
                 a0, a_w1, a_w2, g_w1, g_w2, k_k, k_a, r_k, lnx_g, lnx_b):
    B, T, _ = h.shape
    f32 = jnp.float32
    p_prev = jnp.concatenate([(h_last @ w_rkv)[:, None], p_rkv[:, :-1]], axis=1)
    r, k, v = jnp.split(p_rkv + (p_prev - p_rkv) * mu_rkv, 3, axis=-1)
    dx = jnp.concatenate([h_last[:, None], h[:, :-1]], axis=1) - h
    xw = h + dx * mu_wag[0]
    xa = h + dx * mu_wag[1]
    xg = h + dx * mu_wag[2]
    w_log = -jax.nn.softplus(-(w0 + jnp.tanh(xw @ w_w1) @ w_w2).astype(f32)) - 0.5
    decay = jnp.exp(-jnp.exp(w_log))
    a = jax.nn.sigmoid((a0 + (xa @ a_w1) @ a_w2).astype(f32))
    g = jax.nn.sigmoid(xg @ g_w1) @ g_w2
    heads = lambda t: t.astype(f32).reshape(B, T, H_RWKV, HEAD_DIM)
    kk = heads(k * k_k)
    kk = kk / jnp.maximum(jnp.sqrt(jnp.sum(kk * kk, -1, keepdims=True)), 1e-12)
    k_mod = k.astype(f32) * (1.0 + (a - 1.0) * k_a)
    rh, kh, vh, ah = heads(r), heads(k_mod), heads(v), heads(a)
    wkv, y = wkv7_scan(wkv0, rh, heads(decay), kh, -kk, kk * ah, vh)
    mu = jnp.mean(y, -1, keepdims=True)
    var = jnp.mean(jnp.square(y - mu), -1, keepdims=True)
    y = ((y - mu) * lax.rsqrt(var + LNX_EPS)).reshape(B, T, D_RWKV) * lnx_g + lnx_b
    bonus = jnp.sum(rh * kh * r_k, -1, keepdims=True) * vh
    o = (y + bonus.reshape(B, T, D_RWKV)) * g
    return o.astype(h.dtype), wkv.astype(wkv0.dtype)


def stick_breaking(q, k, v, bias, q_pos, k_pos):
    z = (jnp.einsum('bqhd,bshd->bhqs', q, k).astype(jnp.float32) * SB_SCALE
         + bias.astype(jnp.float32)[None, :, None, None])
    causal = k_pos[None, :] < q_pos[:, None]
    log_1mb = jnp.where(causal, jax.nn.log_sigmoid(-z), 0.0)
    excl = lax.cumsum(log_1mb, axis=3, reverse=True) - log_1mb
    A = jnp.where(causal, jnp.exp(jax.nn.log_sigmoid(z) + excl), 0.0)
    return jnp.einsum('bhqs,bshd->bqhd', A.astype(v.dtype), v)


def sb_prompt(q, k, v, bias):
    B, T, H, d = q.shape
    nb = T // Q_BLOCK
    qb = jnp.moveaxis(q.reshape(B, nb, Q_BLOCK, H, d), 1, 0)
    k_pos = jnp.arange(T)

    def one_block(args):
        i, q_blk = args
        return stick_breaking(q_blk, k, v, bias, i * Q_BLOCK + jnp.arange(Q_BLOCK), k_pos)

    ob = lax.map(one_block, (jnp.arange(nb), qb))
    return jnp.moveaxis(ob, 0, 1).reshape(B, T, H * d)


def sb_sample(ck, cv, page_table, q, k, v, bias):
    B, T, H, d = q.shape
    k_past = ck[page_table].reshape(B, -1, H, d)
    v_past = cv[page_table].reshape(B, -1, H, d)
    past = k_past.shape[1]
    k_all = jnp.concatenate([k_past, k], axis=1)
    v_all = jnp.concatenate([v_past, v], axis=1)
    o = stick_breaking(q, k_all, v_all, bias, past + jnp.arange(T), jnp.arange(past + T))
    return o.reshape(B, T, H * d)


def trunk_layer(x, h_last, wkv0, sb_fn, p):
    x = layer_norm(DN_ALPHA * x + 0.5 * swiglu(x, p['ffn1_w_in'], p['ffn1_w_out']), p['ln1_g'], p['ln1_b'])
    h = x
    B, T, _ = h.shape
    proj = h @ p['w_in']
    n_rkv = 3 * D_RWKV
    p_rkv = proj[..., :n_rkv]
    q, k, v = jnp.split(proj[..., n_rkv:n_rkv + 3 * D_SB], 3, axis=-1)
    gate_logits = proj[..., n_rkv + 3 * D_SB:]
    o_a, wkv = rwkv7_branch(h, p_rkv, h_last, wkv0, p['w_in'][:, :n_rkv], p['mu_rkv'], p['mu_wag'],
                            p['w0'], p['w_w1'], p['w_w2'], p['a0'], p['a_w1'], p['a_w2'],
                            p['g_w1'], p['g_w2'], p['k_k'], p['k_a'], p['r_k'], p['lnx_g'], p['lnx_b'])
    q, k, v = (t.reshape(B, T, H_SB, HEAD_DIM) for t in (q, k, v))
    o_b = sb_fn(q, k, v, p['sb_bias'])
    gates = jax.nn.sigmoid(gate_logits.astype(jnp.float32)).astype(h.dtype)
    merged = (gates[..., :D_MODEL] * (o_a @ p['w_branch'][:D_RWKV])
              + gates[..., D_MODEL:] * (o_b @ p['w_branch'][D_RWKV:]))
    x = layer_norm(DN_ALPHA * x + merged @ p['w_out'], p['ln2_g'], p['ln2_b'])
    x = layer_norm(DN_ALPHA * x + 0.5 * swiglu(x, p['ffn2_w_in'], p['ffn2_w_out']), p['ln3_g'], p['ln3_b'])
    return x, k, v, wkv, h[:, -1]


def setup_inputs(seed: int = 0) -> dict:
    key = jax.random.key(seed)
    ks = iter(jax.random.split(key, 48))
    f32 = jnp.float32
    L = DEPTH
    n_pages = PAST_LEN // PAGE_SIZE
    n_used = DEC_BATCH * n_pages
    n_phys = n_used + max(1, n_used // 4)

    def nrm(shape, scale):
        return jax.random.normal(next(ks), shape, f32) * scale

    def uni(shape, lo, hi):
        return jax.random.uniform(next(ks), shape, f32, lo, hi)

    page_table = jax.random.permutation(next(ks), n_phys)[:n_used].reshape(DEC_BATCH, n_pages).astype(jnp.int32)
    return {
        "x_prompt": nrm((BATCH, SEQ, D_MODEL), 1.0),
        "x_sample": nrm((DEC_BATCH, DEC_SEQ, D_MODEL), 1.0),
        "cache_k": nrm((L, n_phys, PAGE_SIZE, H_SB, HEAD_DIM), 1.0),
        "cache_v": nrm((L, n_phys, PAGE_SIZE, H_SB, HEAD_DIM), 1.0),
        "state_wkv": nrm((L, DEC_BATCH, H_RWKV, HEAD_DIM, HEAD_DIM), 0.5),
        "state_shift": nrm((L, DEC_BATCH, D_MODEL), 1.0),
        "page_table": page_table,
        "ln1_g": 1.0 + nrm((L, D_MODEL), 0.02),
        "ln1_b": nrm((L, D_MODEL), 0.02),
        "ffn1_w_in": nrm((L, D_MODEL, 2 * D_FF), D_MODEL ** -0.5),
        "ffn1_w_out": nrm((L, D_FF, D_MODEL), D_FF ** -0.5 * DN_BETA),
        "w_in": nrm((L, D_MODEL, W_IN_COLS), D_MODEL ** -0.5),
        "mu_rkv": uni((L, 3 * D_RWKV), 0.0, 1.0),
        "mu_wag": uni((L, 3, D_MODEL), 0.0, 1.0),
        "w0": uni((L, D_RWKV), -6.0, 0.0),
        "w_w1": nrm((L, D_MODEL, DECAY_LORA), D_MODEL ** -0.5),
        "w_w2": nrm((L, DECAY_LORA, D_RWKV), 0.1 * DECAY_LORA ** -0.5),
        "a0": nrm((L, D_RWKV), 0.1),
        "a_w1": nrm((L, D_MODEL, AAA_LORA), D_MODEL ** -0.5),
        "a_w2": nrm((L, AAA_LORA, D_RWKV), 0.1 * AAA_LORA ** -0.5),
        "g_w1": nrm((L, D_MODEL, GATE_LORA), D_MODEL ** -0.5),
        "g_w2": nrm((L, GATE_LORA, D_RWKV), GATE_LORA ** -0.5),
        "k_k": 0.85 + nrm((L, D_RWKV), 0.02),
        "k_a": 1.0 + nrm((L, D_RWKV), 0.02),
        "r_k": nrm((L, H_RWKV, HEAD_DIM), 0.1),
        "lnx_g": 1.0 + nrm((L, D_RWKV), 0.02),
        "lnx_b": nrm((L, D_RWKV), 0.02),
        "sb_bias": SB_BIAS_INIT + nrm((L, H_SB), 0.1),
        "w_branch": nrm((L, D_RWKV + D_SB, D_MODEL), D_RWKV ** -0.5),
        "w_out": nrm((L, D_MODEL, D_MODEL), D_MODEL ** -0.5 * DN_BETA),
        "ln2_g": 1.0 + nrm((L, D_MODEL), 0.02),
        "ln2_b": nrm((L, D_MODEL), 0.02),
        "ffn2_w_in": nrm((L, D_MODEL, 2 * D_FF), D_MODEL ** -0.5),
        "ffn2_w_out": nrm((L, D_FF, D_MODEL), D_FF ** -0.5 * DN_BETA),
        "ln3_g": 1.0 + nrm((L, D_MODEL), 0.02),
        "ln3_b": nrm((L, D_MODEL), 0.02),
    }


def reference(x_prompt, x_sample, cache_k, cache_v, state_wkv, state_shift, page_table,
              ln1_g, ln1_b, ffn1_w_in, ffn1_w_out, w_in, mu_rkv, mu_wag, w0, w_w1, w_w2,
              a0, a_w1, a_w2, g_w1, g_w2, k_k, k_a, r_k, lnx_g, lnx_b, sb_bias, w_branch, w_out,
              ln2_g, ln2_b, ffn2_w_in, ffn2_w_out, ln3_g, ln3_b):
    xp, xs = x_prompt, x_sample
    kp_l, vp_l, wp_l, hp_l, ks_l, vs_l, ws_l, hs_l = [], [], [], [], [], [], [], []
    for l in range(DEPTH):
        p = dict(ln1_g=ln1_g[l], ln1_b=ln1_b[l], ffn1_w_in=ffn1_w_in[l], ffn1_w_out=ffn1_w_out[l],
                 w_in=w_in[l], mu_rkv=mu_rkv[l], mu_wag=mu_wag[l], w0=w0[l], w_w1=w_w1[l], w_w2=w_w2[l],
                 a0=a0[l], a_w1=a_w1[l], a_w2=a_w2[l], g_w1=g_w1[l], g_w2=g_w2[l], k_k=k_k[l], k_a=k_a[l],
                 r_k=r_k[l], lnx_g=lnx_g[l], lnx_b=lnx_b[l], sb_bias=sb_bias[l], w_branch=w_branch[l],
                 w_out=w_out[l], ln2_g=ln2_g[l], ln2_b=ln2_b[l], ffn2_w_in=ffn2_w_in[l],
                 ffn2_w_out=ffn2_w_out[l], ln3_g=ln3_g[l], ln3_b=ln3_b[l])
        wkv0_p = jnp.zeros((xp.shape[0], H_RWKV, HEAD_DIM, HEAD_DIM), state_wkv.dtype)
        xp, kp, vp, wp, hp = trunk_layer(xp, jnp.zeros_like(xp[:, 0]), wkv0_p, sb_prompt, p)
        sb_fn = functools.partial(sb_sample, cache_k[l], cache_v[l], page_table)
        xs, ks_, vs_, ws_, hs_ = trunk_layer(xs, state_shift[l].astype(xs.dtype), state_wkv[l], sb_fn, p)
        kp_l.append(kp); vp_l.append(vp); wp_l.append(wp); hp_l.append(hp)
        ks_l.append(ks_); vs_l.append(vs_); ws_l.append(ws_); hs_l.append(hs_)
    return (xp, xs,
            jnp.stack(kp_l), jnp.stack(vp_l), jnp.stack(wp_l), jnp.stack(hp_l),
            jnp.stack(ks_l), jnp.stack(vs_l), jnp.stack(ws_l), jnp.stack(hs_l))
```

```python
import functools

import jax
import jax.numpy as jnp
from jax import lax
from jax.experimental import pallas as pl
from jax.experimental.pallas import tpu as pltpu

F32 = jnp.float32
BF16 = jnp.bfloat16

HEAD_DIM = 64
PAIR = 2 * HEAD_DIM
CHUNK = 64
LN_EPS = 1e-5
LNX_EPS = 64e-5
SB_SCALE = HEAD_DIM ** -0.5

ROW_TILE = 512
SCAN_ROWS = 512
SB_TILE = 256
SB_PAGES = 8
VMEM_LIMIT = 56 * 1024 * 1024


def _dot(a, b):
    return jnp.dot(a.astype(BF16), b.astype(BF16), preferred_element_type=F32)


def _dot_nt(a, b):
    return lax.dot_general(a.astype(BF16), b.astype(BF16), (((1,), (1,)), ((), ())),
                           preferred_element_type=F32)


def _split(x):
    hi = x.astype(BF16)
    lo = (x - hi.astype(F32)).astype(BF16)
    return hi, lo


def _dot_hl(a, b_exact):
    hi, lo = _split(a)
    return (jnp.dot(hi, b_exact, preferred_element_type=F32)
            + jnp.dot(lo, b_exact, preferred_element_type=F32))


def _dot3(a, b, nt=False):
    ah, al = _split(a)
    bh, bl = _split(b)
    if nt:
        d = lambda x, y: lax.dot_general(x, y, (((1,), (1,)), ((), ())), preferred_element_type=F32)
    else:
        d = lambda x, y: jnp.dot(x, y, preferred_element_type=F32)
    return d(ah, bh) + d(al, bh) + d(ah, bl)


def _softplus(u):
    return jnp.maximum(u, 0.0) + jnp.log1p(jnp.exp(-jnp.abs(u)))


def _layer_norm(y, g, b):
    mu = jnp.mean(y, axis=-1, keepdims=True)
    yc = y - mu
    var = jnp.mean(yc * yc, axis=-1, keepdims=True)
    return yc * lax.rsqrt(var + LN_EPS) * g + b


def _full_spec(a, grid_rank):
    nd = a.ndim
    if grid_rank == 1:
        return pl.BlockSpec(a.shape, lambda i: (0,) * nd)
    if grid_rank == 2:
        return pl.BlockSpec(a.shape, lambda i, j: (0,) * nd)
    return pl.BlockSpec(a.shape, lambda i, j, k: (0,) * nd)


def _ffn_ln_kernel(x_ref, wg_ref, wu_ref, wo_ref, g_ref, b_ref, o_ref, acc_ref, *, alpha, n_ff):
    j = pl.program_id(1)

    @pl.when(j == 0)
    def _():
        acc_ref[...] = jnp.zeros_like(acc_ref)

    xb = x_ref[...].astype(BF16)
    gate = jnp.dot(xb, wg_ref[...], preferred_element_type=F32)
    up = jnp.dot(xb, wu_ref[...], preferred_element_type=F32)
    mid = gate * jax.nn.sigmoid(gate) * up
    acc_ref[...] += jnp.dot(mid.astype(BF16), wo_ref[...], preferred_element_type=F32)

    @pl.when(j == n_ff - 1)
    def _():
        y = alpha * x_ref[...] + 0.5 * acc_ref[...]
        o_ref[...] = _layer_norm(y, g_ref[...], b_ref[...])


def _ffn_tile(d_ff):
    for n in (2, 1, 11, 22):
        if d_ff % n == 0 and (d_ff // n) % 128 == 0:
            return d_ff // n
    return d_ff


def ffn_ln(x, w_in_bf, w_out_bf, g, b, alpha):
    n, d = x.shape
    d_ff = w_out_bf.shape[0]
    tm = min(ROW_TILE, n)
    tf = _ffn_tile(d_ff)
    n_ff = d_ff // tf
    return pl.pallas_call(
        functools.partial(_ffn_ln_kernel, alpha=alpha, n_ff=n_ff),
        grid=(n // tm, n_ff),
        in_specs=[
            pl.BlockSpec((tm, d), lambda i, j: (i, 0)),
            pl.BlockSpec((d, tf), lambda i, j: (0, j)),
            pl.BlockSpec((d, tf), lambda i, j: (0, j + n_ff)),
            pl.BlockSpec((tf, d), lambda i, j: (j, 0)),
            pl.BlockSpec((1, d), lambda i, j: (0, 0)),
            pl.BlockSpec((1, d), lambda i, j: (0, 0)),
        ],
        out_specs=pl.BlockSpec((tm, d), lambda i, j: (i, 0)),
        out_shape=jax.ShapeDtypeStruct((n, d), F32),
        scratch_shapes=[pltpu.VMEM((tm, d), F32)],
        compiler_params=pltpu.CompilerParams(
            dimension_semantics=("arbitrary", "arbitrary"), vmem_limit_bytes=VMEM_LIMIT),
        name="ffn_ln",
    )(x, w_in_bf, w_in_bf, w_out_bf, g.reshape(1, d), b.reshape(1, d))


N_PREP_PARAMS = 14


def _rwkv_prep_math(h, h_prev, params, outs):
    (wrkv_ref, ww1_ref, ww2_ref, aw1_ref, aw2_ref, gw1_ref, gw2_ref,
     mu_rkv_ref, mu_wag_ref, w0_ref, a0_ref, kk_ref, ka_ref, e_ref) = params
    r_out, lw_out, k_out, a_out, b_out, v_out, g_out = outs
    d_r = w0_ref.shape[-1]
    p = jnp.dot(h.astype(BF16), wrkv_ref[...], preferred_element_type=F32)
    p_prev = jnp.dot(h_prev.astype(BF16), wrkv_ref[...], preferred_element_type=F32)
    rkv = p + (p_prev - p) * mu_rkv_ref[...]
    r = rkv[:, :d_r]
    k = rkv[:, d_r:2 * d_r]
    v = rkv[:, 2 * d_r:]
    dx = h_prev - h
    xw = h + dx * mu_wag_ref[0:1, :]
    xa = h + dx * mu_wag_ref[1:2, :]
    xg = h + dx * mu_wag_ref[2:3, :]
    lw = _dot(jnp.tanh(_dot(xw, ww1_ref[...])), ww2_ref[...])
    w_log = -_softplus(-(w0_ref[...] + lw)) - 0.5
    a_gate = jax.nn.sigmoid(a0_ref[...] + _dot(_dot(xa, aw1_ref[...]), aw2_ref[...]))
    g = _dot(jax.nn.sigmoid(_dot(xg, gw1_ref[...])), gw2_ref[...])
    kk = k * kk_ref[...]
    ss = _dot_hl(kk * kk, e_ref[...])
    kk = kk / jnp.maximum(jnp.sqrt(ss), 1e-12)
    r_out[...] = r
    lw_out[...] = -jnp.exp(w_log)
    k_out[...] = k * (1.0 + (a_gate - 1.0) * ka_ref[...])
    a_out[...] = -kk
    b_out[...] = kk * a_gate
    v_out[...] = v
    g_out[...] = g


def _rwkv_prep_prompt_kernel(h_ref, *refs, tiles_per_seq):
    params, outs, carry_ref = refs[:N_PREP_PARAMS], refs[N_PREP_PARAMS:N_PREP_PARAMS + 7], refs[-1]
    i = pl.program_id(0)
    h = h_ref[...]
    tm = h.shape[0]

    @pl.when(i % tiles_per_seq == 0)
    def _():
        carry_ref[...] = jnp.zeros_like(carry_ref)

    row = lax.broadcasted_iota(jnp.int32, (tm, 1), 0)
    h_prev = jnp.where(row == 0, carry_ref[7:8, :], pltpu.roll(h, 1, 0))
    _rwkv_prep_math(h, h_prev, params, outs)
    carry_ref[...] = h[tm - 8:, :]


def _rwkv_prep_sample_kernel(h_ref, hl_ref, *refs, seq):
    params, outs = refs[:N_PREP_PARAMS], refs[N_PREP_PARAMS:N_PREP_PARAMS + 7]
    h = h_ref[...]
    tm = h.shape[0]
    row = lax.broadcasted_iota(jnp.int32, (tm, 1), 0)
    h_prev = jnp.where(row % seq == 0, hl_ref[...], pltpu.roll(h, 1, 0))
    _rwkv_prep_math(h, h_prev, params, outs)


def rwkv_prep(h, h_last_rows, seq, prm):
    n, d = h.shape
    d_r = prm["w0"].shape[-1]
    plist = [prm["w_rkv"], prm["w_w1"], prm["w_w2"], prm["a_w1"], prm["a_w2"], prm["g_w1"], prm["g_w2"],
             prm["mu_rkv"], prm["mu_wag"], prm["w0"], prm["a0"], prm["k_k"], prm["k_a"], prm["e_head"]]
    assert len(plist) == N_PREP_PARAMS
    if h_last_rows is None:
        tm = min(ROW_TILE, seq)
        kern = functools.partial(_rwkv_prep_prompt_kernel, tiles_per_seq=seq // tm)
        args = [h] + plist
        in_specs = [pl.BlockSpec((tm, d), lambda i: (i, 0))] + [_full_spec(a, 1) for a in plist]
        scratch = [pltpu.VMEM((8, d), F32)]
    else:
        tm = n
        kern = functools.partial(_rwkv_prep_sample_kernel, seq=seq)
        args = [h, h_last_rows] + plist
        in_specs = [pl.BlockSpec((tm, d), lambda i: (i, 0))] * 2 + [_full_spec(a, 1) for a in plist]
        scratch = []
    return pl.pallas_call(
        kern,
        grid=(n // tm,),
        in_specs=in_specs,
        out_specs=[pl.BlockSpec((tm, d_r), lambda i: (i, 0))] * 7,
        out_shape=[jax.ShapeDtypeStruct((n, d_r), F32)] * 7,
        scratch_shapes=scratch,
        compiler_params=pltpu.CompilerParams(
            dimension_semantics=("arbitrary",), vmem_limit_bytes=VMEM_LIMIT),
        name="rwkv_prep",
    )(*args)


def _sb_proj_kernel(h_ref, w_ref, q_out, k_out, v_out, kb_out, vb_out):
    d_s = k_out.shape[-1]
    p = jnp.dot(h_ref[...].astype(BF16), w_ref[...], preferred_element_type=F32)
    q_out[...] = (p[:, :d_s] * SB_SCALE).astype(BF16)
    k = p[:, d_s:2 * d_s]
    v = p[:, 2 * d_s:]
    k_out[...] = k
    v_out[...] = v
    kb_out[...] = k.astype(BF16)
    vb_out[...] = v.astype(BF16)


def sb_proj(h, w_sb_bf):
    n, d = h.shape
    d_s = w_sb_bf.shape[1] // 3
    tm = min(ROW_TILE, n)
    blk = lambda: pl.BlockSpec((tm, d_s), lambda i: (i, 0))
    return pl.pallas_call(
        _sb_proj_kernel,
        grid=(n // tm,),
        in_specs=[pl.BlockSpec((tm, d), lambda i: (i, 0)), _full_spec(w_sb_bf, 1)],
        out_specs=[blk() for _ in range(5)],
        out_shape=[jax.ShapeDtypeStruct((n, d_s), BF16), jax.ShapeDtypeStruct((n, d_s), F32),
                   jax.ShapeDtypeStruct((n, d_s), F32), jax.ShapeDtypeStruct((n, d_s), BF16),
                   jax.ShapeDtypeStruct((n, d_s), BF16)],
        compiler_params=pltpu.CompilerParams(
            dimension_semantics=("arbitrary",), vmem_limit_bytes=VMEM_LIMIT),
        name="sb_proj",
    )(h, w_sb_bf)


def _stack_heads(x, m0):
    return jnp.concatenate([jnp.where(m0, x, 0.0), jnp.where(m0, 0.0, x)], axis=0)


def _rwkv_scan_kernel(r_ref, lw_ref, k_ref, a_ref, b_ref, v_ref, g_ref, s0_ref,
                      rk_ref, lg_ref, lb_ref, tri_ref, emean_ref, eones_ref,
                      o_ref, sfin_ref, s_ref, *, n_chunks):
    i = pl.program_id(2)
    c2 = 2 * CHUNK

    @pl.when(i == 0)
    def _():
        s_ref[...] = s0_ref[...]

    lane = lax.broadcasted_iota(jnp.int32, (1, PAIR), 1)
    m0 = lane < HEAD_DIM
    rr = lax.broadcasted_iota(jnp.int32, (c2, c2), 0)
    cc = lax.broadcasted_iota(jnp.int32, (c2, c2), 1)
    strict = cc < rr
    incl = cc <= rr
    eye = (cc == rr).astype(F32)
    tri = tri_ref[...]

    def chunk(c, carry):
        rows = pl.ds(pl.multiple_of(c * CHUNK, CHUNK), CHUNK)
        r = r_ref[rows, :]
        lw = lw_ref[rows, :]
        k = k_ref[rows, :]
        a = a_ref[rows, :]
        b = b_ref[rows, :]
        v = v_ref[rows, :]
        hi, lo = _split(lw)
        lo2 = (lw - hi.astype(F32) - lo.astype(F32)).astype(BF16)
        cl = (jnp.dot(tri, hi, preferred_element_type=F32) + jnp.dot(tri, lo, preferred_element_type=F32)
              + jnp.dot(tri, lo2, preferred_element_type=F32))
        w_in = jnp.exp(cl)
        w_inv = jnp.exp(-cl)
        w_ex = jnp.exp(cl - lw)
        cl_end = cl[CHUNK - 1:CHUNK, :]
        w_end = jnp.exp(cl_end)
        w_tail = jnp.exp(cl_end - cl)
        ast = _stack_heads(a * w_ex, m0)
        rst = _stack_heads(r * w_in, m0)
        bst = _stack_heads(b * w_inv, m0)
        kst = _stack_heads(k * w_inv, m0)
        vst = _stack_heads(v, m0)
        btl = _stack_heads(b * w_tail, m0)
        ktl = _stack_heads(k * w_tail, m0)

        nab = jnp.where(strict, _dot3(ast, bst, nt=True), 0.0)
        mak = jnp.where(strict, _dot3(ast, kst, nt=True), 0.0)
        mrb = jnp.where(incl, _dot3(rst, bst, nt=True), 0.0)
        mrk = jnp.where(incl, _dot3(rst, kst, nt=True), 0.0)

        t = eye + nab
        pw = nab
        for _ in range(5):
            pw = _dot3(pw, pw)
            t = t + _dot3(pw, t)

        s = s_ref[...]
        rhs = _dot3(ast, s, nt=True) + _dot3(mak, vst)
        u = _dot3(t, rhs)
        y = _dot3(rst, s, nt=True) + _dot3(mrb, u) + _dot3(mrk, vst)
        s_ref[...] = s * w_end + _dot3(u.T, btl) + _dot3(vst.T, ktl)

        yp = y[:CHUNK, :] + y[CHUNK:, :]
        mu = _dot_hl(yp, emean_ref[...])
        yc = yp - mu
        var = _dot_hl(yc * yc, emean_ref[...])
        yn = yc * lax.rsqrt(var + LNX_EPS) * lg_ref[...] + lb_ref[...]
        bonus = _dot_hl(r * k * rk_ref[...], eones_ref[...]) * v
        o_ref[rows, :] = (yn + bonus) * g_ref[rows, :]
        return carry

    lax.fori_loop(0, n_chunks, chunk, 0)
    sfin_ref[...] = s_ref[...]


def rwkv_scan(r, lw, k, a, b, v, g, s0, r_k, lnx_g, lnx_b, batch, seq):
    n, d_r = r.shape
    n_pairs = d_r // PAIR
    rows = min(SCAN_ROWS, seq)
    steps = seq // rows
    tri = jnp.tril(jnp.ones((CHUNK, CHUNK), F32)).astype(BF16)
    blk = jnp.arange(PAIR) // HEAD_DIM
    same = (blk[:, None] == blk[None, :]).astype(F32)
    emean = (same / HEAD_DIM).astype(BF16)
    eones = same.astype(BF16)
    tok = pl.BlockSpec((rows, PAIR), lambda bi, p, i: (bi * steps + i, p))
    par = pl.BlockSpec((1, PAIR), lambda bi, p, i: (0, p))
    st = pl.BlockSpec((None, None, PAIR, PAIR), lambda bi, p, i: (bi, p, 0, 0))
    return pl.pallas_call(
        functools.partial(_rwkv_scan_kernel, n_chunks=rows // CHUNK),
        grid=(batch, n_pairs, steps),
        in_specs=[tok] * 7 + [st, par, par, par, _full_spec(tri, 3), _full_spec(emean, 3), _full_spec(eones, 3)],
        out_specs=[tok, st],
        out_shape=[jax.ShapeDtypeStruct((n, d_r), F32),
                   jax.ShapeDtypeStruct((batch, n_pairs, PAIR, PAIR), F32)],
        scratch_shapes=[pltpu.VMEM((PAIR, PAIR), F32)],
        compiler_params=pltpu.CompilerParams(
            dimension_semantics=("arbitrary", "arbitrary", "arbitrary"), vmem_limit_bytes=VMEM_LIMIT),
        name="rwkv_scan",
    )(r, lw, k, a, b, v, g, s0, r_k.reshape(1, d_r), lnx_g.reshape(1, d_r), lnx_b.reshape(1, d_r),
      tri, emean, eones)


def _sb_prompt_kernel(bias_ref, q_ref, k_ref, v_ref, ux_ref, o_ref, acc0, acc1, c0, c1, *, tq):
    p = pl.program_id(1)
    i = pl.program_id(2)
    lane = lax.broadcasted_iota(jnp.int32, (1, PAIR), 1)
    m0 = lane < HEAD_DIM
    q = q_ref[...]
    zero = jnp.zeros_like(q)
    q_heads = (jnp.where(m0, q, zero), jnp.where(m0, zero, q))
    biases = (bias_ref[2 * p], bias_ref[2 * p + 1])
    accs = (acc0, acc1)
    cs = (c0, c1)
    for ref in accs + cs:
        ref[...] = jnp.zeros_like(ref)
    ux = ux_ref[...]
    rr = lax.broadcasted_iota(jnp.int32, (tq, tq), 0)
    cc = lax.broadcasted_iota(jnp.int32, (tq, tq), 1)
    causal = cc < rr

    def block(kb, diag):
        rows = pl.ds(pl.multiple_of(kb * tq, tq), tq)
        ks = k_ref[rows, :]
        vs = v_ref[rows, :]
        for qh, bh, acc, c in zip(q_heads, biases, accs, cs):
            z = lax.dot_general(qh, ks, (((1,), (1,)), ((), ())), preferred_element_type=F32) + bh
            sp = _softplus(z)
            l = -sp
            if diag:
                l = jnp.where(causal, l, 0.0)
            cx = _dot_hl(l, ux)
            cprev = c[...]
            e = (z - sp) + cx[:, :tq] + jnp.concatenate([cprev] * (tq // PAIR), axis=1)
            w = jnp.exp(e)
            if diag:
                w = jnp.where(causal, w, 0.0)
            acc[...] += jnp.dot(w.astype(BF16), vs, preferred_element_type=F32)
            c[...] = cprev + cx[:, tq:]

    block(i, True)

    def body(it, carry):
        block(i - 1 - it, False)
        return carry

    lax.fori_loop(0, i, body, 0)
    o_ref[...] = jnp.where(m0, acc0[...], acc1[...])


def sb_prompt(q_bf, k_bf, v_bf, bias, batch, seq):
    n, d_s = q_bf.shape
    n_pairs = d_s // PAIR
    tq = min(SB_TILE, seq)
    nq = seq // tq
    j = jnp.arange(tq)
    ux = jnp.concatenate([(j[:, None] > j[None, :]).astype(F32), jnp.ones((tq, PAIR), F32)], axis=1).astype(BF16)
    grid_spec = pltpu.PrefetchScalarGridSpec(
        num_scalar_prefetch=1,
        grid=(batch, n_pairs, nq),
        in_specs=[
            pl.BlockSpec((tq, PAIR), lambda b, p, i, bias_ref: (b * nq + i, p)),
            pl.BlockSpec((seq, PAIR), lambda b, p, i, bias_ref: (b, p)),
            pl.BlockSpec((seq, PAIR), lambda b, p, i, bias_ref: (b, p)),
            pl.BlockSpec(ux.shape, lambda b, p, i, bias_ref: (0, 0)),
        ],
        out_specs=pl.BlockSpec((tq, PAIR), lambda b, p, i, bias_ref: (b * nq + i, p)),
        scratch_shapes=[pltpu.VMEM((tq, PAIR), F32)] * 4,
    )
    return pl.pallas_call(
        functools.partial(_sb_prompt_kernel, tq=tq),
        grid_spec=grid_spec,
        out_shape=jax.ShapeDtypeStruct((n, d_s), F32),
        compiler_params=pltpu.CompilerParams(
            dimension_semantics=("arbitrary", "arbitrary", "arbitrary"), vmem_limit_bytes=VMEM_LIMIT),
        name="sb_prompt",
    )(bias, q_bf, k_bf, v_bf, ux)


def _sb_sample_kernel(pt_ref, q_ref, brow_ref, kn_ref, vn_ref, *refs, n_pg, n_heads):
    kps, vps = refs[:n_pg], refs[n_pg:2 * n_pg]
    ux_ref, o_ref, acc_ref, c_ref = refs[2 * n_pg:]
    j = pl.program_id(1)
    qbd = q_ref[...]
    nrow = qbd.shape[0]
    brow = brow_ref[...]
    ux = ux_ref[...]

    def process(z, valid, av):
        nb = z.shape[1] // PAIR
        z = z + jnp.concatenate([brow] * nb, axis=1)
        sp = _softplus(z)
        l = -sp
        if valid is not None:
            l = jnp.where(valid, l, 0.0)
        lst = jnp.concatenate([l[:, bi * PAIR:(bi + 1) * PAIR] for bi in range(nb)], axis=0)
        cx = _dot_hl(lst, ux)
        run = c_ref[...]
        parts = [None] * nb
        for bi in range(nb - 1, -1, -1):
            blk = cx[bi * nrow:(bi + 1) * nrow, :]
            parts[bi] = blk[:, :PAIR] + run
            run = run + blk[:, PAIR:]
        w = jnp.exp((z - sp) + jnp.concatenate(parts, axis=1))
        if valid is not None:
            w = jnp.where(valid, w, 0.0)
        acc_ref[...] += av(w.astype(BF16))
        c_ref[...] = run

    @pl.when(j == 0)
    def _():
        acc_ref[...] = jnp.zeros_like(acc_ref)
        c_ref[...] = jnp.zeros_like(c_ref)
        t_of_row = lax.broadcasted_iota(jnp.int32, (nrow, PAIR), 0) // n_heads
        key = lax.broadcasted_iota(jnp.int32, (nrow, PAIR), 1)
        process(_dot_nt(qbd, kn_ref[...]), key < t_of_row, lambda w: _dot(w, vn_ref[...]))

    d_s = qbd.shape[1]
    page_t = lambda r: r[...].reshape(d_s, PAIR).astype(BF16)
    z_pages = jnp.concatenate([_dot(qbd, page_t(r)) for r in kps], axis=1)

    def av_pages(w):
        out = _dot_nt(w[:, :PAIR], page_t(vps[0]))
        for pi in range(1, n_pg):
            out = out + _dot_nt(w[:, pi * PAIR:(pi + 1) * PAIR], page_t(vps[pi]))
        return out

    process(z_pages, None, av_pages)

    @pl.when(j == pl.num_programs(1) - 1)
    def _():
        d_s = acc_ref.shape[1]
        head_of_row = lax.broadcasted_iota(jnp.int32, (nrow, d_s), 0) % n_heads
        head_of_lane = lax.broadcasted_iota(jnp.int32, (nrow, d_s), 1) // HEAD_DIM
        sel = jnp.where(head_of_row == head_of_lane, acc_ref[...], 0.0)
        o_ref[...] = jnp.sum(sel.reshape(nrow // n_heads, n_heads, d_s), axis=1)


def sb_sample(q_bd, brow, k_new, v_new, cache_k, cache_v, page_table, n_heads):
    bsz, nrow, d_s = q_bd.shape
    n_pages = page_table.shape[1]
    page = cache_k.shape[3]
    assert page == PAIR
    n_pg = min(SB_PAGES, n_pages)
    steps = n_pages // n_pg
    jj = jnp.arange(PAIR)
    ux = jnp.concatenate([(jj[:, None] > jj[None, :]).astype(F32), jnp.ones((PAIR, PAIR), F32)], axis=1).astype(BF16)

    def page_spec(i):
        return pl.BlockSpec((None, n_heads, HEAD_DIM, page),
                            lambda b, j, pt: (pt[b, n_pages - n_pg * (j + 1) + i], 0, 0, 0))

    per_b = lambda shape: pl.BlockSpec((None,) + shape, lambda b, j, pt: (b, 0, 0))
    grid_spec = pltpu.PrefetchScalarGridSpec(
        num_scalar_prefetch=1,
        grid=(bsz, steps),
        in_specs=[per_b((nrow, d_s)), pl.BlockSpec(brow.shape, lambda b, j, pt: (0, 0)),
                  per_b((PAIR, d_s)), per_b((PAIR, d_s))]
                 + [page_spec(i) for i in range(n_pg)] * 2
                 + [pl.BlockSpec(ux.shape, lambda b, j, pt: (0, 0))],
        out_specs=per_b((nrow // n_heads, d_s)),
        scratch_shapes=[pltpu.VMEM((nrow, d_s), F32), pltpu.VMEM((nrow, PAIR), F32)],
    )
    return pl.pallas_call(
        functools.partial(_sb_sample_kernel, n_pg=n_pg, n_heads=n_heads),
        grid_spec=grid_spec,
        out_shape=jax.ShapeDtypeStruct((bsz, nrow // n_heads, d_s), F32),
        compiler_params=pltpu.CompilerParams(
            dimension_semantics=("arbitrary", "arbitrary"), vmem_limit_bytes=VMEM_LIMIT),
        name="sb_sample",
    )(page_table, q_bd, brow, k_new, v_new, *([cache_k] * n_pg), *([cache_v] * n_pg), ux)


def _merge_kernel(h_ref, oa_ref, ob_ref, wg_ref, wba_ref, wbb_ref, wo_ref, g_ref, b_ref, o_ref, *, alpha):
    h = h_ref[...]
    d = h.shape[1]
    gates = jax.nn.sigmoid(jnp.dot(h.astype(BF16), wg_ref[...], preferred_element_type=F32))
    ma = jnp.dot(oa_ref[...].astype(BF16), wba_ref[...], preferred_element_type=F32)
    mb = jnp.dot(ob_ref[...].astype(BF16), wbb_ref[...], preferred_element_type=F32)
    merged = gates[:, :d] * ma + gates[:, d:] * mb
    y = alpha * h + jnp.dot(merged.astype(BF16), wo_ref[...], preferred_element_type=F32)
    o_ref[...] = _layer_norm(y, g_ref[...], b_ref[...])


def merge(h, o_a, o_b, w_gate_bf, w_ba_bf, w_bb_bf, w_out_bf, g, b, alpha):
    n, d = h.shape
    tm = min(ROW_TILE, n)
    row = lambda a: pl.BlockSpec((tm, a.shape[1]), lambda i: (i, 0))
    ws = [w_gate_bf, w_ba_bf, w_bb_bf, w_out_bf, g.reshape(1, d), b.reshape(1, d)]
    return pl.pallas_call(
        functools.partial(_merge_kernel, alpha=alpha),
        grid=(n // tm,),
        in_specs=[row(h), row(o_a), row(o_b)] + [_full_spec(a, 1) for a in ws],
        out_specs=pl.BlockSpec((tm, d), lambda i: (i, 0)),
        out_shape=jax.ShapeDtypeStruct((n, d), F32),
        compiler_params=pltpu.CompilerParams(
            dimension_semantics=("arbitrary",), vmem_limit_bytes=VMEM_LIMIT),
        name="merge",
    )(h, o_a, o_b, *ws)


def _pair_blockdiag(s):
    bsz, n_h = s.shape[:2]
    s = s.reshape(bsz, n_h // 2, 2, HEAD_DIM, HEAD_DIM)
    eye = jnp.eye(2, dtype=s.dtype)
    out = s[:, :, :, :, None, :] * eye[None, None, :, None, :, None]
    return out.reshape(bsz, n_h // 2, PAIR, PAIR)


def _pair_unblock(sbd):
    bsz, n_p = sbd.shape[:2]
    s = sbd.reshape(bsz, n_p, 2, HEAD_DIM, 2, HEAD_DIM)
    return jnp.stack([s[:, :, 0, :, 0, :], s[:, :, 1, :, 1, :]], axis=2).reshape(
        bsz, 2 * n_p, HEAD_DIM, HEAD_DIM)


def _layer(x, batch, seq, h_last, wkv0, sb_fn, w, alpha):
    n, d = x.shape
    h = ffn_ln(x, w["ffn1_in"], w["ffn1_out"], w["ln1_g"], w["ln1_b"], alpha)
    if h_last is None:
        hl_rows = None
        rows_pad = seq
    else:
        hl_rows = jnp.repeat(h_last, seq, axis=0)
        rows_pad = CHUNK
    r, lw, k, a, b, v, g = rwkv_prep(h, hl_rows, seq, w)
    n_h = w["w0"].shape[-1] // HEAD_DIM
    if wkv0 is None:
        s0 = jnp.zeros((batch, n_h // 2, PAIR, PAIR), F32)
    else:
        s0 = _pair_blockdiag(wkv0.astype(F32))
    scan_in = [r, lw, k, a, b, v, g]
    if rows_pad != seq:
        pad = lambda t: jnp.pad(t.reshape(batch, seq, -1), ((0, 0), (0, rows_pad - seq), (0, 0))).reshape(
            batch * rows_pad, -1)
        scan_in = [pad(t) for t in scan_in]
    o_a, s_fin = rwkv_scan(*scan_in, s0, w["r_k"], w["lnx_g"], w["lnx_b"], batch, rows_pad)
    if rows_pad != seq:
        o_a = o_a.reshape(batch, rows_pad, -1)[:, :seq].reshape(n, -1)
    wkv = _pair_unblock(s_fin)
    q_bf, k_sb, v_sb, k_bf, v_bf = sb_proj(h, w["w_sb"])
    o_b = sb_fn(q_bf, k_sb, v_sb, k_bf, v_bf)
    x2 = merge(h, o_a, o_b, w["w_gate"], w["w_ba"], w["w_bb"], w["w_out"], w["ln2_g"], w["ln2_b"], alpha)
    x3 = ffn_ln(x2, w["ffn2_in"], w["ffn2_out"], w["ln3_g"], w["ln3_b"], alpha)
    return x3, k_sb, v_sb, wkv, h.reshape(batch, seq, d)[:, -1]


def kernel(x_prompt, x_sample, cache_k, cache_v, state_wkv, state_shift, page_table, ln1_g, ln1_b, ffn1_w_in, ffn1_w_out, w_in, mu_rkv, mu_wag, w0, w_w1, w_w2, a0, a_w1, a_w2, g_w1, g_w2, k_k, k_a, r_k, lnx_g, lnx_b, sb_bias, w_branch, w_out, ln2_g, ln2_b, ffn2_w_in, ffn2_w_out, ln3_g, ln3_b):
    depth = ln1_g.shape[0]
    alpha = (2.0 * depth) ** 0.25
    bp, tp, d = x_prompt.shape
    bs, ts, _ = x_sample.shape
    d_r = w0.shape[-1]
    d_s = (w_in.shape[-1] - 3 * d_r - 2 * d) // 3
    n_hs = d_s // HEAD_DIM
    blk = jnp.arange(d_r) // HEAD_DIM
    e_head = (blk[:, None] == blk[None, :]).astype(BF16)

    xp = x_prompt.reshape(bp * tp, d)
    xs = x_sample.reshape(bs * ts, d)
    outs = [[] for _ in range(8)]
    for l in range(depth):
        row = lambda a: a[l].reshape(1, -1)
        w = dict(
            ffn1_in=ffn1_w_in[l].astype(BF16), ffn1_out=ffn1_w_out[l].astype(BF16),
            ffn2_in=ffn2_w_in[l].astype(BF16), ffn2_out=ffn2_w_out[l].astype(BF16),
            ln1_g=ln1_g[l], ln1_b=ln1_b[l], ln2_g=ln2_g[l], ln2_b=ln2_b[l], ln3_g=ln3_g[l], ln3_b=ln3_b[l],
            w_rkv=w_in[l][:, :3 * d_r].astype(BF16),
            w_sb=w_in[l][:, 3 * d_r:3 * d_r + 3 * d_s].astype(BF16),
            w_gate=w_in[l][:, 3 * d_r + 3 * d_s:].astype(BF16),
            w_w1=w_w1[l].astype(BF16), w_w2=w_w2[l].astype(BF16), a_w1=a_w1[l].astype(BF16),
            a_w2=a_w2[l].astype(BF16), g_w1=g_w1[l].astype(BF16), g_w2=g_w2[l].astype(BF16),
            mu_rkv=row(mu_rkv), mu_wag=mu_wag[l], w0=row(w0), a0=row(a0), k_k=row(k_k), k_a=row(k_a),
            e_head=e_head, r_k=r_k[l].reshape(-1), lnx_g=lnx_g[l], lnx_b=lnx_b[l],
            w_ba=w_branch[l][:d_r].astype(BF16), w_bb=w_branch[l][d_r:].astype(BF16),
            w_out=w_out[l].astype(BF16),
        )
        bias = sb_bias[l].astype(F32)

        def prompt_sb(q_bf, k_sb, v_sb, k_bf, v_bf):
            return sb_prompt(q_bf, k_bf, v_bf, bias, bp, tp)

        def sample_sb(q_bf, k_sb, v_sb, k_bf, v_bf):
            head_of_lane = jnp.arange(d_s) // HEAD_DIM
            onehot = (jnp.arange(n_hs)[:, None] == head_of_lane[None, :])
            q_bd = jnp.where(onehot[None, None], q_bf.reshape(bs, ts, 1, d_s), jnp.zeros((), BF16))
            q_bd = q_bd.reshape(bs, ts * n_hs, d_s)
            brow = jnp.broadcast_to(jnp.tile(bias, ts)[:, None], (ts * n_hs, PAIR))
            padk = lambda t: jnp.pad(t.reshape(bs, ts, d_s), ((0, 0), (0, PAIR - ts), (0, 0)))
            ck = jnp.transpose(cache_k[l], (0, 2, 3, 1))
            cv = jnp.transpose(cache_v[l], (0, 2, 3, 1))
            o = sb_sample(q_bd, brow, padk(k_sb), padk(v_sb), ck, cv, page_table, n_hs)
            return o.reshape(bs * ts, d_s)

        xp, kp, vp, wp, hp = _layer(xp, bp, tp, None, None, prompt_sb, w, alpha)
        xs, ks_, vs_, ws_, hs_ = _layer(xs, bs, ts, state_shift[l].astype(F32), state_wkv[l], sample_sb, w, alpha)
        for lst, val in zip(outs, (kp.reshape(bp, tp, n_hs, HEAD_DIM), vp.reshape(bp, tp, n_hs, HEAD_DIM), wp, hp,
                                   ks_.reshape(bs, ts, n_hs, HEAD_DIM), vs_.reshape(bs, ts, n_hs, HEAD_DIM),
                                   ws_.astype(state_wkv.dtype), hs_)):
            lst.append(val)
    return (xp.reshape(bp, tp, d), xs.reshape(bs, ts, d)) + tuple(jnp.stack(o) for o in outs)
```

```python
import functools

import jax
import jax.numpy as jnp
from jax import lax
from jax.experimental import pallas as pl
from jax.experimental.pallas import tpu as pltpu

F32 = jnp.float32
BF16 = jnp.bfloat16

HEAD_DIM = 64
PAIR = 2 * HEAD_DIM
CHUNK = 64
LN_EPS = 1e-5
LNX_EPS = 64e-5
SB_SCALE = HEAD_DIM ** -0.5

ROW_TILE = 512
SCAN_ROWS = 512
SB_TILE = 512
SB_KEYS = 256
SB_PAGES = 8
VMEM_LIMIT = 56 * 1024 * 1024


def _dot(a, b):
    return jnp.dot(a.astype(BF16), b.astype(BF16), preferred_element_type=F32)


def _dot_nt(a, b):
    return lax.dot_general(a.astype(BF16), b.astype(BF16), (((1,), (1,)), ((), ())),
                           preferred_element_type=F32)


def _split(x):
    hi = x.astype(BF16)
    lo = (x - hi.astype(F32)).astype(BF16)
    return hi, lo


def _dot_hl(a, b_exact):
    hi, lo = _split(a)
    return (jnp.dot(hi, b_exact, preferred_element_type=F32)
            + jnp.dot(lo, b_exact, preferred_element_type=F32))


def _dot3(a, b, nt=False):
    ah, al = _split(a)
    bh, bl = _split(b)
    if nt:
        d = lambda x, y: lax.dot_general(x, y, (((1,), (1,)), ((), ())), preferred_element_type=F32)
    else:
        d = lambda x, y: jnp.dot(x, y, preferred_element_type=F32)
    return d(ah, bh) + d(al, bh) + d(ah, bl)


def _softplus(u):
    return jnp.maximum(u, 0.0) + jnp.log1p(jnp.exp(-jnp.abs(u)))


def _layer_norm(y, g, b):
    mu = jnp.mean(y, axis=-1, keepdims=True)
    yc = y - mu
    var = jnp.mean(yc * yc, axis=-1, keepdims=True)
    return yc * lax.rsqrt(var + LN_EPS) * g + b


def _full_spec(a, grid_rank):
    nd = a.ndim
    if grid_rank == 1:
        return pl.BlockSpec(a.shape, lambda i: (0,) * nd)
    if grid_rank == 2:
        return pl.BlockSpec(a.shape, lambda i, j: (0,) * nd)
    return pl.BlockSpec(a.shape, lambda i, j, k: (0,) * nd)


def _ffn_ln_kernel(x_ref, wg_ref, wu_ref, wo_ref, g_ref, b_ref, o_ref, acc_ref, *, alpha, n_ff):
    j = pl.program_id(1)

    @pl.when(j == 0)
    def _():
        acc_ref[...] = jnp.zeros_like(acc_ref)

    xb = x_ref[...].astype(BF16)
    gate = jnp.dot(xb, wg_ref[...], preferred_element_type=F32)
    up = jnp.dot(xb, wu_ref[...], preferred_element_type=F32)
    mid = gate * jax.nn.sigmoid(gate) * up
    acc_ref[...] += jnp.dot(mid.astype(BF16), wo_ref[...], preferred_element_type=F32)

    @pl.when(j == n_ff - 1)
    def _():
        y = alpha * x_ref[...] + 0.5 * acc_ref[...]
        o_ref[...] = _layer_norm(y, g_ref[...], b_ref[...])


def _ffn_tile(d_ff):
    for n in (2, 1, 11, 22):
        if d_ff % n == 0 and (d_ff // n) % 128 == 0:
            return d_ff // n
    return d_ff


def ffn_ln(x, w_in_bf, w_out_bf, g, b, alpha):
    n, d = x.shape
    d_ff = w_out_bf.shape[0]
    tm = min(ROW_TILE, n)
    tf = _ffn_tile(d_ff)
    n_ff = d_ff // tf
    return pl.pallas_call(
        functools.partial(_ffn_ln_kernel, alpha=alpha, n_ff=n_ff),
        grid=(n // tm, n_ff),
        in_specs=[
            pl.BlockSpec((tm, d), lambda i, j: (i, 0)),
            pl.BlockSpec((d, tf), lambda i, j: (0, j)),
            pl.BlockSpec((d, tf), lambda i, j: (0, j + n_ff)),
            pl.BlockSpec((tf, d), lambda i, j: (j, 0)),
            pl.BlockSpec((1, d), lambda i, j: (0, 0)),
            pl.BlockSpec((1, d), lambda i, j: (0, 0)),
        ],
        out_specs=pl.BlockSpec((tm, d), lambda i, j: (i, 0)),
        out_shape=jax.ShapeDtypeStruct((n, d), F32),
        scratch_shapes=[pltpu.VMEM((tm, d), F32)],
        compiler_params=pltpu.CompilerParams(
            dimension_semantics=("arbitrary", "arbitrary"), vmem_limit_bytes=VMEM_LIMIT),
        name="ffn_ln",
    )(x, w_in_bf, w_in_bf, w_out_bf, g.reshape(1, d), b.reshape(1, d))


N_PREP_PARAMS = 14


def _rwkv_prep_math(h, h_prev, p, p_prev, params, outs):
    (wrkv_ref, ww1_ref, ww2_ref, aw1_ref, aw2_ref, gw1_ref, gw2_ref,
     mu_rkv_ref, mu_wag_ref, w0_ref, a0_ref, kk_ref, ka_ref, e_ref) = params
    r_out, lw_out, k_out, a_out, b_out, v_out, g_out = outs
    d_r = w0_ref.shape[-1]
    rkv = p + (p_prev - p) * mu_rkv_ref[...]
    r = rkv[:, :d_r]
    k = rkv[:, d_r:2 * d_r]
    v = rkv[:, 2 * d_r:]
    dx = h_prev - h
    xw = h + dx * mu_wag_ref[0:1, :]
    xa = h + dx * mu_wag_ref[1:2, :]
    xg = h + dx * mu_wag_ref[2:3, :]
    lw = _dot(jnp.tanh(_dot(xw, ww1_ref[...])), ww2_ref[...])
    w_log = -_softplus(-(w0_ref[...] + lw)) - 0.5
    a_gate = jax.nn.sigmoid(a0_ref[...] + _dot(_dot(xa, aw1_ref[...]), aw2_ref[...]))
    g = _dot(jax.nn.sigmoid(_dot(xg, gw1_ref[...])), gw2_ref[...])
    kk = k * kk_ref[...]
    ss = _dot_hl(kk * kk, e_ref[...])
    kk = kk / jnp.maximum(jnp.sqrt(ss), 1e-12)
    r_out[...] = r
    lw_out[...] = -jnp.exp(w_log)
    k_out[...] = k * (1.0 + (a_gate - 1.0) * ka_ref[...])
    a_out[...] = -kk
    b_out[...] = kk * a_gate
    v_out[...] = v
    g_out[...] = g


def _rwkv_prep_prompt_kernel(h_ref, *refs, tiles_per_seq):
    params, outs = refs[:N_PREP_PARAMS], refs[N_PREP_PARAMS:N_PREP_PARAMS + 7]
    hcarry_ref, pcarry_ref = refs[-2:]
    i = pl.program_id(0)
    h = h_ref[...]
    tm = h.shape[0]

    @pl.when(i % tiles_per_seq == 0)
    def _():
        hcarry_ref[...] = jnp.zeros_like(hcarry_ref)
        pcarry_ref[...] = jnp.zeros_like(pcarry_ref)

    row = lax.broadcasted_iota(jnp.int32, (tm, 1), 0)
    p = jnp.dot(h.astype(BF16), params[0][...], preferred_element_type=F32)
    h_prev = jnp.where(row == 0, hcarry_ref[7:8, :], pltpu.roll(h, 1, 0))
    p_prev = jnp.where(row == 0, pcarry_ref[7:8, :], pltpu.roll(p, 1, 0))
    _rwkv_prep_math(h, h_prev, p, p_prev, params, outs)
    hcarry_ref[...] = h[tm - 8:, :]
    pcarry_ref[...] = p[tm - 8:, :]


def _rwkv_prep_sample_kernel(h_ref, hl_ref, *refs, seq):
    params, outs = refs[:N_PREP_PARAMS], refs[N_PREP_PARAMS:N_PREP_PARAMS + 7]
    h = h_ref[...]
    tm = h.shape[0]
    row = lax.broadcasted_iota(jnp.int32, (tm, 1), 0)
    first = row % seq == 0
    p = jnp.dot(h.astype(BF16), params[0][...], preferred_element_type=F32)
    p_last = jnp.dot(hl_ref[...].astype(BF16), params[0][...], preferred_element_type=F32)
    h_prev = jnp.where(first, hl_ref[...], pltpu.roll(h, 1, 0))
    p_prev = jnp.where(first, p_last, pltpu.roll(p, 1, 0))
    _rwkv_prep_math(h, h_prev, p, p_prev, params, outs)


def rwkv_prep(h, h_last_rows, seq, prm):
    n, d = h.shape
    d_r = prm["w0"].shape[-1]
    plist = [prm["w_rkv"], prm["w_w1"], prm["w_w2"], prm["a_w1"], prm["a_w2"], prm["g_w1"], prm["g_w2"],
             prm["mu_rkv"], prm["mu_wag"], prm["w0"], prm["a0"], prm["k_k"], prm["k_a"], prm["e_head"]]
    assert len(plist) == N_PREP_PARAMS
    if h_last_rows is None:
        tm = min(ROW_TILE, seq)
        kern = functools.partial(_rwkv_prep_prompt_kernel, tiles_per_seq=seq // tm)
        args = [h] + plist
        in_specs = [pl.BlockSpec((tm, d), lambda i: (i, 0))] + [_full_spec(a, 1) for a in plist]
        scratch = [pltpu.VMEM((8, d), F32), pltpu.VMEM((8, 3 * d_r), F32)]
    else:
        tm = n
        kern = functools.partial(_rwkv_prep_sample_kernel, seq=seq)
        args = [h, h_last_rows] + plist
        in_specs = [pl.BlockSpec((tm, d), lambda i: (i, 0))] * 2 + [_full_spec(a, 1) for a in plist]
        scratch = []
    return pl.pallas_call(
        kern,
        grid=(n // tm,),
        in_specs=in_specs,
        out_specs=[pl.BlockSpec((tm, d_r), lambda i: (i, 0))] * 7,
        out_shape=[jax.ShapeDtypeStruct((n, d_r), F32)] * 7,
        scratch_shapes=scratch,
        compiler_params=pltpu.CompilerParams(
            dimension_semantics=("arbitrary",), vmem_limit_bytes=VMEM_LIMIT),
        name="rwkv_prep",
    )(*args)


def _sb_proj_kernel(h_ref, w_ref, q_out, k_out, v_out, kb_out, vb_out):
    d_s = k_out.shape[-1]
    p = jnp.dot(h_ref[...].astype(BF16), w_ref[...], preferred_element_type=F32)
    q_out[...] = (p[:, :d_s] * SB_SCALE).astype(BF16)
    k = p[:, d_s:2 * d_s]
    v = p[:, 2 * d_s:]
    k_out[...] = k
    v_out[...] = v
    kb_out[...] = k.astype(BF16)
    vb_out[...] = v.astype(BF16)


def sb_proj(h, w_sb_bf):
    n, d = h.shape
    d_s = w_sb_bf.shape[1] // 3
    tm = min(ROW_TILE, n)
    blk = lambda: pl.BlockSpec((tm, d_s), lambda i: (i, 0))
    return pl.pallas_call(
        _sb_proj_kernel,
        grid=(n // tm,),
        in_specs=[pl.BlockSpec((tm, d), lambda i: (i, 0)), _full_spec(w_sb_bf, 1)],
        out_specs=[blk() for _ in range(5)],
        out_shape=[jax.ShapeDtypeStruct((n, d_s), BF16), jax.ShapeDtypeStruct((n, d_s), F32),
                   jax.ShapeDtypeStruct((n, d_s), F32), jax.ShapeDtypeStruct((n, d_s), BF16),
                   jax.ShapeDtypeStruct((n, d_s), BF16)],
        compiler_params=pltpu.CompilerParams(
            dimension_semantics=("arbitrary",), vmem_limit_bytes=VMEM_LIMIT),
        name="sb_proj",
    )(h, w_sb_bf)


def _stack_heads(x, m0):
    return jnp.concatenate([jnp.where(m0, x, 0.0), jnp.where(m0, 0.0, x)], axis=0)


def _rwkv_scan_kernel(r_ref, lw_ref, k_ref, a_ref, b_ref, v_ref, g_ref, s0_ref,
                      rk_ref, lg_ref, lb_ref, tri_ref, emean_ref, eones_ref,
                      o_ref, sfin_ref, s_ref, *, n_chunks, n_pairs):
    i = pl.program_id(1)
    c2 = 2 * CHUNK

    @pl.when(i == 0)
    def _():
        s_ref[...] = s0_ref[...]

    lane = lax.broadcasted_iota(jnp.int32, (1, PAIR), 1)
    m0 = lane < HEAD_DIM
    rr = lax.broadcasted_iota(jnp.int32, (c2, c2), 0)
    cc = lax.broadcasted_iota(jnp.int32, (c2, c2), 1)
    strict = cc < rr
    incl = cc <= rr
    eye = (cc == rr).astype(F32)
    tri = tri_ref[...]
    mm = lambda x, y: jnp.dot(x, y, preferred_element_type=F32)
    mm_nt = lambda x, y: lax.dot_general(x, y, (((1,), (1,)), ((), ())), preferred_element_type=F32)

    emean = emean_ref[...]
    eones = eones_ref[...]
    r_k = rk_ref[...]
    lnx_g = lg_ref[...]
    lnx_b = lb_ref[...]

    pairs = range(n_pairs)
    cat0 = lambda x, y: jnp.concatenate([x, y], axis=0)

    def prep(ins, p):
        lanes = slice(p * PAIR, (p + 1) * PAIR)
        r, lw, k, a, b, v, g = (x[:, lanes] for x in ins)
        hi, lo = _split(lw)
        lo2 = (lw - hi.astype(F32) - lo.astype(F32)).astype(BF16)
        cl = mm(tri, hi) + mm(tri, lo) + mm(tri, lo2)
        w_in = jnp.exp(cl)
        w_inv = jnp.exp(-cl)
        w_ex = jnp.exp(cl - lw)
        cl_end = cl[CHUNK - 1:CHUNK, :]
        w_tail = jnp.exp(cl_end - cl)
        stack = lambda x: _stack_heads(x, m0).astype(BF16)
        vst_f = _stack_heads(v, m0)
        return dict(
            r=r, k=k, v=v, g=g, lanes=lanes, w_end=jnp.exp(cl_end), vst_f=vst_f, vst=vst_f.astype(BF16),
            ast=stack(a * w_ex), rst=stack(r * w_in), bst=stack(b * w_inv), kst=stack(k * w_inv),
            btl=stack(b * w_tail), ktl=stack(k * w_tail))

    def chunk(c, carry):
        rows = pl.ds(pl.multiple_of(c * CHUNK, CHUNK), CHUNK)
        ins = [ref[rows, :] for ref in (r_ref, lw_ref, k_ref, a_ref, b_ref, v_ref, g_ref)]
        s = [s_ref[p] for p in pairs]
        sb = [x.astype(BF16) for x in s]
        d = [prep(ins, p) for p in pairs]
        m4 = [mm_nt(cat0(x["ast"], x["rst"]), cat0(x["bst"], x["kst"])) for x in d]
        nab = [jnp.where(strict, m[:c2, :c2], 0.0) for m in m4]
        mak = [jnp.where(strict, m[:c2, c2:], 0.0).astype(BF16) for m in m4]
        mrbk = [jnp.concatenate([jnp.where(incl, m[c2:, :c2], 0.0), jnp.where(incl, m[c2:, c2:], 0.0)],
                                axis=1).astype(BF16) for m in m4]
        t = [eye + n for n in nab]
        pw = [n.astype(BF16) for n in nab]
        for _ in range(5):
            pw = [mm(x, x).astype(BF16) for x in pw]
            t = [tt + mm(x, tt.astype(BF16)) for x, tt in zip(pw, t)]
        mv = [mm(mak[p], d[p]["vst"]) for p in pairs]
        rhs = [mm_nt(d[p]["ast"], sb[p]) + mv[p] for p in pairs]
        u = [mm(t[p].astype(BF16), rhs[p].astype(BF16)) for p in pairs]
        uv_t = [cat0(u[p], d[p]["vst_f"]).T.astype(BF16) for p in pairs]
        s_new = [s[p] * d[p]["w_end"] + mm(uv_t[p], cat0(d[p]["btl"], d[p]["ktl"])) for p in pairs]
        y = [mm_nt(d[p]["rst"], sb[p]) + mm(mrbk[p], cat0(u[p].astype(BF16), d[p]["vst"])) for p in pairs]
        yp = [x[:CHUNK, :] + x[CHUNK:, :] for x in y]
        mu = [_dot_hl(x, emean) for x in yp]
        yc = [x - m for x, m in zip(yp, mu)]
        var = [_dot_hl(x * x, emean) for x in yc]
        bonus = [_dot_hl(x["r"] * x["k"] * r_k[:, x["lanes"]], eones) * x["v"] for x in d]
        outs = [(yc[p] * lax.rsqrt(var[p] + LNX_EPS) * lnx_g[:, d[p]["lanes"]] + lnx_b[:, d[p]["lanes"]]
                 + bonus[p]) * d[p]["g"] for p in pairs]
        o_ref[rows, :] = jnp.concatenate(outs, axis=1)
        for p in pairs:
            s_ref[p] = s_new[p]
        return carry

    lax.fori_loop(0, n_chunks, chunk, 0)
    sfin_ref[...] = s_ref[...]


def rwkv_scan(r, lw, k, a, b, v, g, s0, r_k, lnx_g, lnx_b, batch, seq):
    n, d_r = r.shape
    n_pairs = d_r // PAIR
    rows = min(SCAN_ROWS, seq)
    steps = seq // rows
    tri = jnp.tril(jnp.ones((CHUNK, CHUNK), F32)).astype(BF16)
    blk = jnp.arange(PAIR) // HEAD_DIM
    same = (blk[:, None] == blk[None, :]).astype(F32)
    emean = (same / HEAD_DIM).astype(BF16)
    eones = same.astype(BF16)
    tok = pl.BlockSpec((rows, d_r), lambda bi, i: (bi * steps + i, 0))
    par = pl.BlockSpec((1, d_r), lambda bi, i: (0, 0))
    st = pl.BlockSpec((None, n_pairs, PAIR, PAIR), lambda bi, i: (bi, 0, 0, 0))
    return pl.pallas_call(
        functools.partial(_rwkv_scan_kernel, n_chunks=rows // CHUNK, n_pairs=n_pairs),
        grid=(batch, steps),
        in_specs=[tok] * 7 + [st, par, par, par, _full_spec(tri, 2), _full_spec(emean, 2), _full_spec(eones, 2)],
        out_specs=[tok, st],
        out_shape=[jax.ShapeDtypeStruct((n, d_r), F32),
                   jax.ShapeDtypeStruct((batch, n_pairs, PAIR, PAIR), F32)],
        scratch_shapes=[pltpu.VMEM((n_pairs, PAIR, PAIR), F32)],
        compiler_params=pltpu.CompilerParams(
            dimension_semantics=("arbitrary", "arbitrary"), vmem_limit_bytes=VMEM_LIMIT),
        name="rwkv_scan",
    )(r, lw, k, a, b, v, g, s0, r_k.reshape(1, d_r), lnx_g.reshape(1, d_r), lnx_b.reshape(1, d_r),
      tri, emean, eones)


def _sb_prompt_kernel(bias_ref, q_ref, k_ref, v_ref, ux_ref, o_ref, acc0, acc1, c0, c1, *, tq, tk):
    p = pl.program_id(1)
    i = pl.program_id(2)
    n_sub = tq // tk
    lane = lax.broadcasted_iota(jnp.int32, (1, PAIR), 1)
    m0 = lane < HEAD_DIM
    q = q_ref[...]
    zero = jnp.zeros_like(q)
    q_heads = (jnp.where(m0, q, zero), jnp.where(m0, zero, q))
    biases = (bias_ref[2 * p], bias_ref[2 * p + 1])
    accs = (acc0, acc1)
    cs = (c0, c1)
    for ref in accs + cs:
        ref[...] = jnp.zeros_like(ref)
    ux = ux_ref[...]
    rr = lax.broadcasted_iota(jnp.int32, (tq, tk), 0)
    cc = lax.broadcasted_iota(jnp.int32, (tq, tk), 1)

    def block(kb, causal):
        rows = pl.ds(pl.multiple_of(kb * tk, tk), tk)
        ks = k_ref[rows, :]
        vs = v_ref[rows, :]
        zs = [lax.dot_general(qh, ks, (((1,), (1,)), ((), ())), preferred_element_type=F32) + bh
              for qh, bh in zip(q_heads, biases)]
        ls = [jnp.minimum(-z, 0.0) - jnp.log(1.0 + jnp.exp(-jnp.abs(z))) for z in zs]
        lm = ls if causal is None else [jnp.where(causal, l, 0.0) for l in ls]
        cxs = [jnp.dot(l.astype(BF16), ux, preferred_element_type=F32) for l in lm]
        cprev = [c[...] for c in cs]
        ws = [jnp.exp((z + l) + cx[:, :tk] + jnp.concatenate([cp] * (tk // PAIR), axis=1))
              for z, l, cx, cp in zip(zs, ls, cxs, cprev)]
        if causal is not None:
            ws = [jnp.where(causal, w, 0.0) for w in ws]
        for acc, c, w, cx, cp in zip(accs, cs, ws, cxs, cprev):
            acc[...] += jnp.dot(w.astype(BF16), vs, preferred_element_type=F32)
            c[...] = cp + cx[:, tk:]

    for d in range(n_sub - 1, -1, -1):
        block(i * n_sub + d, cc + d * tk < rr)

    def body(it, carry):
        block(i * n_sub - 1 - it, None)
        return carry

    lax.fori_loop(0, i * n_sub, body, 0)
    o_ref[...] = jnp.where(m0, acc0[...], acc1[...])


def sb_prompt(q_bf, k_bf, v_bf, bias, batch, seq):
    n, d_s = q_bf.shape
    n_pairs = d_s // PAIR
    tq = min(SB_TILE, seq)
    tk = min(SB_KEYS, seq)
    nq = seq // tq
    j = jnp.arange(tk)
    ux = jnp.concatenate([(j[:, None] > j[None, :]).astype(F32), jnp.ones((tk, PAIR), F32)], axis=1).astype(BF16)
    grid_spec = pltpu.PrefetchScalarGridSpec(
        num_scalar_prefetch=1,
        grid=(batch, n_pairs, nq),
        in_specs=[
            pl.BlockSpec((tq, PAIR), lambda b, p, i, bias_ref: (b * nq + i, p)),
            pl.BlockSpec((seq, PAIR), lambda b, p, i, bias_ref: (b, p)),
            pl.BlockSpec((seq, PAIR), lambda b, p, i, bias_ref: (b, p)),
            pl.BlockSpec(ux.shape, lambda b, p, i, bias_ref: (0, 0)),
        ],
        out_specs=pl.BlockSpec((tq, PAIR), lambda b, p, i, bias_ref: (b * nq + i, p)),
        scratch_shapes=[pltpu.VMEM((tq, PAIR), F32)] * 4,
    )
    return pl.pallas_call(
        functools.partial(_sb_prompt_kernel, tq=tq, tk=tk),
        grid_spec=grid_spec,
        out_shape=jax.ShapeDtypeStruct((n, d_s), F32),
        compiler_params=pltpu.CompilerParams(
            dimension_semantics=("arbitrary", "arbitrary", "arbitrary"), vmem_limit_bytes=VMEM_LIMIT),
        name="sb_prompt",
    )(bias, q_bf, k_bf, v_bf, ux)


def _sb_sample_kernel(pt_ref, q_ref, brow_ref, kn_ref, vn_ref, *refs, n_pg, n_heads):
    kps, vps = refs[:n_pg], refs[n_pg:2 * n_pg]
    ux_ref, o_ref, acc_ref, c_ref = refs[2 * n_pg:]
    j = pl.program_id(1)
    qbd = q_ref[...]
    nrow = qbd.shape[0]
    brow = brow_ref[...]
    ux = ux_ref[...]

    def process(z, valid, av):
        nb = z.shape[1] // PAIR
        z = z + jnp.concatenate([brow] * nb, axis=1)
        sp = _softplus(z)
        l = -sp
        if valid is not None:
            l = jnp.where(valid, l, 0.0)
        lst = jnp.concatenate([l[:, bi * PAIR:(bi + 1) * PAIR] for bi in range(nb)], axis=0)
        cx = _dot_hl(lst, ux)
        run = c_ref[...]
        parts = [None] * nb
        for bi in range(nb - 1, -1, -1):
            blk = cx[bi * nrow:(bi + 1) * nrow, :]
            parts[bi] = blk[:, :PAIR] + run
            run = run + blk[:, PAIR:]
        w = jnp.exp((z - sp) + jnp.concatenate(parts, axis=1))
        if valid is not None:
            w = jnp.where(valid, w, 0.0)
        acc_ref[...] += av(w.astype(BF16))
        c_ref[...] = run

    @pl.when(j == 0)
    def _():
        acc_ref[...] = jnp.zeros_like(acc_ref)
        c_ref[...] = jnp.zeros_like(c_ref)
        t_of_row = lax.broadcasted_iota(jnp.int32, (nrow, PAIR), 0) // n_heads
        key = lax.broadcasted_iota(jnp.int32, (nrow, PAIR), 1)
        process(_dot_nt(qbd, kn_ref[...]), key < t_of_row, lambda w: _dot(w, vn_ref[...]))

    d_s = qbd.shape[1]
    page_t = lambda r: r[...].reshape(d_s, PAIR).astype(BF16)
    z_pages = jnp.concatenate([_dot(qbd, page_t(r)) for r in kps], axis=1)

    def av_pages(w):
        out = _dot_nt(w[:, :PAIR], page_t(vps[0]))
        for pi in range(1, n_pg):
            out = out + _dot_nt(w[:, pi * PAIR:(pi + 1) * PAIR], page_t(vps[pi]))
        return out

    process(z_pages, None, av_pages)

    @pl.when(j == pl.num_programs(1) - 1)
    def _():
        d_s = acc_ref.shape[1]
        head_of_row = lax.broadcasted_iota(jnp.int32, (nrow, d_s), 0) % n_heads
        head_of_lane = lax.broadcasted_iota(jnp.int32, (nrow, d_s), 1) // HEAD_DIM
        sel = jnp.where(head_of_row == head_of_lane, acc_ref[...], 0.0)
        o_ref[...] = jnp.sum(sel.reshape(nrow // n_heads, n_heads, d_s), axis=1)


def sb_sample(q_bd, brow, k_new, v_new, cache_k, cache_v, page_table, n_heads):
    bsz, nrow, d_s = q_bd.shape
    n_pages = page_table.shape[1]
    page = cache_k.shape[3]
    assert page == PAIR
    n_pg = min(SB_PAGES, n_pages)
    steps = n_pages // n_pg
    jj = jnp.arange(PAIR)
    ux = jnp.concatenate([(jj[:, None] > jj[None, :]).astype(F32), jnp.ones((PAIR, PAIR), F32)], axis=1).astype(BF16)

    def page_spec(i):
        return pl.BlockSpec((None, n_heads, HEAD_DIM, page),
                            lambda b, j, pt: (pt[b, n_pages - n_pg * (j + 1) + i], 0, 0, 0))

    per_b = lambda shape: pl.BlockSpec((None,) + shape, lambda b, j, pt: (b, 0, 0))
    grid_spec = pltpu.PrefetchScalarGridSpec(
        num_scalar_prefetch=1,
        grid=(bsz, steps),
        in_specs=[per_b((nrow, d_s)), pl.BlockSpec(brow.shape, lambda b, j, pt: (0, 0)),
                  per_b((PAIR, d_s)), per_b((PAIR, d_s))]
                 + [page_spec(i) for i in range(n_pg)] * 2
                 + [pl.BlockSpec(ux.shape, lambda b, j, pt: (0, 0))],
        out_specs=per_b((nrow // n_heads, d_s)),
        scratch_shapes=[pltpu.VMEM((nrow, d_s), F32), pltpu.VMEM((nrow, PAIR), F32)],
    )
    return pl.pallas_call(
        functools.partial(_sb_sample_kernel, n_pg=n_pg, n_heads=n_heads),
        grid_spec=grid_spec,
        out_shape=jax.ShapeDtypeStruct((bsz, nrow // n_heads, d_s), F32),
        compiler_params=pltpu.CompilerParams(
            dimension_semantics=("arbitrary", "arbitrary"), vmem_limit_bytes=VMEM_LIMIT),
        name="sb_sample",
    )(page_table, q_bd, brow, k_new, v_new, *([cache_k] * n_pg), *([cache_v] * n_pg), ux)


def _merge_kernel(h_ref, oa_ref, ob_ref, wg_ref, wba_ref, wbb_ref, wo_ref, g_ref, b_ref, o_ref, *, alpha):
    h = h_ref[...]
    d = h.shape[1]
    gates = jax.nn.sigmoid(jnp.dot(h.astype(BF16), wg_ref[...], preferred_element_type=F32))
    ma = jnp.dot(oa_ref[...].astype(BF16), wba_ref[...], preferred_element_type=F32)
    mb = jnp.dot(ob_ref[...].astype(BF16), wbb_ref[...], preferred_element_type=F32)
    merged = gates[:, :d] * ma + gates[:, d:] * mb
    y = alpha * h + jnp.dot(merged.astype(BF16), wo_ref[...], preferred_element_type=F32)
    o_ref[...] = _layer_norm(y, g_ref[...], b_ref[...])


def merge(h, o_a, o_b, w_gate_bf, w_ba_bf, w_bb_bf, w_out_bf, g, b, alpha):
    n, d = h.shape
    tm = min(ROW_TILE, n)
    row = lambda a: pl.BlockSpec((tm, a.shape[1]), lambda i: (i, 0))
    ws = [w_gate_bf, w_ba_bf, w_bb_bf, w_out_bf, g.reshape(1, d), b.reshape(1, d)]
    return pl.pallas_call(
        functools.partial(_merge_kernel, alpha=alpha),
        grid=(n // tm,),
        in_specs=[row(h), row(o_a), row(o_b)] + [_full_spec(a, 1) for a in ws],
        out_specs=pl.BlockSpec((tm, d), lambda i: (i, 0)),
        out_shape=jax.ShapeDtypeStruct((n, d), F32),
        compiler_params=pltpu.CompilerParams(
            dimension_semantics=("arbitrary",), vmem_limit_bytes=VMEM_LIMIT),
        name="merge",
    )(h, o_a, o_b, *ws)


def _pair_blockdiag(s):
    bsz, n_h = s.shape[:2]
    s = s.reshape(bsz, n_h // 2, 2, HEAD_DIM, HEAD_DIM)
    eye = jnp.eye(2, dtype=s.dtype)
    out = s[:, :, :, :, None, :] * eye[None, None, :, None, :, None]
    return out.reshape(bsz, n_h // 2, PAIR, PAIR)


def _pair_unblock(sbd):
    bsz, n_p = sbd.shape[:2]
    s = sbd.reshape(bsz, n_p, 2, HEAD_DIM, 2, HEAD_DIM)
    return jnp.stack([s[:, :, 0, :, 0, :], s[:, :, 1, :, 1, :]], axis=2).reshape(
        bsz, 2 * n_p, HEAD_DIM, HEAD_DIM)


def _layer(x, batch, seq, h_last, wkv0, sb_fn, w, alpha):
    n, d = x.shape
    h = ffn_ln(x, w["ffn1_in"], w["ffn1_out"], w["ln1_g"], w["ln1_b"], alpha)
    if h_last is None:
        hl_rows = None
        rows_pad = seq
    else:
        hl_rows = jnp.repeat(h_last, seq, axis=0)
        rows_pad = CHUNK
    r, lw, k, a, b, v, g = rwkv_prep(h, hl_rows, seq, w)
    n_h = w["w0"].shape[-1] // HEAD_DIM
    if wkv0 is None:
        s0 = jnp.zeros((batch, n_h // 2, PAIR, PAIR), F32)
    else:
        s0 = _pair_blockdiag(wkv0.astype(F32))
    scan_in = [r, lw, k, a, b, v, g]
    if rows_pad != seq:
        pad = lambda t: jnp.pad(t.reshape(batch, seq, -1), ((0, 0), (0, rows_pad - seq), (0, 0))).reshape(
            batch * rows_pad, -1)
        scan_in = [pad(t) for t in scan_in]
    o_a, s_fin = rwkv_scan(*scan_in, s0, w["r_k"], w["lnx_g"], w["lnx_b"], batch, rows_pad)
    if rows_pad != seq:
        o_a = o_a.reshape(batch, rows_pad, -1)[:, :seq].reshape(n, -1)
    wkv = _pair_unblock(s_fin)
    q_bf, k_sb, v_sb, k_bf, v_bf = sb_proj(h, w["w_sb"])
    o_b = sb_fn(q_bf, k_sb, v_sb, k_bf, v_bf)
    x2 = merge(h, o_a, o_b, w["w_gate"], w["w_ba"], w["w_bb"], w["w_out"], w["ln2_g"], w["ln2_b"], alpha)
    x3 = ffn_ln(x2, w["ffn2_in"], w["ffn2_out"], w["ln3_g"], w["ln3_b"], alpha)
    return x3, k_sb, v_sb, wkv, h.reshape(batch, seq, d)[:, -1]


def kernel(x_prompt, x_sample, cache_k, cache_v, state_wkv, state_shift, page_table, ln1_g, ln1_b, ffn1_w_in, ffn1_w_out, w_in, mu_rkv, mu_wag, w0, w_w1, w_w2, a0, a_w1, a_w2, g_w1, g_w2, k_k, k_a, r_k, lnx_g, lnx_b, sb_bias, w_branch, w_out, ln2_g, ln2_b, ffn2_w_in, ffn2_w_out, ln3_g, ln3_b):
    depth = ln1_g.shape[0]
    alpha = (2.0 * depth) ** 0.25
    bp, tp, d = x_prompt.shape
    bs, ts, _ = x_sample.shape
    d_r = w0.shape[-1]
    d_s = (w_in.shape[-1] - 3 * d_r - 2 * d) // 3
    n_hs = d_s // HEAD_DIM
    blk = jnp.arange(d_r) // HEAD_DIM
    e_head = (blk[:, None] == blk[None, :]).astype(BF16)

    xp = x_prompt.reshape(bp * tp, d)
    xs = x_sample.reshape(bs * ts, d)
    outs = [[] for _ in range(8)]
    for l in range(depth):
        row = lambda a: a[l].reshape(1, -1)
        w = dict(
            ffn1_in=ffn1_w_in[l].astype(BF16), ffn1_out=ffn1_w_out[l].astype(BF16),
            ffn2_in=ffn2_w_in[l].astype(BF16), ffn2_out=ffn2_w_out[l].astype(BF16),
            ln1_g=ln1_g[l], ln1_b=ln1_b[l], ln2_g=ln2_g[l], ln2_b=ln2_b[l], ln3_g=ln3_g[l], ln3_b=ln3_b[l],
            w_rkv=w_in[l][:, :3 * d_r].astype(BF16),
            w_sb=w_in[l][:, 3 * d_r:3 * d_r + 3 * d_s].astype(BF16),
            w_gate=w_in[l][:, 3 * d_r + 3 * d_s:].astype(BF16),
            w_w1=w_w1[l].astype(BF16), w_w2=w_w2[l].astype(BF16), a_w1=a_w1[l].astype(BF16),
            a_w2=a_w2[l].astype(BF16), g_w1=g_w1[l].astype(BF16), g_w2=g_w2[l].astype(BF16),
            mu_rkv=row(mu_rkv), mu_wag=mu_wag[l], w0=row(w0), a0=row(a0), k_k=row(k_k), k_a=row(k_a),
            e_head=e_head, r_k=r_k[l].reshape(-1), lnx_g=lnx_g[l], lnx_b=lnx_b[l],
            w_ba=w_branch[l][:d_r].astype(BF16), w_bb=w_branch[l][d_r:].astype(BF16),
            w_out=w_out[l].astype(BF16),
        )
        bias = sb_bias[l].astype(F32)

        def prompt_sb(q_bf, k_sb, v_sb, k_bf, v_bf):
            return sb_prompt(q_bf, k_bf, v_bf, bias, bp, tp)

        def sample_sb(q_bf, k_sb, v_sb, k_bf, v_bf):
            head_of_lane = jnp.arange(d_s) // HEAD_DIM
            onehot = (jnp.arange(n_hs)[:, None] == head_of_lane[None, :])
            q_bd = jnp.where(onehot[None, None], q_bf.reshape(bs, ts, 1, d_s), jnp.zeros((), BF16))
            q_bd = q_bd.reshape(bs, ts * n_hs, d_s)
            brow = jnp.broadcast_to(jnp.tile(bias, ts)[:, None], (ts * n_hs, PAIR))
            padk = lambda t: jnp.pad(t.reshape(bs, ts, d_s), ((0, 0), (0, PAIR - ts), (0, 0)))
            ck = jnp.transpose(cache_k[l], (0, 2, 3, 1))
            cv = jnp.transpose(cache_v[l], (0, 2, 3, 1))
            o = sb_sample(q_bd, brow, padk(k_sb), padk(v_sb), ck, cv, page_table, n_hs)
            return o.reshape(bs * ts, d_s)

        xp, kp, vp, wp, hp = _layer(xp, bp, tp, None, None, prompt_sb, w, alpha)
        xs, ks_, vs_, ws_, hs_ = _layer(xs, bs, ts, state_shift[l].astype(F32), state_wkv[l], sample_sb, w, alpha)
        for lst, val in zip(outs, (kp.reshape(bp, tp, n_hs, HEAD_DIM), vp.reshape(bp, tp, n_hs, HEAD_DIM), wp, hp,
                                   ks_.reshape(bs, ts, n_hs, HEAD_DIM), vs_.reshape(bs, ts, n_hs, HEAD_DIM),
                                   ws_.astype(state_wkv.dtype), hs_)):
            lst.append(val)
    return (xp.reshape(bp, tp, d), xs.reshape(bs, ts, d)) + tuple(jnp.stack(o) for o in outs)
```

```python
import functools

import jax
import jax.numpy as jnp
from jax import lax
from jax.experimental import pallas as pl
from jax.experimental.pallas import tpu as pltpu

F32 = jnp.float32
BF16 = jnp.bfloat16

HEAD_DIM = 64
PAIR = 2 * HEAD_DIM
CHUNK = 64
LN_EPS = 1e-5
LNX_EPS = 64e-5
SB_SCALE = HEAD_DIM ** -0.5

ROW_TILE = 512
SCAN_ROWS = 512
SB_TILE = 512
SB_KEYS = 256
SB_PAIRS = 2
SB_PAGES = 16
VMEM_LIMIT = 56 * 1024 * 1024


def _dot(a, b):
    return jnp.dot(a.astype(BF16), b.astype(BF16), preferred_element_type=F32)


def _dot_nt(a, b):
    return lax.dot_general(a.astype(BF16), b.astype(BF16), (((1,), (1,)), ((), ())),
                           preferred_element_type=F32)


def _split(x):
    hi = x.astype(BF16)
    lo = (x - hi.astype(F32)).astype(BF16)
    return hi, lo


def _dot_hl(a, b_exact):
    hi, lo = _split(a)
    return (jnp.dot(hi, b_exact, preferred_element_type=F32)
            + jnp.dot(lo, b_exact, preferred_element_type=F32))


def _dot3(a, b, nt=False):
    ah, al = _split(a)
    bh, bl = _split(b)
    if nt:
        d = lambda x, y: lax.dot_general(x, y, (((1,), (1,)), ((), ())), preferred_element_type=F32)
    else:
        d = lambda x, y: jnp.dot(x, y, preferred_element_type=F32)
    return d(ah, bh) + d(al, bh) + d(ah, bl)


LOG2E = 1.4426950408889634


def _neg_abs(x):
    bits = pltpu.bitcast(x, jnp.uint32) | jnp.uint32(0x80000000)
    return pltpu.bitcast(bits, F32)


def _softplus(u):
    return jnp.maximum(u, 0.0) + jnp.log1p(jnp.exp(-jnp.abs(u)))


def _layer_norm(y, g, b):
    mu = jnp.mean(y, axis=-1, keepdims=True)
    yc = y - mu
    var = jnp.mean(yc * yc, axis=-1, keepdims=True)
    return yc * lax.rsqrt(var + LN_EPS) * g + b


def _full_spec(a, grid_rank):
    nd = a.ndim
    if grid_rank == 1:
        return pl.BlockSpec(a.shape, lambda i: (0,) * nd)
    if grid_rank == 2:
        return pl.BlockSpec(a.shape, lambda i, j: (0,) * nd)
    return pl.BlockSpec(a.shape, lambda i, j, k: (0,) * nd)


def _ffn_ln_kernel(x_ref, wg_ref, wu_ref, wo_ref, g_ref, b_ref, o_ref, acc_ref, *, alpha, n_ff):
    j = pl.program_id(1)

    @pl.when(j == 0)
    def _():
        acc_ref[...] = jnp.zeros_like(acc_ref)

    xb = x_ref[...].astype(BF16)
    gate = jnp.dot(xb, wg_ref[...], preferred_element_type=F32)
    up = jnp.dot(xb, wu_ref[...], preferred_element_type=F32)
    mid = gate * jax.nn.sigmoid(gate) * up
    acc_ref[...] += jnp.dot(mid.astype(BF16), wo_ref[...], preferred_element_type=F32)

    @pl.when(j == n_ff - 1)
    def _():
        y = alpha * x_ref[...] + 0.5 * acc_ref[...]
        o_ref[...] = _layer_norm(y, g_ref[...], b_ref[...])


def _ffn_tile(d_ff):
    for n in (2, 1, 11, 22):
        if d_ff % n == 0 and (d_ff // n) % 128 == 0:
            return d_ff // n
    return d_ff


def ffn_ln(x, w_in_bf, w_out_bf, g, b, alpha):
    n, d = x.shape
    d_ff = w_out_bf.shape[0]
    tm = min(ROW_TILE, n)
    tf = _ffn_tile(d_ff)
    n_ff = d_ff // tf
    return pl.pallas_call(
        functools.partial(_ffn_ln_kernel, alpha=alpha, n_ff=n_ff),
        grid=(n // tm, n_ff),
        in_specs=[
            pl.BlockSpec((tm, d), lambda i, j: (i, 0)),
            pl.BlockSpec((d, tf), lambda i, j: (0, j)),
            pl.BlockSpec((d, tf), lambda i, j: (0, j + n_ff)),
            pl.BlockSpec((tf, d), lambda i, j: (j, 0)),
            pl.BlockSpec((1, d), lambda i, j: (0, 0)),
            pl.BlockSpec((1, d), lambda i, j: (0, 0)),
        ],
        out_specs=pl.BlockSpec((tm, d), lambda i, j: (i, 0)),
        out_shape=jax.ShapeDtypeStruct((n, d), F32),
        scratch_shapes=[pltpu.VMEM((tm, d), F32)],
        compiler_params=pltpu.CompilerParams(
            dimension_semantics=("arbitrary", "arbitrary"), vmem_limit_bytes=VMEM_LIMIT),
        name="ffn_ln",
    )(x, w_in_bf, w_in_bf, w_out_bf, g.reshape(1, d), b.reshape(1, d))


N_PREP_PARAMS = 14


def _rwkv_prep_math(h, h_prev, p, p_prev, params, outs):
    (wrkv_ref, ww1_ref, ww2_ref, aw1_ref, aw2_ref, gw1_ref, gw2_ref,
     mu_rkv_ref, mu_wag_ref, w0_ref, a0_ref, kk_ref, ka_ref, e_ref) = params
    r_out, lw_out, k_out, a_out, b_out, v_out, g_out = outs
    d_r = w0_ref.shape[-1]
    rkv = p + (p_prev - p) * mu_rkv_ref[...]
    r = rkv[:, :d_r]
    k = rkv[:, d_r:2 * d_r]
    v = rkv[:, 2 * d_r:]
    dx = h_prev - h
    xw = h + dx * mu_wag_ref[0:1, :]
    xa = h + dx * mu_wag_ref[1:2, :]
    xg = h + dx * mu_wag_ref[2:3, :]
    lw = _dot(jnp.tanh(_dot(xw, ww1_ref[...])), ww2_ref[...])
    w_log = -_softplus(-(w0_ref[...] + lw)) - 0.5
    a_gate = jax.nn.sigmoid(a0_ref[...] + _dot(_dot(xa, aw1_ref[...]), aw2_ref[...]))
    g = _dot(jax.nn.sigmoid(_dot(xg, gw1_ref[...])), gw2_ref[...])
    kk = k * kk_ref[...]
    ss = _dot_hl(kk * kk, e_ref[...])
    kk = kk / jnp.maximum(jnp.sqrt(ss), 1e-12)
    r_out[...] = r
    lw_out[...] = -jnp.exp(w_log)
    k_out[...] = k * (1.0 + (a_gate - 1.0) * ka_ref[...])
    a_out[...] = -kk
    b_out[...] = kk * a_gate
    v_out[...] = v
    g_out[...] = g


def _rwkv_prep_prompt_kernel(h_ref, *refs, tiles_per_seq):
    params, outs = refs[:N_PREP_PARAMS], refs[N_PREP_PARAMS:N_PREP_PARAMS + 7]
    hcarry_ref, pcarry_ref = refs[-2:]
    i = pl.program_id(0)
    h = h_ref[...]
    tm = h.shape[0]

    @pl.when(i % tiles_per_seq == 0)
    def _():
        hcarry_ref[...] = jnp.zeros_like(hcarry_ref)
        pcarry_ref[...] = jnp.zeros_like(pcarry_ref)

    row = lax.broadcasted_iota(jnp.int32, (tm, 1), 0)
    p = jnp.dot(h.astype(BF16), params[0][...], preferred_element_type=F32)
    h_prev = jnp.where(row == 0, hcarry_ref[7:8, :], pltpu.roll(h, 1, 0))
    p_prev = jnp.where(row == 0, pcarry_ref[7:8, :], pltpu.roll(p, 1, 0))
    _rwkv_prep_math(h, h_prev, p, p_prev, params, outs)
    hcarry_ref[...] = h[tm - 8:, :]
    pcarry_ref[...] = p[tm - 8:, :]


def _rwkv_prep_sample_kernel(h_ref, hl_ref, *refs, seq):
    params, outs = refs[:N_PREP_PARAMS], refs[N_PREP_PARAMS:N_PREP_PARAMS + 7]
    h = h_ref[...]
    tm = h.shape[0]
    row = lax.broadcasted_iota(jnp.int32, (tm, 1), 0)
    first = row % seq == 0
    p = jnp.dot(h.astype(BF16), params[0][...], preferred_element_type=F32)
    p_last = jnp.dot(hl_ref[...].astype(BF16), params[0][...], preferred_element_type=F32)
    h_prev = jnp.where(first, hl_ref[...], pltpu.roll(h, 1, 0))
    p_prev = jnp.where(first, p_last, pltpu.roll(p, 1, 0))
    _rwkv_prep_math(h, h_prev, p, p_prev, params, outs)


def rwkv_prep(h, h_last_rows, seq, prm):
    n, d = h.shape
    d_r = prm["w0"].shape[-1]
    plist = [prm["w_rkv"], prm["w_w1"], prm["w_w2"], prm["a_w1"], prm["a_w2"], prm["g_w1"], prm["g_w2"],
             prm["mu_rkv"], prm["mu_wag"], prm["w0"], prm["a0"], prm["k_k"], prm["k_a"], prm["e_head"]]
    assert len(plist) == N_PREP_PARAMS
    if h_last_rows is None:
        tm = min(ROW_TILE, seq)
        kern = functools.partial(_rwkv_prep_prompt_kernel, tiles_per_seq=seq // tm)
        args = [h] + plist
        in_specs = [pl.BlockSpec((tm, d), lambda i: (i, 0))] + [_full_spec(a, 1) for a in plist]
        scratch = [pltpu.VMEM((8, d), F32), pltpu.VMEM((8, 3 * d_r), F32)]
    else:
        tm = n
        kern = functools.partial(_rwkv_prep_sample_kernel, seq=seq)
        args = [h, h_last_rows] + plist
        in_specs = [pl.BlockSpec((tm, d), lambda i: (i, 0))] * 2 + [_full_spec(a, 1) for a in plist]
        scratch = []
    return pl.pallas_call(
        kern,
        grid=(n // tm,),
        in_specs=in_specs,
        out_specs=[pl.BlockSpec((tm, d_r), lambda i: (i, 0))] * 7,
        out_shape=[jax.ShapeDtypeStruct((n, d_r), F32)] * 7,
        scratch_shapes=scratch,
        compiler_params=pltpu.CompilerParams(
            dimension_semantics=("arbitrary",), vmem_limit_bytes=VMEM_LIMIT),
        name="rwkv_prep",
    )(*args)


def _sb_proj_kernel(h_ref, w_ref, q_out, k_out, v_out, kb_out, vb_out, *, head_major):
    d_s = kb_out.shape[-1]
    p = jnp.dot(h_ref[...].astype(BF16), w_ref[...], preferred_element_type=F32)
    q_out[...] = (p[:, :d_s] * SB_SCALE).astype(BF16)
    k = p[:, d_s:2 * d_s]
    v = p[:, 2 * d_s:]
    if head_major:
        k_out[...] = k.T.reshape(k_out.shape)
        v_out[...] = v.T.reshape(v_out.shape)
    else:
        k_out[...] = k
        v_out[...] = v
    kb_out[...] = k.astype(BF16)
    vb_out[...] = v.astype(BF16)


def sb_proj(h, w_sb_bf, batch, seq, head_major):
    n, d = h.shape
    d_s = w_sb_bf.shape[1] // 3
    tm = min(ROW_TILE, seq if head_major else n)
    blk = lambda: pl.BlockSpec((tm, d_s), lambda i: (i, 0))
    if head_major:
        tps = seq // tm
        n_h = d_s // HEAD_DIM
        kv_spec = lambda: pl.BlockSpec((None, n_h, HEAD_DIM, tm), lambda i: (i // tps, 0, 0, i % tps))
        kv_shape = jax.ShapeDtypeStruct((batch, n_h, HEAD_DIM, seq), F32)
    else:
        kv_spec = blk
        kv_shape = jax.ShapeDtypeStruct((n, d_s), F32)
    return pl.pallas_call(
        functools.partial(_sb_proj_kernel, head_major=head_major),
        grid=(n // tm,),
        in_specs=[pl.BlockSpec((tm, d), lambda i: (i, 0)), _full_spec(w_sb_bf, 1)],
        out_specs=[blk(), kv_spec(), kv_spec(), blk(), blk()],
        out_shape=[jax.ShapeDtypeStruct((n, d_s), BF16), kv_shape, kv_shape,
                   jax.ShapeDtypeStruct((n, d_s), BF16), jax.ShapeDtypeStruct((n, d_s), BF16)],
        compiler_params=pltpu.CompilerParams(
            dimension_semantics=("arbitrary",), vmem_limit_bytes=VMEM_LIMIT),
        name="sb_proj",
    )(h, w_sb_bf)


def _stack_heads(x, m0):
    return jnp.concatenate([jnp.where(m0, x, 0.0), jnp.where(m0, 0.0, x)], axis=0)


def _rwkv_scan_kernel(r_ref, lw_ref, k_ref, a_ref, b_ref, v_ref, g_ref, s0_ref,
                      rk_ref, lg_ref, lb_ref, tri_ref, emean_ref, eones_ref,
                      o_ref, sfin_ref, s_ref, *, n_chunks, n_pairs):
    i = pl.program_id(1)
    c2 = 2 * CHUNK

    @pl.when(i == 0)
    def _():
        zero = jnp.zeros((HEAD_DIM, HEAD_DIM), F32)
        for p in range(n_pairs):
            top = jnp.concatenate([s0_ref[2 * p], zero], axis=1)
            bot = jnp.concatenate([zero, s0_ref[2 * p + 1]], axis=1)
            s_ref[p] = jnp.concatenate([top, bot], axis=0)

    lane = lax.broadcasted_iota(jnp.int32, (1, PAIR), 1)
    m0 = lane < HEAD_DIM
    rr = lax.broadcasted_iota(jnp.int32, (c2, c2), 0)
    cc = lax.broadcasted_iota(jnp.int32, (c2, c2), 1)
    strict = cc < rr
    incl = cc <= rr
    eye = (cc == rr).astype(F32)
    tri = tri_ref[...]
    mm = lambda x, y: jnp.dot(x, y, preferred_element_type=F32)
    mm_nt = lambda x, y: lax.dot_general(x, y, (((1,), (1,)), ((), ())), preferred_element_type=F32)

    emean = emean_ref[...]
    eones = eones_ref[...]
    r_k = rk_ref[...]
    lnx_g = lg_ref[...]
    lnx_b = lb_ref[...]

    pairs = range(n_pairs)
    cat0 = lambda x, y: jnp.concatenate([x, y], axis=0)

    def prep(ins, p):
        lanes = slice(p * PAIR, (p + 1) * PAIR)
        r, lw, k, a, b, v, g = (x[:, lanes] for x in ins)
        hi, lo = _split(lw)
        lo2 = (lw - hi.astype(F32) - lo.astype(F32)).astype(BF16)
        cl = mm(tri, hi) + mm(tri, lo) + mm(tri, lo2)
        w_in = jnp.exp(cl)
        w_inv = jnp.exp(-cl)
        w_ex = jnp.exp(cl - lw)
        cl_end = cl[CHUNK - 1:CHUNK, :]
        w_tail = jnp.exp(cl_end - cl)
        stack = lambda x: _stack_heads(x, m0).astype(BF16)
        vst_f = _stack_heads(v, m0)
        return dict(
            r=r, k=k, v=v, g=g, lanes=lanes, w_end=jnp.exp(cl_end), vst_f=vst_f, vst=vst_f.astype(BF16),
            ast=stack(a * w_ex), rst=stack(r * w_in), bst=stack(b * w_inv), kst=stack(k * w_inv),
            btl=stack(b * w_tail), ktl=stack(k * w_tail))

    def chunk(c, carry):
        rows = pl.ds(pl.multiple_of(c * CHUNK, CHUNK), CHUNK)
        ins = [ref[rows, :] for ref in (r_ref, lw_ref, k_ref, a_ref, b_ref, v_ref, g_ref)]
        s = [s_ref[p] for p in pairs]
        sb = [x.astype(BF16) for x in s]
        d = [prep(ins, p) for p in pairs]
        m4 = [mm_nt(cat0(x["ast"], x["rst"]), cat0(x["bst"], x["kst"])) for x in d]
        nab = [jnp.where(strict, m[:c2, :c2], 0.0) for m in m4]
        mak = [jnp.where(strict, m[:c2, c2:], 0.0).astype(BF16) for m in m4]
        mrbk = [jnp.concatenate([jnp.where(incl, m[c2:, :c2], 0.0), jnp.where(incl, m[c2:, c2:], 0.0)],
                                axis=1).astype(BF16) for m in m4]
        t = [eye + n for n in nab]
        pw = [n.astype(BF16) for n in nab]
        for _ in range(5):
            pw = [mm(x, x).astype(BF16) for x in pw]
            t = [tt + mm(x, tt.astype(BF16)) for x, tt in zip(pw, t)]
        mv = [mm(mak[p], d[p]["vst"]) for p in pairs]
        rhs = [mm_nt(d[p]["ast"], sb[p]) + mv[p] for p in pairs]
        u = [mm(t[p].astype(BF16), rhs[p].astype(BF16)) for p in pairs]
        uv_t = [cat0(u[p], d[p]["vst_f"]).T.astype(BF16) for p in pairs]
        s_new = [s[p] * d[p]["w_end"] + mm(uv_t[p], cat0(d[p]["btl"], d[p]["ktl"])) for p in pairs]
        y = [mm_nt(d[p]["rst"], sb[p]) + mm(mrbk[p], cat0(u[p].astype(BF16), d[p]["vst"])) for p in pairs]
        yp = [x[:CHUNK, :] + x[CHUNK:, :] for x in y]
        mu = [_dot_hl(x, emean) for x in yp]
        yc = [x - m for x, m in zip(yp, mu)]
        var = [_dot_hl(x * x, emean) for x in yc]
        bonus = [_dot_hl(x["r"] * x["k"] * r_k[:, x["lanes"]], eones) * x["v"] for x in d]
        outs = [(yc[p] * lax.rsqrt(var[p] + LNX_EPS) * lnx_g[:, d[p]["lanes"]] + lnx_b[:, d[p]["lanes"]]
                 + bonus[p]) * d[p]["g"] for p in pairs]
        o_ref[rows, :] = jnp.concatenate(outs, axis=1)
        for p in pairs:
            s_ref[p] = s_new[p]
        return carry

    lax.fori_loop(0, n_chunks, chunk, 0)
    for p in range(n_pairs):
        sfin_ref[2 * p] = s_ref[p, :HEAD_DIM, :HEAD_DIM]
        sfin_ref[2 * p + 1] = s_ref[p, HEAD_DIM:, HEAD_DIM:]


def rwkv_scan(r, lw, k, a, b, v, g, s0, r_k, lnx_g, lnx_b, batch, seq):
    n, d_r = r.shape
    n_pairs = d_r // PAIR
    rows = min(SCAN_ROWS, seq)
    steps = seq // rows
    tri = jnp.tril(jnp.ones((CHUNK, CHUNK), F32)).astype(BF16)
    blk = jnp.arange(PAIR) // HEAD_DIM
    same = (blk[:, None] == blk[None, :]).astype(F32)
    emean = (same / HEAD_DIM).astype(BF16)
    eones = same.astype(BF16)
    tok = pl.BlockSpec((rows, d_r), lambda bi, i: (bi * steps + i, 0))
    par = pl.BlockSpec((1, d_r), lambda bi, i: (0, 0))
    st = pl.BlockSpec((None, 2 * n_pairs, HEAD_DIM, HEAD_DIM), lambda bi, i: (bi, 0, 0, 0))
    return pl.pallas_call(
        functools.partial(_rwkv_scan_kernel, n_chunks=rows // CHUNK, n_pairs=n_pairs),
        grid=(batch, steps),
        in_specs=[tok] * 7 + [st, par, par, par, _full_spec(tri, 2), _full_spec(emean, 2), _full_spec(eones, 2)],
        out_specs=[tok, st],
        out_shape=[jax.ShapeDtypeStruct((n, d_r), F32),
                   jax.ShapeDtypeStruct((batch, 2 * n_pairs, HEAD_DIM, HEAD_DIM), F32)],
        scratch_shapes=[pltpu.VMEM((n_pairs, PAIR, PAIR), F32)],
        compiler_params=pltpu.CompilerParams(
            dimension_semantics=("arbitrary", "arbitrary"), vmem_limit_bytes=VMEM_LIMIT),
        name="rwkv_scan",
    )(r, lw, k, a, b, v, g, s0, r_k.reshape(1, d_r), lnx_g.reshape(1, d_r), lnx_b.reshape(1, d_r),
      tri, emean, eones)


def _sb_prompt_kernel(bias_ref, q_ref, k_ref, v_ref, u_ref, o_ref, acc_ref, c_ref, *, tq, tk, n_hp):
    g = pl.program_id(1)
    i = pl.program_id(2)
    n_sub = tq // tk
    n_heads = 2 * n_hp
    lane = lax.broadcasted_iota(jnp.int32, (1, PAIR), 1)
    m0 = lane < HEAD_DIM
    pair_lanes = [slice((h // 2) * PAIR, (h // 2 + 1) * PAIR) for h in range(n_heads)]
    q_heads = []
    for h in range(n_heads):
        qp = q_ref[:, pair_lanes[h]]
        qh = jnp.where(m0 if h % 2 == 0 else ~m0, qp, jnp.zeros_like(qp))
        bias = jnp.full((1, PAIR), bias_ref[n_heads * g + h], F32)
        b1 = bias.astype(BF16).astype(F32)
        b2 = (bias - b1).astype(BF16).astype(F32)
        b3 = (bias - b1) - b2
        ext = jnp.where(lane == 0, b1, jnp.where(lane == 1, b2, jnp.where(lane == 2, b3, 0.0)))
        q_heads.append(jnp.concatenate([qh, jnp.broadcast_to(ext, qh.shape).astype(BF16)], axis=1))
    k_ext = jnp.broadcast_to(jnp.where(lane < 3, 1.0, 0.0), (tq, PAIR)).astype(BF16)
    acc_ref[...] = jnp.zeros_like(acc_ref)
    c_ref[...] = jnp.zeros_like(c_ref)
    u = u_ref[...]
    rr = lax.broadcasted_iota(jnp.int32, (tq, tq), 0)
    cc = lax.broadcasted_iota(jnp.int32, (tq, tq), 1)
    subs = [slice(j * tk, (j + 1) * tk) for j in range(n_sub)]
    tile = lambda x: jnp.concatenate([x] * (tk // PAIR), axis=1)

    def block(kb, causal):
        rows = pl.ds(pl.multiple_of(kb * tq, tq), tq)
        ks = [jnp.concatenate([k_ref[rows, pair_lanes[h]], k_ext], axis=1) for h in range(n_heads)]
        vs = [v_ref[rows, pair_lanes[h]] for h in range(n_heads)]
        zs = [lax.dot_general(qh, kh, (((1,), (1,)), ((), ())), preferred_element_type=F32)
              for qh, kh in zip(q_heads, ks)]
        ns = [jnp.maximum(z, 0.0) + jnp.log(1.0 + jnp.exp(_neg_abs(z))) for z in zs]
        nm = ns if causal is None else [jnp.where(causal, n, 0.0) for n in ns]
        cums = [[jnp.dot(n[:, s].astype(BF16), u, preferred_element_type=F32) for s in subs] for n in nm]
        tots = [[jnp.broadcast_to(cum[:, :1] + n[:, s.start:s.start + 1], (tq, PAIR)) for cum, s in zip(cm, subs)]
                for cm, n in zip(cums, nm)]
        ws = []
        for h in range(n_heads):
            run = c_ref[h]
            parts = [None] * n_sub
            for j in range(n_sub - 1, -1, -1):
                parts[j] = cums[h][j] + tile(run)
                run = run + tots[h][j]
            c_ref[h] = run
            w = jnp.exp((zs[h] - ns[h]) - jnp.concatenate(parts, axis=1))
            ws.append(w if causal is None else jnp.where(causal, w, 0.0))
        for h in range(n_heads):
            acc_ref[h] += jnp.dot(ws[h].astype(BF16), vs[h], preferred_element_type=F32)

    block(i, cc < rr)

    def body(it, carry):
        block(i - 1 - it, None)
        return carry

    lax.fori_loop(0, i, body, 0)
    o_ref[...] = jnp.concatenate(
        [jnp.where(m0, acc_ref[2 * hp], acc_ref[2 * hp + 1]) for hp in range(n_hp)], axis=1)


def sb_prompt(q_bf, k_bf, v_bf, bias, batch, seq):
    n, d_s = q_bf.shape
    n_hp = SB_PAIRS
    width = n_hp * PAIR
    n_grp = d_s // width
    tq = min(SB_TILE, seq)
    tk = min(SB_KEYS, seq)
    nq = seq // tq
    j = jnp.arange(tk)
    ux = (j[:, None] > j[None, :]).astype(BF16)
    grid_spec = pltpu.PrefetchScalarGridSpec(
        num_scalar_prefetch=1,
        grid=(batch, n_grp, nq),
        in_specs=[
            pl.BlockSpec((tq, width), lambda b, p, i, bias_ref: (b * nq + i, p)),
            pl.BlockSpec((seq, width), lambda b, p, i, bias_ref: (b, p)),
            pl.BlockSpec((seq, width), lambda b, p, i, bias_ref: (b, p)),
            pl.BlockSpec(ux.shape, lambda b, p, i, bias_ref: (0, 0)),
        ],
        out_specs=pl.BlockSpec((tq, width), lambda b, p, i, bias_ref: (b * nq + i, p)),
        scratch_shapes=[pltpu.VMEM((2 * n_hp, tq, PAIR), F32)] * 2,
    )
    return pl.pallas_call(
        functools.partial(_sb_prompt_kernel, tq=tq, tk=tk, n_hp=n_hp),
        grid_spec=grid_spec,
        out_shape=jax.ShapeDtypeStruct((n, d_s), F32),
        compiler_params=pltpu.CompilerParams(
            dimension_semantics=("arbitrary", "arbitrary", "arbitrary"), vmem_limit_bytes=VMEM_LIMIT),
        name="sb_prompt",
    )(bias, q_bf, k_bf, v_bf, ux)


def _sb_sample_kernel(pt_ref, q_ref, brow_ref, kn_ref, vn_ref, *refs, n_pg, n_heads):
    kps, vps = refs[:n_pg], refs[n_pg:2 * n_pg]
    ux_ref, o_ref, acc_ref, c_ref = refs[2 * n_pg:]
    j = pl.program_id(1)
    qbd = q_ref[...]
    nrow = qbd.shape[0]
    brow = brow_ref[...]
    ux = ux_ref[...]

    def log_terms(z, valid):
        nb = z.shape[1] // PAIR
        z = z + jnp.concatenate([brow] * nb, axis=1)
        sp = _softplus(z)
        l = -sp
        if valid is not None:
            l = jnp.where(valid, l, 0.0)
        lst = jnp.concatenate([l[:, bi * PAIR:(bi + 1) * PAIR] for bi in range(nb)], axis=0)
        return z - sp, _dot_hl(lst, ux)

    def weights(la, cx, run, valid):
        nb = la.shape[1] // PAIR
        parts = [None] * nb
        for bi in range(nb - 1, -1, -1):
            blk = cx[bi * nrow:(bi + 1) * nrow, :]
            parts[bi] = blk[:, :PAIR] + run
            run = run + blk[:, PAIR:]
        w = jnp.exp(la + jnp.concatenate(parts, axis=1))
        if valid is not None:
            w = jnp.where(valid, w, 0.0)
        return w.astype(BF16), run

    @pl.when(j == 0)
    def _():
        t_of_row = lax.broadcasted_iota(jnp.int32, (nrow, PAIR), 0) // n_heads
        key = lax.broadcasted_iota(jnp.int32, (nrow, PAIR), 1)
        valid = key < t_of_row
        la, cx = log_terms(_dot_nt(qbd, kn_ref[...]), valid)
        w, run = weights(la, cx, jnp.zeros((nrow, PAIR), F32), valid)
        acc_ref[...] = _dot(w, vn_ref[...])
        c_ref[...] = run

    d_s = qbd.shape[1]
    page_t = lambda r: r[...].reshape(d_s, PAIR).astype(BF16)
    n_grp = 2 if n_pg % 2 == 0 else 1
    per = n_pg // n_grp
    groups = [range(gi * per, (gi + 1) * per) for gi in range(n_grp)]
    zs = [jnp.concatenate([_dot(qbd, page_t(kps[pi])) for pi in grp], axis=1) for grp in groups]
    terms = [log_terms(z, None) for z in zs]
    run = c_ref[...]
    ws = [None] * n_grp
    for gi in range(n_grp - 1, -1, -1):
        ws[gi], run = weights(terms[gi][0], terms[gi][1], run, None)
    c_ref[...] = run
    out = None
    for grp, w in zip(groups, ws):
        for li, pi in enumerate(grp):
            o = _dot_nt(w[:, li * PAIR:(li + 1) * PAIR], page_t(vps[pi]))
            out = o if out is None else out + o
    acc_ref[...] += out

    @pl.when(j == pl.num_programs(1) - 1)
    def _():
        d_s = acc_ref.shape[1]
        head_of_row = lax.broadcasted_iota(jnp.int32, (nrow, d_s), 0) % n_heads
        head_of_lane = lax.broadcasted_iota(jnp.int32, (nrow, d_s), 1) // HEAD_DIM
        sel = jnp.where(head_of_row == head_of_lane, acc_ref[...], 0.0)
        o_ref[...] = jnp.sum(sel.reshape(nrow // n_heads, n_heads, d_s), axis=1)


def sb_sample(q_bd, brow, k_new, v_new, cache_k, cache_v, page_table, n_heads):
    bsz, nrow, d_s = q_bd.shape
    n_pages = page_table.shape[1]
    page = cache_k.shape[3]
    assert page == PAIR
    n_pg = min(SB_PAGES, n_pages)
    steps = n_pages // n_pg
    jj = jnp.arange(PAIR)
    ux = jnp.concatenate([(jj[:, None] > jj[None, :]).astype(F32), jnp.ones((PAIR, PAIR), F32)], axis=1).astype(BF16)

    def page_spec(i):
        return pl.BlockSpec((None, n_heads, HEAD_DIM, page),
                            lambda b, j, pt: (pt[b, n_pages - n_pg * (j + 1) + i], 0, 0, 0))

    per_b = lambda shape: pl.BlockSpec((None,) + shape, lambda b, j, pt: (b, 0, 0))
    grid_spec = pltpu.PrefetchScalarGridSpec(
        num_scalar_prefetch=1,
        grid=(bsz, steps),
        in_specs=[per_b((nrow, d_s)), pl.BlockSpec(brow.shape, lambda b, j, pt: (0, 0)),
                  per_b((PAIR, d_s)), per_b((PAIR, d_s))]
                 + [page_spec(i) for i in range(n_pg)] * 2
                 + [pl.BlockSpec(ux.shape, lambda b, j, pt: (0, 0))],
        out_specs=per_b((nrow // n_heads, d_s)),
        scratch_shapes=[pltpu.VMEM((nrow, d_s), F32), pltpu.VMEM((nrow, PAIR), F32)],
    )
    return pl.pallas_call(
        functools.partial(_sb_sample_kernel, n_pg=n_pg, n_heads=n_heads),
        grid_spec=grid_spec,
        out_shape=jax.ShapeDtypeStruct((bsz, nrow // n_heads, d_s), F32),
        compiler_params=pltpu.CompilerParams(
            dimension_semantics=("arbitrary", "arbitrary"), vmem_limit_bytes=VMEM_LIMIT),
        name="sb_sample",
    )(page_table, q_bd, brow, k_new, v_new, *([cache_k] * n_pg), *([cache_v] * n_pg), ux)


def _merge_kernel(h_ref, oa_ref, ob_ref, wg_ref, wba_ref, wbb_ref, wo_ref, g_ref, b_ref, o_ref, *, alpha):
    h = h_ref[...]
    d = h.shape[1]
    gates = jax.nn.sigmoid(jnp.dot(h.astype(BF16), wg_ref[...], preferred_element_type=F32))
    ma = jnp.dot(oa_ref[...].astype(BF16), wba_ref[...], preferred_element_type=F32)
    mb = jnp.dot(ob_ref[...].astype(BF16), wbb_ref[...], preferred_element_type=F32)
    merged = gates[:, :d] * ma + gates[:, d:] * mb
    y = alpha * h + jnp.dot(merged.astype(BF16), wo_ref[...], preferred_element_type=F32)
    o_ref[...] = _layer_norm(y, g_ref[...], b_ref[...])


def merge(h, o_a, o_b, w_gate_bf, w_ba_bf, w_bb_bf, w_out_bf, g, b, alpha):
    n, d = h.shape
    tm = min(ROW_TILE, n)
    row = lambda a: pl.BlockSpec((tm, a.shape[1]), lambda i: (i, 0))
    ws = [w_gate_bf, w_ba_bf, w_bb_bf, w_out_bf, g.reshape(1, d), b.reshape(1, d)]
    return pl.pallas_call(
        functools.partial(_merge_kernel, alpha=alpha),
        grid=(n // tm,),
        in_specs=[row(h), row(o_a), row(o_b)] + [_full_spec(a, 1) for a in ws],
        out_specs=pl.BlockSpec((tm, d), lambda i: (i, 0)),
        out_shape=jax.ShapeDtypeStruct((n, d), F32),
        compiler_params=pltpu.CompilerParams(
            dimension_semantics=("arbitrary",), vmem_limit_bytes=VMEM_LIMIT),
        name="merge",
    )(h, o_a, o_b, *ws)


def _layer(x, batch, seq, h_last, wkv0, sb_fn, w, alpha):
    n, d = x.shape
    h = ffn_ln(x, w["ffn1_in"], w["ffn1_out"], w["ln1_g"], w["ln1_b"], alpha)
    if h_last is None:
        hl_rows = None
        rows_pad = seq
    else:
        hl_rows = jnp.repeat(h_last, seq, axis=0)
        rows_pad = CHUNK
    r, lw, k, a, b, v, g = rwkv_prep(h, hl_rows, seq, w)
    n_h = w["w0"].shape[-1] // HEAD_DIM
    if wkv0 is None:
        s0 = jnp.zeros((batch, n_h, HEAD_DIM, HEAD_DIM), F32)
    else:
        s0 = wkv0.astype(F32)
    scan_in = [r, lw, k, a, b, v, g]
    if rows_pad != seq:
        pad = lambda t: jnp.pad(t.reshape(batch, seq, -1), ((0, 0), (0, rows_pad - seq), (0, 0))).reshape(
            batch * rows_pad, -1)
        scan_in = [pad(t) for t in scan_in]
    o_a, wkv = rwkv_scan(*scan_in, s0, w["r_k"], w["lnx_g"], w["lnx_b"], batch, rows_pad)
    if rows_pad != seq:
        o_a = o_a.reshape(batch, rows_pad, -1)[:, :seq].reshape(n, -1)
    q_bf, k_sb, v_sb, k_bf, v_bf = sb_proj(h, w["w_sb"], batch, seq, head_major=h_last is None)
    o_b = sb_fn(q_bf, k_sb, v_sb, k_bf, v_bf)
    x2 = merge(h, o_a, o_b, w["w_gate"], w["w_ba"], w["w_bb"], w["w_out"], w["ln2_g"], w["ln2_b"], alpha)
    x3 = ffn_ln(x2, w["ffn2_in"], w["ffn2_out"], w["ln3_g"], w["ln3_b"], alpha)
    return x3, k_sb, v_sb, wkv, h.reshape(batch, seq, d)[:, -1]


def kernel(x_prompt, x_sample, cache_k, cache_v, state_wkv, state_shift, page_table, ln1_g, ln1_b, ffn1_w_in, ffn1_w_out, w_in, mu_rkv, mu_wag, w0, w_w1, w_w2, a0, a_w1, a_w2, g_w1, g_w2, k_k, k_a, r_k, lnx_g, lnx_b, sb_bias, w_branch, w_out, ln2_g, ln2_b, ffn2_w_in, ffn2_w_out, ln3_g, ln3_b):
    depth = ln1_g.shape[0]
    alpha = (2.0 * depth) ** 0.25
    bp, tp, d = x_prompt.shape
    bs, ts, _ = x_sample.shape
    d_r = w0.shape[-1]
    d_s = (w_in.shape[-1] - 3 * d_r - 2 * d) // 3
    n_hs = d_s // HEAD_DIM
    blk = jnp.arange(d_r) // HEAD_DIM
    e_head = (blk[:, None] == blk[None, :]).astype(BF16)

    xp = x_prompt.reshape(bp * tp, d)
    xs = x_sample.reshape(bs * ts, d)
    outs = [[] for _ in range(8)]
    for l in range(depth):
        row = lambda a: a[l].reshape(1, -1)
        w = dict(
            ffn1_in=ffn1_w_in[l].astype(BF16), ffn1_out=ffn1_w_out[l].astype(BF16),
            ffn2_in=ffn2_w_in[l].astype(BF16), ffn2_out=ffn2_w_out[l].astype(BF16),
            ln1_g=ln1_g[l], ln1_b=ln1_b[l], ln2_g=ln2_g[l], ln2_b=ln2_b[l], ln3_g=ln3_g[l], ln3_b=ln3_b[l],
            w_rkv=w_in[l][:, :3 * d_r].astype(BF16),
            w_sb=w_in[l][:, 3 * d_r:3 * d_r + 3 * d_s].astype(BF16),
            w_gate=w_in[l][:, 3 * d_r + 3 * d_s:].astype(BF16),
            w_w1=w_w1[l].astype(BF16), w_w2=w_w2[l].astype(BF16), a_w1=a_w1[l].astype(BF16),
            a_w2=a_w2[l].astype(BF16), g_w1=g_w1[l].astype(BF16), g_w2=g_w2[l].astype(BF16),
            mu_rkv=row(mu_rkv), mu_wag=mu_wag[l], w0=row(w0), a0=row(a0), k_k=row(k_k), k_a=row(k_a),
            e_head=e_head, r_k=r_k[l].reshape(-1), lnx_g=lnx_g[l], lnx_b=lnx_b[l],
            w_ba=w_branch[l][:d_r].astype(BF16), w_bb=w_branch[l][d_r:].astype(BF16),
            w_out=w_out[l].astype(BF16),
        )
        bias = sb_bias[l].astype(F32)

        def prompt_sb(q_bf, k_sb, v_sb, k_bf, v_bf):
            return sb_prompt(q_bf, k_bf, v_bf, bias, bp, tp)

        def sample_sb(q_bf, k_sb, v_sb, k_bf, v_bf):
            head_of_lane = jnp.arange(d_s) // HEAD_DIM
            onehot = (jnp.arange(n_hs)[:, None] == head_of_lane[None, :])
            q_bd = jnp.where(onehot[None, None], q_bf.reshape(bs, ts, 1, d_s), jnp.zeros((), BF16))
            q_bd = q_bd.reshape(bs, ts * n_hs, d_s)
            brow = jnp.broadcast_to(jnp.tile(bias, ts)[:, None], (ts * n_hs, PAIR))
            padk = lambda t: jnp.pad(t.reshape(bs, ts, d_s), ((0, 0), (0, PAIR - ts), (0, 0)))
            ck = jnp.transpose(cache_k[l], (0, 2, 3, 1))
            cv = jnp.transpose(cache_v[l], (0, 2, 3, 1))
            o = sb_sample(q_bd, brow, padk(k_sb), padk(v_sb), ck, cv, page_table, n_hs)
            return o.reshape(bs * ts, d_s)

        xp, kp, vp, wp, hp = _layer(xp, bp, tp, None, None, prompt_sb, w, alpha)
        xs, ks_, vs_, ws_, hs_ = _layer(xs, bs, ts, state_shift[l].astype(F32), state_wkv[l], sample_sb, w, alpha)
        for lst, val in zip(outs, (jnp.transpose(kp, (0, 3, 1, 2)), jnp.transpose(vp, (0, 3, 1, 2)), wp, hp,
                                   ks_.reshape(bs, ts, n_hs, HEAD_DIM), vs_.reshape(bs, ts, n_hs, HEAD_DIM),
                                   ws_.astype(state_wkv.dtype), hs_)):
            lst.append(val)
    return (xp.reshape(bp, tp, d), xs.reshape(bs, ts, d)) + tuple(jnp.stack(o) for o in outs)
```

```python
import functools

import jax
import jax.numpy as jnp
from jax import lax
from jax.experimental import pallas as pl
from jax.experimental.pallas import tpu as pltpu

F32 = jnp.float32
BF16 = jnp.bfloat16

HEAD_DIM = 64
PAIR = 2 * HEAD_DIM
CHUNK = 64
LN_EPS = 1e-5
LNX_EPS = 64e-5
SB_SCALE = HEAD_DIM ** -0.5

ROW_TILE = 512
SCAN_ROWS = 1024
SCAN_GROUP = 4
SB_TILE = 512
SB_KEYS = 256
SB_PAIRS = 2
SB_PAGES = 16
VMEM_LIMIT = 56 * 1024 * 1024


def _dot(a, b):
    return jnp.dot(a.astype(BF16), b.astype(BF16), preferred_element_type=F32)


def _dot_nt(a, b):
    return lax.dot_general(a.astype(BF16), b.astype(BF16), (((1,), (1,)), ((), ())),
                           preferred_element_type=F32)


def _split(x):
    hi = x.astype(BF16)
    lo = (x - hi.astype(F32)).astype(BF16)
    return hi, lo


def _dot_hl(a, b_exact):
    hi, lo = _split(a)
    return (jnp.dot(hi, b_exact, preferred_element_type=F32)
            + jnp.dot(lo, b_exact, preferred_element_type=F32))


def _dot3(a, b, nt=False):
    ah, al = _split(a)
    bh, bl = _split(b)
    if nt:
        d = lambda x, y: lax.dot_general(x, y, (((1,), (1,)), ((), ())), preferred_element_type=F32)
    else:
        d = lambda x, y: jnp.dot(x, y, preferred_element_type=F32)
    return d(ah, bh) + d(al, bh) + d(ah, bl)


LOG2E = 1.4426950408889634


def _neg_abs(x):
    bits = pltpu.bitcast(x, jnp.uint32) | jnp.uint32(0x80000000)
    return pltpu.bitcast(bits, F32)


def _softplus(u):
    return jnp.maximum(u, 0.0) + jnp.log1p(jnp.exp(-jnp.abs(u)))


def _layer_norm(y, g, b):
    mu = jnp.mean(y, axis=-1, keepdims=True)
    yc = y - mu
    var = jnp.mean(yc * yc, axis=-1, keepdims=True)
    return yc * lax.rsqrt(var + LN_EPS) * g + b


def _full_spec(a, grid_rank):
    nd = a.ndim
    if grid_rank == 1:
        return pl.BlockSpec(a.shape, lambda i: (0,) * nd)
    if grid_rank == 2:
        return pl.BlockSpec(a.shape, lambda i, j: (0,) * nd)
    return pl.BlockSpec(a.shape, lambda i, j, k: (0,) * nd)


def _ffn_ln_kernel(x_ref, wg_ref, wu_ref, wo_ref, g_ref, b_ref, o_ref, acc_ref, *, alpha, n_ff):
    j = pl.program_id(1)

    @pl.when(j == 0)
    def _():
        acc_ref[...] = jnp.zeros_like(acc_ref)

    xb = x_ref[...].astype(BF16)
    gate = jnp.dot(xb, wg_ref[...], preferred_element_type=F32)
    up = jnp.dot(xb, wu_ref[...], preferred_element_type=F32)
    mid = gate * jax.nn.sigmoid(gate) * up
    acc_ref[...] += jnp.dot(mid.astype(BF16), wo_ref[...], preferred_element_type=F32)

    @pl.when(j == n_ff - 1)
    def _():
        y = alpha * x_ref[...] + 0.5 * acc_ref[...]
        o_ref[...] = _layer_norm(y, g_ref[...], b_ref[...])


def _ffn_tile(d_ff):
    for n in (2, 1, 11, 22):
        if d_ff % n == 0 and (d_ff // n) % 128 == 0:
            return d_ff // n
    return d_ff


def ffn_ln(x, w_in_bf, w_out_bf, g, b, alpha):
    n, d = x.shape
    d_ff = w_out_bf.shape[0]
    tm = min(ROW_TILE, n)
    tf = _ffn_tile(d_ff)
    n_ff = d_ff // tf
    return pl.pallas_call(
        functools.partial(_ffn_ln_kernel, alpha=alpha, n_ff=n_ff),
        grid=(n // tm, n_ff),
        in_specs=[
            pl.BlockSpec((tm, d), lambda i, j: (i, 0)),
            pl.BlockSpec((d, tf), lambda i, j: (0, j)),
            pl.BlockSpec((d, tf), lambda i, j: (0, j + n_ff)),
            pl.BlockSpec((tf, d), lambda i, j: (j, 0)),
            pl.BlockSpec((1, d), lambda i, j: (0, 0)),
            pl.BlockSpec((1, d), lambda i, j: (0, 0)),
        ],
        out_specs=pl.BlockSpec((tm, d), lambda i, j: (i, 0)),
        out_shape=jax.ShapeDtypeStruct((n, d), F32),
        scratch_shapes=[pltpu.VMEM((tm, d), F32)],
        compiler_params=pltpu.CompilerParams(
            dimension_semantics=("arbitrary", "arbitrary"), vmem_limit_bytes=VMEM_LIMIT),
        name="ffn_ln",
    )(x, w_in_bf, w_in_bf, w_out_bf, g.reshape(1, d), b.reshape(1, d))


N_PREP_PARAMS = 14


def _rwkv_prep_math(h, h_prev, p, p_prev, params, outs):
    (wrkv_ref, ww1_ref, ww2_ref, aw1_ref, aw2_ref, gw1_ref, gw2_ref,
     mu_rkv_ref, mu_wag_ref, w0_ref, a0_ref, kk_ref, ka_ref, e_ref) = params
    r_out, lw_out, k_out, a_out, b_out, v_out, g_out = outs
    d_r = w0_ref.shape[-1]
    rkv = p + (p_prev - p) * mu_rkv_ref[...]
    r = rkv[:, :d_r]
    k = rkv[:, d_r:2 * d_r]
    v = rkv[:, 2 * d_r:]
    dx = h_prev - h
    xw = h + dx * mu_wag_ref[0:1, :]
    xa = h + dx * mu_wag_ref[1:2, :]
    xg = h + dx * mu_wag_ref[2:3, :]
    lw = _dot(jnp.tanh(_dot(xw, ww1_ref[...])), ww2_ref[...])
    w_log = -_softplus(-(w0_ref[...] + lw)) - 0.5
    a_gate = jax.nn.sigmoid(a0_ref[...] + _dot(_dot(xa, aw1_ref[...]), aw2_ref[...]))
    g = _dot(jax.nn.sigmoid(_dot(xg, gw1_ref[...])), gw2_ref[...])
    kk = k * kk_ref[...]
    ss = _dot_hl(kk * kk, e_ref[...])
    kk = kk / jnp.maximum(jnp.sqrt(ss), 1e-12)
    r_out[...] = r
    lw_out[...] = -jnp.exp(w_log)
    k_out[...] = k * (1.0 + (a_gate - 1.0) * ka_ref[...])
    a_out[...] = -kk
    b_out[...] = kk * a_gate
    v_out[...] = v
    g_out[...] = g


def _rwkv_prep_prompt_kernel(h_ref, *refs, tiles_per_seq):
    params, outs = refs[:N_PREP_PARAMS], refs[N_PREP_PARAMS:N_PREP_PARAMS + 7]
    hcarry_ref, pcarry_ref = refs[-2:]
    i = pl.program_id(0)
    h = h_ref[...]
    tm = h.shape[0]

    @pl.when(i % tiles_per_seq == 0)
    def _():
        hcarry_ref[...] = jnp.zeros_like(hcarry_ref)
        pcarry_ref[...] = jnp.zeros_like(pcarry_ref)

    row = lax.broadcasted_iota(jnp.int32, (tm, 1), 0)
    p = jnp.dot(h.astype(BF16), params[0][...], preferred_element_type=F32)
    h_prev = jnp.where(row == 0, hcarry_ref[7:8, :], pltpu.roll(h, 1, 0))
    p_prev = jnp.where(row == 0, pcarry_ref[7:8, :], pltpu.roll(p, 1, 0))
    _rwkv_prep_math(h, h_prev, p, p_prev, params, outs)
    hcarry_ref[...] = h[tm - 8:, :]
    pcarry_ref[...] = p[tm - 8:, :]


def _rwkv_prep_sample_kernel(h_ref, hl_ref, *refs, seq):
    params, outs = refs[:N_PREP_PARAMS], refs[N_PREP_PARAMS:N_PREP_PARAMS + 7]
    h = h_ref[...]
    tm = h.shape[0]
    row = lax.broadcasted_iota(jnp.int32, (tm, 1), 0)
    first = row % seq == 0
    p = jnp.dot(h.astype(BF16), params[0][...], preferred_element_type=F32)
    p_last = jnp.dot(hl_ref[...].astype(BF16), params[0][...], preferred_element_type=F32)
    h_prev = jnp.where(first, hl_ref[...], pltpu.roll(h, 1, 0))
    p_prev = jnp.where(first, p_last, pltpu.roll(p, 1, 0))
    _rwkv_prep_math(h, h_prev, p, p_prev, params, outs)


def rwkv_prep(h, h_last_rows, seq, prm):
    n, d = h.shape
    d_r = prm["w0"].shape[-1]
    plist = [prm["w_rkv"], prm["w_w1"], prm["w_w2"], prm["a_w1"], prm["a_w2"], prm["g_w1"], prm["g_w2"],
             prm["mu_rkv"], prm["mu_wag"], prm["w0"], prm["a0"], prm["k_k"], prm["k_a"], prm["e_head"]]
    assert len(plist) == N_PREP_PARAMS
    if h_last_rows is None:
        tm = min(ROW_TILE, seq)
        kern = functools.partial(_rwkv_prep_prompt_kernel, tiles_per_seq=seq // tm)
        args = [h] + plist
        in_specs = [pl.BlockSpec((tm, d), lambda i: (i, 0))] + [_full_spec(a, 1) for a in plist]
        scratch = [pltpu.VMEM((8, d), F32), pltpu.VMEM((8, 3 * d_r), F32)]
    else:
        tm = n
        kern = functools.partial(_rwkv_prep_sample_kernel, seq=seq)
        args = [h, h_last_rows] + plist
        in_specs = [pl.BlockSpec((tm, d), lambda i: (i, 0))] * 2 + [_full_spec(a, 1) for a in plist]
        scratch = []
    return pl.pallas_call(
        kern,
        grid=(n // tm,),
        in_specs=in_specs,
        out_specs=[pl.BlockSpec((tm, d_r), lambda i: (i, 0))] * 7,
        out_shape=[jax.ShapeDtypeStruct((n, d_r), F32)] * 7,
        scratch_shapes=scratch,
        compiler_params=pltpu.CompilerParams(
            dimension_semantics=("arbitrary",), vmem_limit_bytes=VMEM_LIMIT),
        name="rwkv_prep",
    )(*args)


def _sb_proj_kernel(h_ref, w_ref, q_out, k_out, v_out, kb_out, vb_out, *, head_major):
    d_s = kb_out.shape[-1]
    p = jnp.dot(h_ref[...].astype(BF16), w_ref[...], preferred_element_type=F32)
    q_out[...] = (p[:, :d_s] * SB_SCALE).astype(BF16)
    k = p[:, d_s:2 * d_s]
    v = p[:, 2 * d_s:]
    if head_major:
        k_out[...] = k.T.reshape(k_out.shape)
        v_out[...] = v.T.reshape(v_out.shape)
    else:
        k_out[...] = k
        v_out[...] = v
    kb_out[...] = k.astype(BF16)
    vb_out[...] = v.astype(BF16)


def sb_proj(h, w_sb_bf, batch, seq, head_major):
    n, d = h.shape
    d_s = w_sb_bf.shape[1] // 3
    tm = min(ROW_TILE, seq if head_major else n)
    blk = lambda: pl.BlockSpec((tm, d_s), lambda i: (i, 0))
    if head_major:
        tps = seq // tm
        n_h = d_s // HEAD_DIM
        kv_spec = lambda: pl.BlockSpec((None, n_h, HEAD_DIM, tm), lambda i: (i // tps, 0, 0, i % tps))
        kv_shape = jax.ShapeDtypeStruct((batch, n_h, HEAD_DIM, seq), F32)
    else:
        kv_spec = blk
        kv_shape = jax.ShapeDtypeStruct((n, d_s), F32)
    return pl.pallas_call(
        functools.partial(_sb_proj_kernel, head_major=head_major),
        grid=(n // tm,),
        in_specs=[pl.BlockSpec((tm, d), lambda i: (i, 0)), _full_spec(w_sb_bf, 1)],
        out_specs=[blk(), kv_spec(), kv_spec(), blk(), blk()],
        out_shape=[jax.ShapeDtypeStruct((n, d_s), BF16), kv_shape, kv_shape,
                   jax.ShapeDtypeStruct((n, d_s), BF16), jax.ShapeDtypeStruct((n, d_s), BF16)],
        compiler_params=pltpu.CompilerParams(
            dimension_semantics=("arbitrary",), vmem_limit_bytes=VMEM_LIMIT),
        name="sb_proj",
    )(h, w_sb_bf)


def _scan_group(n_chunks):
    for g in (SCAN_GROUP, 2, 1):
        if n_chunks % g == 0:
            return g


def _stack_heads(x, m0):
    return jnp.concatenate([jnp.where(m0, x, 0.0), jnp.where(m0, 0.0, x)], axis=0)


def _rwkv_scan_kernel(r_ref, lw_ref, k_ref, a_ref, b_ref, v_ref, g_ref, s0_ref,
                      rk_ref, lg_ref, lb_ref, tri_ref, emean_ref, eones_ref,
                      o_ref, sfin_ref, s_ref, ta_ref, tv_ref, mr_ref, cl_ref, *, n_chunks, n_pairs):
    i = pl.program_id(1)
    c2 = 2 * CHUNK

    @pl.when(i == 0)
    def _():
        zero = jnp.zeros((HEAD_DIM, HEAD_DIM), F32)
        for p in range(n_pairs):
            top = jnp.concatenate([s0_ref[2 * p], zero], axis=1)
            bot = jnp.concatenate([zero, s0_ref[2 * p + 1]], axis=1)
            s_ref[p] = jnp.concatenate([top, bot], axis=0)

    lane = lax.broadcasted_iota(jnp.int32, (1, PAIR), 1)
    m0 = lane < HEAD_DIM
    rr = lax.broadcasted_iota(jnp.int32, (c2, c2), 0)
    cc = lax.broadcasted_iota(jnp.int32, (c2, c2), 1)
    strict = cc < rr
    incl = cc <= rr
    eye = (cc == rr).astype(F32)
    tri = tri_ref[...]
    mm = lambda x, y: jnp.dot(x, y, preferred_element_type=F32)
    mm_nt = lambda x, y: lax.dot_general(x, y, (((1,), (1,)), ((), ())), preferred_element_type=F32)

    emean = emean_ref[...]
    eones = eones_ref[...]
    r_k = rk_ref[...]
    lnx_g = lg_ref[...]
    lnx_b = lb_ref[...]

    pairs = range(n_pairs)
    cat0 = lambda x, y: jnp.concatenate([x, y], axis=0)
    stack = lambda x: _stack_heads(x, m0).astype(BF16)
    in_refs = (r_ref, lw_ref, k_ref, a_ref, b_ref, v_ref, g_ref)

    def load_rows(c):
        rows = pl.ds(pl.multiple_of(c * CHUNK, CHUNK), CHUNK)
        return rows, [ref[rows, :] for ref in in_refs]

    def pre_prep(ins, p):
        lanes = slice(p * PAIR, (p + 1) * PAIR)
        r, lw, k, a, b, v, g = (x[:, lanes] for x in ins)
        hi, lo = _split(lw)
        lo2 = (lw - hi.astype(F32) - lo.astype(F32)).astype(BF16)
        cl = mm(tri, hi) + mm(tri, lo) + mm(tri, lo2)
        w_inv = jnp.exp(-cl)
        return dict(cl=cl, vst=stack(v), ast=stack(a * jnp.exp(cl - lw)), rst=stack(r * jnp.exp(cl)),
                    bst=stack(b * w_inv), kst=stack(k * w_inv))

    def state_prep(ins, cl, p):
        lanes = slice(p * PAIR, (p + 1) * PAIR)
        r, lw, k, a, b, v, g = (x[:, lanes] for x in ins)
        cl_end = cl[CHUNK - 1:CHUNK, :]
        w_tail = jnp.exp(cl_end - cl)
        vst_f = _stack_heads(v, m0)
        return dict(r=r, k=k, v=v, g=g, lanes=lanes, w_end=jnp.exp(cl_end), vst_f=vst_f, vst=vst_f.astype(BF16),
                    rst=stack(r * jnp.exp(cl)), btl=stack(b * w_tail), ktl=stack(k * w_tail))

    grp = _scan_group(n_chunks)
    n_grp = n_chunks // grp

    def step(g_state, g_pre, slot):
        todo = []
        if g_state is not None:
            s_cur = [s_ref[p] for p in pairs]
            done = []
            for j in range(grp):
                def first(j=j):
                    rows, ins = load_rows(g_state * grp + j)
                    s = list(s_cur)
                    sb = [x.astype(BF16) for x in s]
                    u = [mm_nt(ta_ref[slot, j, p], sb[p]) + tv_ref[slot, j, p] for p in pairs]
                    return rows, ins, s, sb, u

                def second(st, j=j):
                    rows, ins, s, sb, u = st
                    e = [state_prep(ins, cl_ref[slot, j, p], p) for p in pairs]
                    uv_t = [cat0(u[p], e[p]["vst_f"]).T.astype(BF16) for p in pairs]
                    for p in pairs:
                        s_cur[p] = s[p] * e[p]["w_end"] + mm(uv_t[p], cat0(e[p]["btl"], e[p]["ktl"]))
                    y = [mm_nt(e[p]["rst"], sb[p])
                         + mm(mr_ref[slot, j, p], cat0(u[p].astype(BF16), e[p]["vst"])) for p in pairs]
                    done.append((rows, e, y))
                todo.append((first, second))

        def run_state_piece(state):
            idx, pending = state
            if idx >= len(todo):
                return state
            if pending is None:
                return idx, todo[idx][0]()
            todo[idx][1](pending)
            return idx + 1, None

        cursor = (0, None)
        if g_pre is not None:
            chains = [(j, p) for j in range(grp) for p in pairs]
            ins_p = [load_rows(g_pre * grp + j)[1] for j in range(grp)]
            cursor = run_state_piece(cursor)
            d = [pre_prep(ins_p[j], p) for j, p in chains]
            m4 = [mm_nt(cat0(x["ast"], x["rst"]), cat0(x["bst"], x["kst"])) for x in d]
            cursor = run_state_piece(cursor)
            nab = [jnp.where(strict, m[:c2, :c2], 0.0) for m in m4]
            mak = [jnp.where(strict, m[:c2, c2:], 0.0).astype(BF16) for m in m4]
            mrbk = [jnp.concatenate([jnp.where(incl, m[c2:, :c2], 0.0), jnp.where(incl, m[c2:, c2:], 0.0)],
                                    axis=1).astype(BF16) for m in m4]
            t = [eye + n for n in nab]
            pw = [n.astype(BF16) for n in nab]
            for _ in range(5):
                pw = [mm(x, x).astype(BF16) for x in pw]
                cursor = run_state_piece(cursor)
                t = [tt + mm(x, tt.astype(BF16)) for x, tt in zip(pw, t)]
                cursor = run_state_piece(cursor)
            mv = [mm(mk, x["vst"]) for mk, x in zip(mak, d)]
            tx = [mm(tt.astype(BF16), jnp.concatenate([x["ast"], m.astype(BF16)], axis=1))
                  for tt, x, m in zip(t, d, mv)]
        while cursor[0] < len(todo):
            cursor = run_state_piece(cursor)
        if g_state is not None:
            for rows, e, y in done:
                yp = [x[:CHUNK, :] + x[CHUNK:, :] for x in y]
                mu = [_dot_hl(x, emean) for x in yp]
                yc = [x - m for x, m in zip(yp, mu)]
                var = [_dot_hl(x * x, emean) for x in yc]
                bonus = [_dot_hl(x["r"] * x["k"] * r_k[:, x["lanes"]], eones) * x["v"] for x in e]
                outs = [(yc[p] * lax.rsqrt(var[p] + LNX_EPS) * lnx_g[:, e[p]["lanes"]] + lnx_b[:, e[p]["lanes"]]
                         + bonus[p]) * e[p]["g"] for p in pairs]
                o_ref[rows, :] = jnp.concatenate(outs, axis=1)
            for p in pairs:
                s_ref[p] = s_cur[p]
        if g_pre is not None:
            for (j, p), x, txx, mr in zip(chains, d, tx, mrbk):
                ta_ref[1 - slot, j, p] = txx[:, :PAIR].astype(BF16)
                tv_ref[1 - slot, j, p] = txx[:, PAIR:]
                mr_ref[1 - slot, j, p] = mr
                cl_ref[1 - slot, j, p] = x["cl"]

    step(None, 0, 1)

    def body(gi, carry):
        step(gi, gi + 1, lax.rem(gi, 2))
        return carry

    lax.fori_loop(0, n_grp - 1, body, 0)
    step(n_grp - 1, None, (n_grp - 1) % 2)
    for p in range(n_pairs):
        sfin_ref[2 * p] = s_ref[p, :HEAD_DIM, :HEAD_DIM]
        sfin_ref[2 * p + 1] = s_ref[p, HEAD_DIM:, HEAD_DIM:]


def rwkv_scan(r, lw, k, a, b, v, g, s0, r_k, lnx_g, lnx_b, batch, seq):
    n, d_r = r.shape
    n_pairs = d_r // PAIR
    rows = min(SCAN_ROWS, seq)
    steps = seq // rows
    grp = _scan_group(rows // CHUNK)
    tri = jnp.tril(jnp.ones((CHUNK, CHUNK), F32)).astype(BF16)
    blk = jnp.arange(PAIR) // HEAD_DIM
    same = (blk[:, None] == blk[None, :]).astype(F32)
    emean = (same / HEAD_DIM).astype(BF16)
    eones = same.astype(BF16)
    tok = pl.BlockSpec((rows, d_r), lambda bi, i: (bi * steps + i, 0))
    par = pl.BlockSpec((1, d_r), lambda bi, i: (0, 0))
    st = pl.BlockSpec((None, 2 * n_pairs, HEAD_DIM, HEAD_DIM), lambda bi, i: (bi, 0, 0, 0))
    return pl.pallas_call(
        functools.partial(_rwkv_scan_kernel, n_chunks=rows // CHUNK, n_pairs=n_pairs),
        grid=(batch, steps),
        in_specs=[tok] * 7 + [st, par, par, par, _full_spec(tri, 2), _full_spec(emean, 2), _full_spec(eones, 2)],
        out_specs=[tok, st],
        out_shape=[jax.ShapeDtypeStruct((n, d_r), F32),
                   jax.ShapeDtypeStruct((batch, 2 * n_pairs, HEAD_DIM, HEAD_DIM), F32)],
        scratch_shapes=[pltpu.VMEM((n_pairs, PAIR, PAIR), F32),
                        pltpu.VMEM((2, grp, n_pairs, PAIR, PAIR), BF16),
                        pltpu.VMEM((2, grp, n_pairs, PAIR, PAIR), F32),
                        pltpu.VMEM((2, grp, n_pairs, PAIR, 2 * PAIR), BF16),
                        pltpu.VMEM((2, grp, n_pairs, CHUNK, PAIR), F32)],
        compiler_params=pltpu.CompilerParams(
            dimension_semantics=("arbitrary", "arbitrary"), vmem_limit_bytes=VMEM_LIMIT),
        name="rwkv_scan",
    )(r, lw, k, a, b, v, g, s0, r_k.reshape(1, d_r), lnx_g.reshape(1, d_r), lnx_b.reshape(1, d_r),
      tri, emean, eones)


def _sb_prompt_kernel(bias_ref, q_ref, k_ref, v_ref, u_ref, o_ref, acc_ref, c_ref, *, tq, tk, n_hp):
    g = pl.program_id(1)
    i = pl.program_id(2)
    n_sub = tq // tk
    n_heads = 2 * n_hp
    lane = lax.broadcasted_iota(jnp.int32, (1, PAIR), 1)
    m0 = lane < HEAD_DIM
    pair_lanes = [slice((h // 2) * PAIR, (h // 2 + 1) * PAIR) for h in range(n_heads)]
    q_heads = []
    for h in range(n_heads):
        qp = q_ref[:, pair_lanes[h]]
        qh = jnp.where(m0 if h % 2 == 0 else ~m0, qp, jnp.zeros_like(qp))
        bias = jnp.full((1, PAIR), bias_ref[n_heads * g + h], F32)
        b1 = bias.astype(BF16).astype(F32)
        b2 = (bias - b1).astype(BF16).astype(F32)
        b3 = (bias - b1) - b2
        ext = jnp.where(lane == 0, b1, jnp.where(lane == 1, b2, jnp.where(lane == 2, b3, 0.0)))
        q_heads.append(jnp.concatenate([qh, jnp.broadcast_to(ext, qh.shape).astype(BF16)], axis=1))
    k_ext = jnp.broadcast_to(jnp.where(lane < 3, 1.0, 0.0), (tq, PAIR)).astype(BF16)
    acc_ref[...] = jnp.zeros_like(acc_ref)
    c_ref[...] = jnp.zeros_like(c_ref)
    u = u_ref[...]
    rr = lax.broadcasted_iota(jnp.int32, (tq, tq), 0)
    cc = lax.broadcasted_iota(jnp.int32, (tq, tq), 1)
    subs = [slice(j * tk, (j + 1) * tk) for j in range(n_sub)]
    tile = lambda x: jnp.concatenate([x] * (tk // PAIR), axis=1)

    def block(kb, causal):
        rows = pl.ds(pl.multiple_of(kb * tq, tq), tq)
        ks = [jnp.concatenate([k_ref[rows, pair_lanes[h]], k_ext], axis=1) for h in range(n_heads)]
        vs = [v_ref[rows, pair_lanes[h]] for h in range(n_heads)]
        zs = [lax.dot_general(qh, kh, (((1,), (1,)), ((), ())), preferred_element_type=F32)
              for qh, kh in zip(q_heads, ks)]
        ns = [jnp.maximum(z, 0.0) + jnp.log(1.0 + jnp.exp(_neg_abs(z))) for z in zs]
        nm = ns if causal is None else [jnp.where(causal, n, 0.0) for n in ns]
        cums = [[jnp.dot(n[:, s].astype(BF16), u, preferred_element_type=F32) for s in subs] for n in nm]
        tots = [[jnp.broadcast_to(cum[:, :1] + n[:, s.start:s.start + 1], (tq, PAIR)) for cum, s in zip(cm, subs)]
                for cm, n in zip(cums, nm)]
        ws = []
        for h in range(n_heads):
            run = c_ref[h]
            parts = [None] * n_sub
            for j in range(n_sub - 1, -1, -1):
                parts[j] = cums[h][j] + tile(run)
                run = run + tots[h][j]
            c_ref[h] = run
            w = jnp.exp((zs[h] - ns[h]) - jnp.concatenate(parts, axis=1))
            ws.append(w if causal is None else jnp.where(causal, w, 0.0))
        for h in range(n_heads):
            acc_ref[h] += jnp.dot(ws[h].astype(BF16), vs[h], preferred_element_type=F32)

    block(i, cc < rr)

    def body(it, carry):
        block(i - 1 - it, None)
        return carry

    lax.fori_loop(0, i, body, 0)
    o_ref[...] = jnp.concatenate(
        [jnp.where(m0, acc_ref[2 * hp], acc_ref[2 * hp + 1]) for hp in range(n_hp)], axis=1)


def sb_prompt(q_bf, k_bf, v_bf, bias, batch, seq):
    n, d_s = q_bf.shape
    n_hp = SB_PAIRS
    width = n_hp * PAIR
    n_grp = d_s // width
    tq = min(SB_TILE, seq)
    tk = min(SB_KEYS, seq)
    nq = seq // tq
    j = jnp.arange(tk)
    ux = (j[:, None] > j[None, :]).astype(BF16)
    grid_spec = pltpu.PrefetchScalarGridSpec(
        num_scalar_prefetch=1,
        grid=(batch, n_grp, nq),
        in_specs=[
            pl.BlockSpec((tq, width), lambda b, p, i, bias_ref: (b * nq + i, p)),
            pl.BlockSpec((seq, width), lambda b, p, i, bias_ref: (b, p)),
            pl.BlockSpec((seq, width), lambda b, p, i, bias_ref: (b, p)),
            pl.BlockSpec(ux.shape, lambda b, p, i, bias_ref: (0, 0)),
        ],
        out_specs=pl.BlockSpec((tq, width), lambda b, p, i, bias_ref: (b * nq + i, p)),
        scratch_shapes=[pltpu.VMEM((2 * n_hp, tq, PAIR), F32)] * 2,
    )
    return pl.pallas_call(
        functools.partial(_sb_prompt_kernel, tq=tq, tk=tk, n_hp=n_hp),
        grid_spec=grid_spec,
        out_shape=jax.ShapeDtypeStruct((n, d_s), F32),
        compiler_params=pltpu.CompilerParams(
            dimension_semantics=("arbitrary", "arbitrary", "arbitrary"), vmem_limit_bytes=VMEM_LIMIT),
        name="sb_prompt",
    )(bias, q_bf, k_bf, v_bf, ux)


def _sb_sample_kernel(pt_ref, q_ref, brow_ref, kn_ref, vn_ref, *refs, n_pg, n_heads):
    kps, vps = refs[:n_pg], refs[n_pg:2 * n_pg]
    ux_ref, o_ref, acc_ref, c_ref = refs[2 * n_pg:]
    j = pl.program_id(1)
    qbd = q_ref[...]
    nrow = qbd.shape[0]
    brow = brow_ref[...]
    ux = ux_ref[...]

    def log_terms(z, valid):
        nb = z.shape[1] // PAIR
        z = z + jnp.concatenate([brow] * nb, axis=1)
        sp = _softplus(z)
        l = -sp
        if valid is not None:
            l = jnp.where(valid, l, 0.0)
        lst = jnp.concatenate([l[:, bi * PAIR:(bi + 1) * PAIR] for bi in range(nb)], axis=0)
        return z - sp, _dot_hl(lst, ux)

    def weights(la, cx, run, valid):
        nb = la.shape[1] // PAIR
        parts = [None] * nb
        for bi in range(nb - 1, -1, -1):
            blk = cx[bi * nrow:(bi + 1) * nrow, :]
            parts[bi] = blk[:, :PAIR] + run
            run = run + blk[:, PAIR:]
        w = jnp.exp(la + jnp.concatenate(parts, axis=1))
        if valid is not None:
            w = jnp.where(valid, w, 0.0)
        return w.astype(BF16), run

    @pl.when(j == 0)
    def _():
        t_of_row = lax.broadcasted_iota(jnp.int32, (nrow, PAIR), 0) // n_heads
        key = lax.broadcasted_iota(jnp.int32, (nrow, PAIR), 1)
        valid = key < t_of_row
        la, cx = log_terms(_dot_nt(qbd, kn_ref[...]), valid)
        w, run = weights(la, cx, jnp.zeros((nrow, PAIR), F32), valid)
        acc_ref[...] = _dot(w, vn_ref[...])
        c_ref[...] = run

    d_s = qbd.shape[1]
    page_t = lambda r: r[...].reshape(d_s, PAIR).astype(BF16)
    n_grp = 2 if n_pg % 2 == 0 else 1
    per = n_pg // n_grp
    groups = [range(gi * per, (gi + 1) * per) for gi in range(n_grp)]
    zs = [jnp.concatenate([_dot(qbd, page_t(kps[pi])) for pi in grp], axis=1) for grp in groups]
    terms = [log_terms(z, None) for z in zs]
    run = c_ref[...]
    ws = [None] * n_grp
    for gi in range(n_grp - 1, -1, -1):
        ws[gi], run = weights(terms[gi][0], terms[gi][1], run, None)
    c_ref[...] = run
    out = None
    for grp, w in zip(groups, ws):
        for li, pi in enumerate(grp):
            o = _dot_nt(w[:, li * PAIR:(li + 1) * PAIR], page_t(vps[pi]))
            out = o if out is None else out + o
    acc_ref[...] += out

    @pl.when(j == pl.num_programs(1) - 1)
    def _():
        d_s = acc_ref.shape[1]
        head_of_row = lax.broadcasted_iota(jnp.int32, (nrow, d_s), 0) % n_heads
        head_of_lane = lax.broadcasted_iota(jnp.int32, (nrow, d_s), 1) // HEAD_DIM
        sel = jnp.where(head_of_row == head_of_lane, acc_ref[...], 0.0)
        o_ref[...] = jnp.sum(sel.reshape(nrow // n_heads, n_heads, d_s), axis=1)


def sb_sample(q_bd, brow, k_new, v_new, cache_k, cache_v, page_table, n_heads):
    bsz, nrow, d_s = q_bd.shape
    n_pages = page_table.shape[1]
    page = cache_k.shape[3]
    assert page == PAIR
    n_pg = min(SB_PAGES, n_pages)
    steps = n_pages // n_pg
    jj = jnp.arange(PAIR)
    ux = jnp.concatenate([(jj[:, None] > jj[None, :]).astype(F32), jnp.ones((PAIR, PAIR), F32)], axis=1).astype(BF16)

    def page_spec(i):
        return pl.BlockSpec((None, n_heads, HEAD_DIM, page),
                            lambda b, j, pt: (pt[b, n_pages - n_pg * (j + 1) + i], 0, 0, 0))

    per_b = lambda shape: pl.BlockSpec((None,) + shape, lambda b, j, pt: (b, 0, 0))
    grid_spec = pltpu.PrefetchScalarGridSpec(
        num_scalar_prefetch=1,
        grid=(bsz, steps),
        in_specs=[per_b((nrow, d_s)), pl.BlockSpec(brow.shape, lambda b, j, pt: (0, 0)),
                  per_b((PAIR, d_s)), per_b((PAIR, d_s))]
                 + [page_spec(i) for i in range(n_pg)] * 2
                 + [pl.BlockSpec(ux.shape, lambda b, j, pt: (0, 0))],
        out_specs=per_b((nrow // n_heads, d_s)),
        scratch_shapes=[pltpu.VMEM((nrow, d_s), F32), pltpu.VMEM((nrow, PAIR), F32)],
    )
    return pl.pallas_call(
        functools.partial(_sb_sample_kernel, n_pg=n_pg, n_heads=n_heads),
        grid_spec=grid_spec,
        out_shape=jax.ShapeDtypeStruct((bsz, nrow // n_heads, d_s), F32),
        compiler_params=pltpu.CompilerParams(
            dimension_semantics=("arbitrary", "arbitrary"), vmem_limit_bytes=VMEM_LIMIT),
        name="sb_sample",
    )(page_table, q_bd, brow, k_new, v_new, *([cache_k] * n_pg), *([cache_v] * n_pg), ux)


def _merge_kernel(h_ref, oa_ref, ob_ref, wg_ref, wba_ref, wbb_ref, wo_ref, g_ref, b_ref, o_ref, *, alpha):
    h = h_ref[...]
    d = h.shape[1]
    gates = jax.nn.sigmoid(jnp.dot(h.astype(BF16), wg_ref[...], preferred_element_type=F32))
    ma = jnp.dot(oa_ref[...].astype(BF16), wba_ref[...], preferred_element_type=F32)
    mb = jnp.dot(ob_ref[...].astype(BF16), wbb_ref[...], preferred_element_type=F32)
    merged = gates[:, :d] * ma + gates[:, d:] * mb
    y = alpha * h + jnp.dot(merged.astype(BF16), wo_ref[...], preferred_element_type=F32)
    o_ref[...] = _layer_norm(y, g_ref[...], b_ref[...])


def merge(h, o_a, o_b, w_gate_bf, w_ba_bf, w_bb_bf, w_out_bf, g, b, alpha):
    n, d = h.shape
    tm = min(ROW_TILE, n)
    row = lambda a: pl.BlockSpec((tm, a.shape[1]), lambda i: (i, 0))
    ws = [w_gate_bf, w_ba_bf, w_bb_bf, w_out_bf, g.reshape(1, d), b.reshape(1, d)]
    return pl.pallas_call(
        functools.partial(_merge_kernel, alpha=alpha),
        grid=(n // tm,),
        in_specs=[row(h), row(o_a), row(o_b)] + [_full_spec(a, 1) for a in ws],
        out_specs=pl.BlockSpec((tm, d), lambda i: (i, 0)),
        out_shape=jax.ShapeDtypeStruct((n, d), F32),
        compiler_params=pltpu.CompilerParams(
            dimension_semantics=("arbitrary",), vmem_limit_bytes=VMEM_LIMIT),
        name="merge",
    )(h, o_a, o_b, *ws)


def _layer(x, batch, seq, h_last, wkv0, sb_fn, w, alpha):
    n, d = x.shape
    h = ffn_ln(x, w["ffn1_in"], w["ffn1_out"], w["ln1_g"], w["ln1_b"], alpha)
    if h_last is None:
        hl_rows = None
        rows_pad = seq
    else:
        hl_rows = jnp.repeat(h_last, seq, axis=0)
        rows_pad = CHUNK
    r, lw, k, a, b, v, g = rwkv_prep(h, hl_rows, seq, w)
    n_h = w["w0"].shape[-1] // HEAD_DIM
    if wkv0 is None:
        s0 = jnp.zeros((batch, n_h, HEAD_DIM, HEAD_DIM), F32)
    else:
        s0 = wkv0.astype(F32)
    scan_in = [r, lw, k, a, b, v, g]
    if rows_pad != seq:
        pad = lambda t: jnp.pad(t.reshape(batch, seq, -1), ((0, 0), (0, rows_pad - seq), (0, 0))).reshape(
            batch * rows_pad, -1)
        scan_in = [pad(t) for t in scan_in]
    o_a, wkv = rwkv_scan(*scan_in, s0, w["r_k"], w["lnx_g"], w["lnx_b"], batch, rows_pad)
    if rows_pad != seq:
        o_a = o_a.reshape(batch, rows_pad, -1)[:, :seq].reshape(n, -1)
    q_bf, k_sb, v_sb, k_bf, v_bf = sb_proj(h, w["w_sb"], batch, seq, head_major=h_last is None)
    o_b = sb_fn(q_bf, k_sb, v_sb, k_bf, v_bf)
    x2 = merge(h, o_a, o_b, w["w_gate"], w["w_ba"], w["w_bb"], w["w_out"], w["ln2_g"], w["ln2_b"], alpha)
    x3 = ffn_ln(x2, w["ffn2_in"], w["ffn2_out"], w["ln3_g"], w["ln3_b"], alpha)
    return x3, k_sb, v_sb, wkv, h.reshape(batch, seq, d)[:, -1]


def kernel(x_prompt, x_sample, cache_k, cache_v, state_wkv, state_shift, page_table, ln1_g, ln1_b, ffn1_w_in, ffn1_w_out, w_in, mu_rkv, mu_wag, w0, w_w1, w_w2, a0, a_w1, a_w2, g_w1, g_w2, k_k, k_a, r_k, lnx_g, lnx_b, sb_bias, w_branch, w_out, ln2_g, ln2_b, ffn2_w_in, ffn2_w_out, ln3_g, ln3_b):
    depth = ln1_g.shape[0]
    alpha = (2.0 * depth) ** 0.25
    bp, tp, d = x_prompt.shape
    bs, ts, _ = x_sample.shape
    d_r = w0.shape[-1]
    d_s = (w_in.shape[-1] - 3 * d_r - 2 * d) // 3
    n_hs = d_s // HEAD_DIM
    blk = jnp.arange(d_r) // HEAD_DIM
    e_head = (blk[:, None] == blk[None, :]).astype(BF16)

    xp = x_prompt.reshape(bp * tp, d)
    xs = x_sample.reshape(bs * ts, d)
    outs = [[] for _ in range(8)]
    for l in range(depth):
        row = lambda a: a[l].reshape(1, -1)
        w = dict(
            ffn1_in=ffn1_w_in[l].astype(BF16), ffn1_out=ffn1_w_out[l].astype(BF16),
            ffn2_in=ffn2_w_in[l].astype(BF16), ffn2_out=ffn2_w_out[l].astype(BF16),
            ln1_g=ln1_g[l], ln1_b=ln1_b[l], ln2_g=ln2_g[l], ln2_b=ln2_b[l], ln3_g=ln3_g[l], ln3_b=ln3_b[l],
            w_rkv=w_in[l][:, :3 * d_r].astype(BF16),
            w_sb=w_in[l][:, 3 * d_r:3 * d_r + 3 * d_s].astype(BF16),
            w_gate=w_in[l][:, 3 * d_r + 3 * d_s:].astype(BF16),
            w_w1=w_w1[l].astype(BF16), w_w2=w_w2[l].astype(BF16), a_w1=a_w1[l].astype(BF16),
            a_w2=a_w2[l].astype(BF16), g_w1=g_w1[l].astype(BF16), g_w2=g_w2[l].astype(BF16),
            mu_rkv=row(mu_rkv), mu_wag=mu_wag[l], w0=row(w0), a0=row(a0), k_k=row(k_k), k_a=row(k_a),
            e_head=e_head, r_k=r_k[l].reshape(-1), lnx_g=lnx_g[l], lnx_b=lnx_b[l],
            w_ba=w_branch[l][:d_r].astype(BF16), w_bb=w_branch[l][d_r:].astype(BF16),
            w_out=w_out[l].astype(BF16),
        )
        bias = sb_bias[l].astype(F32)

        def prompt_sb(q_bf, k_sb, v_sb, k_bf, v_bf):
            return sb_prompt(q_bf, k_bf, v_bf, bias, bp, tp)

        def sample_sb(q_bf, k_sb, v_sb, k_bf, v_bf):
            head_of_lane = jnp.arange(d_s) // HEAD_DIM
            onehot = (jnp.arange(n_hs)[:, None] == head_of_lane[None, :])
            q_bd = jnp.where(onehot[None, None], q_bf.reshape(bs, ts, 1, d_s), jnp.zeros((), BF16))
            q_bd = q_bd.reshape(bs, ts * n_hs, d_s)
            brow = jnp.broadcast_to(jnp.tile(bias, ts)[:, None], (ts * n_hs, PAIR))
            padk = lambda t: jnp.pad(t.reshape(bs, ts, d_s), ((0, 0), (0, PAIR - ts), (0, 0)))
            ck = jnp.transpose(cache_k[l], (0, 2, 3, 1))
            cv = jnp.transpose(cache_v[l], (0, 2, 3, 1))
            o = sb_sample(q_bd, brow, padk(k_sb), padk(v_sb), ck, cv, page_table, n_hs)
            return o.reshape(bs * ts, d_s)

        xp, kp, vp, wp, hp = _layer(xp, bp, tp, None, None, prompt_sb, w, alpha)
        xs, ks_, vs_, ws_, hs_ = _layer(xs, bs, ts, state_shift[l].astype(F32), state_wkv[l], sample_sb, w, alpha)
        for lst, val in zip(outs, (jnp.transpose(kp, (0, 3, 1, 2)), jnp.transpose(vp, (0, 3, 1, 2)), wp, hp,
                                   ks_.reshape(bs, ts, n_hs, HEAD_DIM), vs_.reshape(bs, ts, n_hs, HEAD_DIM),
                                   ws_.astype(state_wkv.dtype), hs_)):
            lst.append(val)
    return (xp.reshape(bp, tp, d), xs.reshape(bs, ts, d)) + tuple(jnp.stack(o) for o in outs)
```

```python
import functools

import jax
import jax.numpy as jnp
from jax import lax
from jax.experimental import pallas as pl
from jax.experimental.pallas import tpu as pltpu

F32 = jnp.float32
BF16 = jnp.bfloat16

HEAD_DIM = 64
PAIR = 2 * HEAD_DIM
CHUNK = 64
LN_EPS = 1e-5
LNX_EPS = 64e-5
SB_SCALE = HEAD_DIM ** -0.5

ROW_TILE = 512
FFN_ROWS = 1024
FFN_SUB = 512
SCAN_ROWS = 1024
SCAN_GROUP = 4
SB_TILE = 512
SB_KEYS = 256
SB_PAIRS = 2
SB_PAGES = 16
VMEM_LIMIT = 56 * 1024 * 1024


def _dot(a, b):
    return jnp.dot(a.astype(BF16), b.astype(BF16), preferred_element_type=F32)


def _dot_nt(a, b):
    return lax.dot_general(a.astype(BF16), b.astype(BF16), (((1,), (1,)), ((), ())),
                           preferred_element_type=F32)


def _split(x):
    hi = x.astype(BF16)
    lo = (x - hi.astype(F32)).astype(BF16)
    return hi, lo


def _dot_hl(a, b_exact):
    hi, lo = _split(a)
    return (jnp.dot(hi, b_exact, preferred_element_type=F32)
            + jnp.dot(lo, b_exact, preferred_element_type=F32))


def _dot3(a, b, nt=False):
    ah, al = _split(a)
    bh, bl = _split(b)
    if nt:
        d = lambda x, y: lax.dot_general(x, y, (((1,), (1,)), ((), ())), preferred_element_type=F32)
    else:
        d = lambda x, y: jnp.dot(x, y, preferred_element_type=F32)
    return d(ah, bh) + d(al, bh) + d(ah, bl)


LOG2E = 1.4426950408889634


def _neg_abs(x):
    bits = pltpu.bitcast(x, jnp.uint32) | jnp.uint32(0x80000000)
    return pltpu.bitcast(bits, F32)


def _softplus(u):
    return jnp.maximum(u, 0.0) + jnp.log1p(jnp.exp(-jnp.abs(u)))


def _layer_norm(y, g, b):
    mu = jnp.mean(y, axis=-1, keepdims=True)
    yc = y - mu
    var = jnp.mean(yc * yc, axis=-1, keepdims=True)
    return yc * lax.rsqrt(var + LN_EPS) * g + b


def _full_spec(a, grid_rank):
    nd = a.ndim
    if grid_rank == 1:
        return pl.BlockSpec(a.shape, lambda i: (0,) * nd)
    if grid_rank == 2:
        return pl.BlockSpec(a.shape, lambda i, j: (0,) * nd)
    return pl.BlockSpec(a.shape, lambda i, j, k: (0,) * nd)


def _ffn_ln_kernel(x_ref, wg_ref, wu_ref, wo_ref, g_ref, b_ref, o_ref, acc_ref, *, alpha, n_ff):
    j = pl.program_id(1)

    @pl.when(j == 0)
    def _():
        acc_ref[...] = jnp.zeros_like(acc_ref)

    xb = x_ref[...].astype(BF16)
    tf = wg_ref.shape[1]
    cuts = list(range(0, tf, FFN_SUB)) + [tf]
    cols = [slice(a, b) for a, b in zip(cuts[:-1], cuts[1:])]
    gu = [(jnp.dot(xb, wg_ref[:, c], preferred_element_type=F32),
           jnp.dot(xb, wu_ref[:, c], preferred_element_type=F32)) for c in cols]
    mids = [(gate * jax.nn.sigmoid(gate) * up).astype(BF16) for gate, up in gu]
    out = jnp.dot(mids[0], wo_ref[cols[0], :], preferred_element_type=F32)
    for mid, c in zip(mids[1:], cols[1:]):
        out = out + jnp.dot(mid, wo_ref[c, :], preferred_element_type=F32)
    acc_ref[...] += out

    @pl.when(j == n_ff - 1)
    def _():
        y = alpha * x_ref[...] + 0.5 * acc_ref[...]
        o_ref[...] = _layer_norm(y, g_ref[...], b_ref[...])


def _ffn_tile(d_ff):
    for n in (2, 1, 11, 22):
        if d_ff % n == 0 and (d_ff // n) % 128 == 0:
            return d_ff // n
    return d_ff


def ffn_ln(x, w_in_bf, w_out_bf, g, b, alpha):
    n, d = x.shape
    d_ff = w_out_bf.shape[0]
    tm = min(FFN_ROWS, n)
    tf = _ffn_tile(d_ff)
    n_ff = d_ff // tf
    return pl.pallas_call(
        functools.partial(_ffn_ln_kernel, alpha=alpha, n_ff=n_ff),
        grid=(n // tm, n_ff),
        in_specs=[
            pl.BlockSpec((tm, d), lambda i, j: (i, 0)),
            pl.BlockSpec((d, tf), lambda i, j: (0, j)),
            pl.BlockSpec((d, tf), lambda i, j: (0, j + n_ff)),
            pl.BlockSpec((tf, d), lambda i, j: (j, 0)),
            pl.BlockSpec((1, d), lambda i, j: (0, 0)),
            pl.BlockSpec((1, d), lambda i, j: (0, 0)),
        ],
        out_specs=pl.BlockSpec((tm, d), lambda i, j: (i, 0)),
        out_shape=jax.ShapeDtypeStruct((n, d), F32),
        scratch_shapes=[pltpu.VMEM((tm, d), F32)],
        compiler_params=pltpu.CompilerParams(
            dimension_semantics=("arbitrary", "arbitrary"), vmem_limit_bytes=VMEM_LIMIT),
        name="ffn_ln",
    )(x, w_in_bf, w_in_bf, w_out_bf, g.reshape(1, d), b.reshape(1, d))


N_PREP_PARAMS = 15
N_PREP_OUTS = 12


def _rwkv_prep_math(h, h_prev, p, p_prev, params, outs, head_major):
    (wrkv_ref, ww1_ref, ww2_ref, aw1_ref, aw2_ref, gw1_ref, gw2_ref,
     mu_rkv_ref, mu_wag_ref, w0_ref, a0_ref, kk_ref, ka_ref, e_ref, wsb_ref) = params
    r_out, lw_out, k_out, a_out, b_out, v_out, g_out, q_sb, k_sb, v_sb, kb_sb, vb_sb = outs
    d_r = w0_ref.shape[-1]
    d_s = kb_sb.shape[-1]
    psb = jnp.dot(h.astype(BF16), wsb_ref[...], preferred_element_type=F32)
    q_sb[...] = (psb[:, :d_s] * SB_SCALE).astype(BF16)
    ksb = psb[:, d_s:2 * d_s]
    vsb = psb[:, 2 * d_s:]
    if head_major:
        k_sb[...] = ksb.T.reshape(k_sb.shape)
        v_sb[...] = vsb.T.reshape(v_sb.shape)
    else:
        k_sb[...] = ksb
        v_sb[...] = vsb
    kb_sb[...] = ksb.astype(BF16)
    vb_sb[...] = vsb.astype(BF16)
    rkv = p + (p_prev - p) * mu_rkv_ref[...]
    r = rkv[:, :d_r]
    k = rkv[:, d_r:2 * d_r]
    v = rkv[:, 2 * d_r:]
    dx = h_prev - h
    xw = h + dx * mu_wag_ref[0:1, :]
    xa = h + dx * mu_wag_ref[1:2, :]
    xg = h + dx * mu_wag_ref[2:3, :]
    lw = _dot(jnp.tanh(_dot(xw, ww1_ref[...])), ww2_ref[...])
    w_log = -_softplus(-(w0_ref[...] + lw)) - 0.5
    a_gate = jax.nn.sigmoid(a0_ref[...] + _dot(_dot(xa, aw1_ref[...]), aw2_ref[...]))
    g = _dot(jax.nn.sigmoid(_dot(xg, gw1_ref[...])), gw2_ref[...])
    kk = k * kk_ref[...]
    ss = _dot_hl(kk * kk, e_ref[...])
    kk = kk / jnp.maximum(jnp.sqrt(ss), 1e-12)
    r_out[...] = r
    lw_out[...] = -jnp.exp(w_log)
    k_out[...] = k * (1.0 + (a_gate - 1.0) * ka_ref[...])
    a_out[...] = -kk
    b_out[...] = kk * a_gate
    v_out[...] = v
    g_out[...] = g


def _rwkv_prep_prompt_kernel(h_ref, *refs, tiles_per_seq):
    params, outs = refs[:N_PREP_PARAMS], refs[N_PREP_PARAMS:N_PREP_PARAMS + N_PREP_OUTS]
    hcarry_ref, pcarry_ref = refs[-2:]
    i = pl.program_id(0)
    h = h_ref[...]
    tm = h.shape[0]

    @pl.when(i % tiles_per_seq == 0)
    def _():
        hcarry_ref[...] = jnp.zeros_like(hcarry_ref)
        pcarry_ref[...] = jnp.zeros_like(pcarry_ref)

    row = lax.broadcasted_iota(jnp.int32, (tm, 1), 0)
    p = jnp.dot(h.astype(BF16), params[0][...], preferred_element_type=F32)
    h_prev = jnp.where(row == 0, hcarry_ref[7:8, :], pltpu.roll(h, 1, 0))
    p_prev = jnp.where(row == 0, pcarry_ref[7:8, :], pltpu.roll(p, 1, 0))
    _rwkv_prep_math(h, h_prev, p, p_prev, params, outs, True)
    hcarry_ref[...] = h[tm - 8:, :]
    pcarry_ref[...] = p[tm - 8:, :]


def _rwkv_prep_sample_kernel(h_ref, hl_ref, *refs, seq):
    params, outs = refs[:N_PREP_PARAMS], refs[N_PREP_PARAMS:N_PREP_PARAMS + N_PREP_OUTS]
    h = h_ref[...]
    tm = h.shape[0]
    row = lax.broadcasted_iota(jnp.int32, (tm, 1), 0)
    first = row % seq == 0
    p = jnp.dot(h.astype(BF16), params[0][...], preferred_element_type=F32)
    p_last = jnp.dot(hl_ref[...].astype(BF16), params[0][...], preferred_element_type=F32)
    h_prev = jnp.where(first, hl_ref[...], pltpu.roll(h, 1, 0))
    p_prev = jnp.where(first, p_last, pltpu.roll(p, 1, 0))
    _rwkv_prep_math(h, h_prev, p, p_prev, params, outs, False)


def mixer_prep(h, h_last_rows, batch, seq, prm):
    n, d = h.shape
    d_r = prm["w0"].shape[-1]
    d_s = prm["w_sb"].shape[1] // 3
    n_h = d_s // HEAD_DIM
    plist = [prm["w_rkv"], prm["w_w1"], prm["w_w2"], prm["a_w1"], prm["a_w2"], prm["g_w1"], prm["g_w2"],
             prm["mu_rkv"], prm["mu_wag"], prm["w0"], prm["a0"], prm["k_k"], prm["k_a"], prm["e_head"],
             prm["w_sb"]]
    assert len(plist) == N_PREP_PARAMS
    if h_last_rows is None:
        tm = min(ROW_TILE, seq)
        kern = functools.partial(_rwkv_prep_prompt_kernel, tiles_per_seq=seq // tm)
        args = [h] + plist
        in_specs = [pl.BlockSpec((tm, d), lambda i: (i, 0))] + [_full_spec(a, 1) for a in plist]
        scratch = [pltpu.VMEM((8, d), F32), pltpu.VMEM((8, 3 * d_r), F32)]
    else:
        tm = n
        kern = functools.partial(_rwkv_prep_sample_kernel, seq=seq)
        args = [h, h_last_rows] + plist
        in_specs = [pl.BlockSpec((tm, d), lambda i: (i, 0))] * 2 + [_full_spec(a, 1) for a in plist]
        scratch = []
    row_spec = lambda w: pl.BlockSpec((tm, w), lambda i: (i, 0))
    if h_last_rows is None:
        tps = seq // tm
        kv_spec = pl.BlockSpec((None, n_h, HEAD_DIM, tm), lambda i: (i // tps, 0, 0, i % tps))
        kv_shape = jax.ShapeDtypeStruct((batch, n_h, HEAD_DIM, seq), F32)
    else:
        kv_spec = row_spec(d_s)
        kv_shape = jax.ShapeDtypeStruct((n, d_s), F32)
    return pl.pallas_call(
        kern,
        grid=(n // tm,),
        in_specs=in_specs,
        out_specs=[row_spec(d_r)] * 7 + [row_spec(d_s), kv_spec, kv_spec, row_spec(d_s), row_spec(d_s)],
        out_shape=[jax.ShapeDtypeStruct((n, d_r), F32)] * 7
                  + [jax.ShapeDtypeStruct((n, d_s), BF16), kv_shape, kv_shape,
                     jax.ShapeDtypeStruct((n, d_s), BF16), jax.ShapeDtypeStruct((n, d_s), BF16)],
        scratch_shapes=scratch,
        compiler_params=pltpu.CompilerParams(
            dimension_semantics=("arbitrary",), vmem_limit_bytes=VMEM_LIMIT),
        name="mixer_prep",
    )(*args)


def _scan_group(n_chunks):
    for g in (SCAN_GROUP, 2, 1):
        if n_chunks % g == 0:
            return g


def _stack_heads(x, m0):
    return jnp.concatenate([jnp.where(m0, x, 0.0), jnp.where(m0, 0.0, x)], axis=0)


def _rwkv_scan_kernel(r_ref, lw_ref, k_ref, a_ref, b_ref, v_ref, g_ref, s0_ref,
                      rk_ref, lg_ref, lb_ref, tri_ref, emean_ref, eones_ref,
                      o_ref, sfin_ref, s_ref, ta_ref, tv_ref, mr_ref, cl_ref, *, n_chunks, n_pairs, n_seq):
    i = pl.program_id(1)
    c2 = 2 * CHUNK
    seq_chunks = n_chunks // n_seq

    @pl.when(i == 0)
    def _():
        zero = jnp.zeros((HEAD_DIM, HEAD_DIM), F32)
        for q in range(n_seq):
            for p in range(n_pairs):
                top = jnp.concatenate([s0_ref[q, 2 * p], zero], axis=1)
                bot = jnp.concatenate([zero, s0_ref[q, 2 * p + 1]], axis=1)
                s_ref[q, p] = jnp.concatenate([top, bot], axis=0)

    lane = lax.broadcasted_iota(jnp.int32, (1, PAIR), 1)
    m0 = lane < HEAD_DIM
    rr = lax.broadcasted_iota(jnp.int32, (c2, c2), 0)
    cc = lax.broadcasted_iota(jnp.int32, (c2, c2), 1)
    strict = cc < rr
    incl = cc <= rr
    eye = (cc == rr).astype(F32)
    tri = tri_ref[...]
    mm = lambda x, y: jnp.dot(x, y, preferred_element_type=F32)
    mm_nt = lambda x, y: lax.dot_general(x, y, (((1,), (1,)), ((), ())), preferred_element_type=F32)

    emean = emean_ref[...]
    eones = eones_ref[...]
    r_k = rk_ref[...]
    lnx_g = lg_ref[...]
    lnx_b = lb_ref[...]

    pairs = range(n_pairs)
    cat0 = lambda x, y: jnp.concatenate([x, y], axis=0)
    stack = lambda x: _stack_heads(x, m0).astype(BF16)
    in_refs = (r_ref, lw_ref, k_ref, a_ref, b_ref, v_ref, g_ref)

    def load_rows(c):
        rows = pl.ds(pl.multiple_of(c * CHUNK, CHUNK), CHUNK)
        return rows, [ref[rows, :] for ref in in_refs]

    def pre_prep(ins, p):
        lanes = slice(p * PAIR, (p + 1) * PAIR)
        r, lw, k, a, b, v, g = (x[:, lanes] for x in ins)
        hi, lo = _split(lw)
        lo2 = (lw - hi.astype(F32) - lo.astype(F32)).astype(BF16)
        cl = mm(tri, hi) + mm(tri, lo) + mm(tri, lo2)
        w_inv = jnp.exp(-cl)
        return dict(cl=cl, vst=stack(v), ast=stack(a * jnp.exp(cl - lw)), rst=stack(r * jnp.exp(cl)),
                    bst=stack(b * w_inv), kst=stack(k * w_inv))

    def state_prep(ins, cl, p):
        lanes = slice(p * PAIR, (p + 1) * PAIR)
        r, lw, k, a, b, v, g = (x[:, lanes] for x in ins)
        cl_end = cl[CHUNK - 1:CHUNK, :]
        w_tail = jnp.exp(cl_end - cl)
        vst_f = _stack_heads(v, m0)
        return dict(r=r, k=k, v=v, g=g, lanes=lanes, w_end=jnp.exp(cl_end), vst_f=vst_f, vst=vst_f.astype(BF16),
                    rst=stack(r * jnp.exp(cl)), btl=stack(b * w_tail), ktl=stack(k * w_tail))

    grp = _scan_group(n_chunks)
    n_grp = n_chunks // grp
    assert n_seq == 1 or n_grp == 1

    def step(g_state, g_pre, slot):
        todo = []
        if g_state is not None:
            seq_of = [0 if n_seq == 1 else j // seq_chunks for j in range(grp)]
            s_all = [[s_ref[q, p] for p in pairs] for q in range(n_seq)]
            done = []
            for j in range(grp):
                s_cur = s_all[seq_of[j]]

                def first(j=j, s_cur=s_cur):
                    rows, ins = load_rows(g_state * grp + j)
                    s = list(s_cur)
                    sb = [x.astype(BF16) for x in s]
                    u = [mm_nt(ta_ref[slot, j, p], sb[p]) + tv_ref[slot, j, p] for p in pairs]
                    return rows, ins, s, sb, u

                def second(st, j=j, s_cur=s_cur):
                    rows, ins, s, sb, u = st
                    e = [state_prep(ins, cl_ref[slot, j, p], p) for p in pairs]
                    uv_t = [cat0(u[p], e[p]["vst_f"]).T.astype(BF16) for p in pairs]
                    for p in pairs:
                        s_cur[p] = s[p] * e[p]["w_end"] + mm(uv_t[p], cat0(e[p]["btl"], e[p]["ktl"]))
                    y = [mm_nt(e[p]["rst"], sb[p])
                         + mm(mr_ref[slot, j, p], cat0(u[p].astype(BF16), e[p]["vst"])) for p in pairs]
                    done.append((rows, e, y))
                todo.append((first, second))

        def run_state_piece(state):
            idx, pending = state
            if idx >= len(todo):
                return state
            if pending is None:
                return idx, todo[idx][0]()
            todo[idx][1](pending)
            return idx + 1, None

        cursor = (0, None)
        if g_pre is not None:
            chains = [(j, p) for j in range(grp) for p in pairs]
            ins_p = [load_rows(g_pre * grp + j)[1] for j in range(grp)]
            cursor = run_state_piece(cursor)
            d = [pre_prep(ins_p[j], p) for j, p in chains]
            m4 = [mm_nt(cat0(x["ast"], x["rst"]), cat0(x["bst"], x["kst"])) for x in d]
            cursor = run_state_piece(cursor)
            nab = [jnp.where(strict, m[:c2, :c2], 0.0) for m in m4]
            mak = [jnp.where(strict, m[:c2, c2:], 0.0).astype(BF16) for m in m4]
            mrbk = [jnp.concatenate([jnp.where(incl, m[c2:, :c2], 0.0), jnp.where(incl, m[c2:, c2:], 0.0)],
                                    axis=1).astype(BF16) for m in m4]
            t = [eye + n for n in nab]
            pw = [n.astype(BF16) for n in nab]
            for _ in range(5):
                pw = [mm(x, x).astype(BF16) for x in pw]
                cursor = run_state_piece(cursor)
                t = [tt + mm(x, tt.astype(BF16)) for x, tt in zip(pw, t)]
                cursor = run_state_piece(cursor)
            mv = [mm(mk, x["vst"]) for mk, x in zip(mak, d)]
            tx = [mm(tt.astype(BF16), jnp.concatenate([x["ast"], m.astype(BF16)], axis=1))
                  for tt, x, m in zip(t, d, mv)]
        while cursor[0] < len(todo):
            cursor = run_state_piece(cursor)
        if g_state is not None:
            for rows, e, y in done:
                yp = [x[:CHUNK, :] + x[CHUNK:, :] for x in y]
                mu = [_dot_hl(x, emean) for x in yp]
                yc = [x - m for x, m in zip(yp, mu)]
                var = [_dot_hl(x * x, emean) for x in yc]
                bonus = [_dot_hl(x["r"] * x["k"] * r_k[:, x["lanes"]], eones) * x["v"] for x in e]
                outs = [(yc[p] * lax.rsqrt(var[p] + LNX_EPS) * lnx_g[:, e[p]["lanes"]] + lnx_b[:, e[p]["lanes"]]
                         + bonus[p]) * e[p]["g"] for p in pairs]
                o_ref[rows, :] = jnp.concatenate(outs, axis=1)
            for q in range(n_seq):
                for p in pairs:
                    s_ref[q, p] = s_all[q][p]
        if g_pre is not None:
            for (j, p), x, txx, mr in zip(chains, d, tx, mrbk):
                ta_ref[1 - slot, j, p] = txx[:, :PAIR].astype(BF16)
                tv_ref[1 - slot, j, p] = txx[:, PAIR:]
                mr_ref[1 - slot, j, p] = mr
                cl_ref[1 - slot, j, p] = x["cl"]

    step(None, 0, 1)

    def body(gi, carry):
        step(gi, gi + 1, lax.rem(gi, 2))
        return carry

    lax.fori_loop(0, n_grp - 1, body, 0)
    step(n_grp - 1, None, (n_grp - 1) % 2)
    for q in range(n_seq):
        for p in range(n_pairs):
            sfin_ref[q, 2 * p] = s_ref[q, p, :HEAD_DIM, :HEAD_DIM]
            sfin_ref[q, 2 * p + 1] = s_ref[q, p, HEAD_DIM:, HEAD_DIM:]


def rwkv_scan(r, lw, k, a, b, v, g, s0, r_k, lnx_g, lnx_b, batch, seq):
    n, d_r = r.shape
    n_pairs = d_r // PAIR
    if seq > CHUNK:
        n_seq, rows = 1, min(SCAN_ROWS, seq)
    else:
        n_seq = next(q for q in (SCAN_GROUP, 2, 1) if batch % q == 0)
        rows = n_seq * seq
    steps = max(seq // rows, 1)
    grp = _scan_group(rows // CHUNK)
    tri = jnp.tril(jnp.ones((CHUNK, CHUNK), F32)).astype(BF16)
    blk = jnp.arange(PAIR) // HEAD_DIM
    same = (blk[:, None] == blk[None, :]).astype(F32)
    emean = (same / HEAD_DIM).astype(BF16)
    eones = same.astype(BF16)
    tok = pl.BlockSpec((rows, d_r), lambda bi, i: (bi * steps + i, 0))
    par = pl.BlockSpec((1, d_r), lambda bi, i: (0, 0))
    st = pl.BlockSpec((n_seq, 2 * n_pairs, HEAD_DIM, HEAD_DIM), lambda bi, i: (bi, 0, 0, 0))
    return pl.pallas_call(
        functools.partial(_rwkv_scan_kernel, n_chunks=rows // CHUNK, n_pairs=n_pairs, n_seq=n_seq),
        grid=(batch // n_seq, steps),
        in_specs=[tok] * 7 + [st, par, par, par, _full_spec(tri, 2), _full_spec(emean, 2), _full_spec(eones, 2)],
        out_specs=[tok, st],
        out_shape=[jax.ShapeDtypeStruct((n, d_r), F32),
                   jax.ShapeDtypeStruct((batch, 2 * n_pairs, HEAD_DIM, HEAD_DIM), F32)],
        scratch_shapes=[pltpu.VMEM((n_seq, n_pairs, PAIR, PAIR), F32),
                        pltpu.VMEM((2, grp, n_pairs, PAIR, PAIR), BF16),
                        pltpu.VMEM((2, grp, n_pairs, PAIR, PAIR), F32),
                        pltpu.VMEM((2, grp, n_pairs, PAIR, 2 * PAIR), BF16),
                        pltpu.VMEM((2, grp, n_pairs, CHUNK, PAIR), F32)],
        compiler_params=pltpu.CompilerParams(
            dimension_semantics=("arbitrary", "arbitrary"), vmem_limit_bytes=VMEM_LIMIT),
        name="rwkv_scan",
    )(r, lw, k, a, b, v, g, s0, r_k.reshape(1, d_r), lnx_g.reshape(1, d_r), lnx_b.reshape(1, d_r),
      tri, emean, eones)


def _sb_prompt_kernel(bias_ref, q_ref, k_ref, v_ref, u_ref, o_ref, acc_ref, c_ref, *, tq, tk, n_hp):
    g = pl.program_id(1)
    i = pl.program_id(2)
    n_sub = tq // tk
    n_heads = 2 * n_hp
    lane = lax.broadcasted_iota(jnp.int32, (1, PAIR), 1)
    m0 = lane < HEAD_DIM
    pair_lanes = [slice((h // 2) * PAIR, (h // 2 + 1) * PAIR) for h in range(n_heads)]
    q_heads = []
    for h in range(n_heads):
        qp = q_ref[:, pair_lanes[h]]
        qh = jnp.where(m0 if h % 2 == 0 else ~m0, qp, jnp.zeros_like(qp))
        bias = jnp.full((1, PAIR), bias_ref[n_heads * g + h], F32)
        b1 = bias.astype(BF16).astype(F32)
        b2 = (bias - b1).astype(BF16).astype(F32)
        b3 = (bias - b1) - b2
        ext = jnp.where(lane == 0, b1, jnp.where(lane == 1, b2, jnp.where(lane == 2, b3, 0.0)))
        q_heads.append(jnp.concatenate([qh, jnp.broadcast_to(ext, qh.shape).astype(BF16)], axis=1))
    k_ext = {nk: jnp.broadcast_to(jnp.where(lane < 3, 1.0, 0.0), (nk, PAIR)).astype(BF16)
             for nk in range(tk, tq + 1, tk)}
    acc_ref[...] = jnp.zeros_like(acc_ref)
    c_ref[...] = jnp.zeros_like(c_ref)
    u = u_ref[...]
    rr = lax.broadcasted_iota(jnp.int32, (tk, tk), 0)
    cc = lax.broadcasted_iota(jnp.int32, (tk, tk), 1)
    earlier = cc < rr
    tile = lambda x: jnp.concatenate([x] * (tk // PAIR), axis=1)

    def block(kb, q0, nq_rows, n_keys_sub, mask_last):
        qrows = slice(q0, q0 + nq_rows)
        nk = n_keys_sub * tk
        rows = pl.ds(pl.multiple_of(kb * tq, tq), nk)
        subs = [slice(j * tk, (j + 1) * tk) for j in range(n_keys_sub)]
        ks = [jnp.concatenate([k_ref[rows, pair_lanes[h]], k_ext[nk]], axis=1) for h in range(n_heads)]
        vs = [v_ref[rows, pair_lanes[h]] for h in range(n_heads)]
        zs = [lax.dot_general(qh[qrows], kh, (((1,), (1,)), ((), ())), preferred_element_type=F32)
              for qh, kh in zip(q_heads, ks)]
        ns = [jnp.maximum(z, 0.0) + jnp.log(1.0 + jnp.exp(_neg_abs(z))) for z in zs]
        masks = [earlier if (mask_last and j == n_keys_sub - 1) else None for j in range(n_keys_sub)]
        nm = [[n[:, s] if m is None else jnp.where(m, n[:, s], 0.0) for s, m in zip(subs, masks)] for n in ns]
        cums = [[jnp.dot(x.astype(BF16), u, preferred_element_type=F32) for x in nh] for nh in nm]
        tots = [[jnp.broadcast_to(cum[:, :1] + x[:, :1], (nq_rows, PAIR)) for cum, x in zip(cm, nh)]
                for cm, nh in zip(cums, nm)]
        ws = []
        for h in range(n_heads):
            run = c_ref[h, qrows]
            parts = [None] * n_keys_sub
            for j in range(n_keys_sub - 1, -1, -1):
                parts[j] = cums[h][j] + tile(run)
                run = run + tots[h][j]
            c_ref[h, qrows] = run
            w = jnp.exp((zs[h] - ns[h]) - jnp.concatenate(parts, axis=1))
            if mask_last:
                w = jnp.concatenate([w[:, s] if m is None else jnp.where(m, w[:, s], 0.0)
                                     for s, m in zip(subs, masks)], axis=1)
            ws.append(w)
        for h in range(n_heads):
            acc_ref[h, qrows] += jnp.dot(ws[h].astype(BF16), vs[h], preferred_element_type=F32)

    for r in range(n_sub):
        block(i, r * tk, tk, r + 1, True)

    def body(it, carry):
        block(i - 1 - it, 0, tq, n_sub, False)
        return carry

    lax.fori_loop(0, i, body, 0)
    o_ref[...] = jnp.concatenate(
        [jnp.where(m0, acc_ref[2 * hp], acc_ref[2 * hp + 1]) for hp in range(n_hp)], axis=1)


def sb_prompt(q_bf, k_bf, v_bf, bias, batch, seq):
    n, d_s = q_bf.shape
    n_hp = SB_PAIRS
    width = n_hp * PAIR
    n_grp = d_s // width
    tq = min(SB_TILE, seq)
    tk = min(SB_KEYS, seq)
    nq = seq // tq
    j = jnp.arange(tk)
    ux = (j[:, None] > j[None, :]).astype(BF16)
    grid_spec = pltpu.PrefetchScalarGridSpec(
        num_scalar_prefetch=1,
        grid=(batch, n_grp, nq),
        in_specs=[
            pl.BlockSpec((tq, width), lambda b, p, i, bias_ref: (b * nq + i, p)),
            pl.BlockSpec((seq, width), lambda b, p, i, bias_ref: (b, p)),
            pl.BlockSpec((seq, width), lambda b, p, i, bias_ref: (b, p)),
            pl.BlockSpec(ux.shape, lambda b, p, i, bias_ref: (0, 0)),
        ],
        out_specs=pl.BlockSpec((tq, width), lambda b, p, i, bias_ref: (b * nq + i, p)),
        scratch_shapes=[pltpu.VMEM((2 * n_hp, tq, PAIR), F32)] * 2,
    )
    return pl.pallas_call(
        functools.partial(_sb_prompt_kernel, tq=tq, tk=tk, n_hp=n_hp),
        grid_spec=grid_spec,
        out_shape=jax.ShapeDtypeStruct((n, d_s), F32),
        compiler_params=pltpu.CompilerParams(
            dimension_semantics=("arbitrary", "arbitrary", "arbitrary"), vmem_limit_bytes=VMEM_LIMIT),
        name="sb_prompt",
    )(bias, q_bf, k_bf, v_bf, ux)


def _sb_sample_kernel(pt_ref, q_ref, brow_ref, kn_ref, vn_ref, *refs, n_pg, n_heads):
    kps, vps = refs[:n_pg], refs[n_pg:2 * n_pg]
    ux_ref, o_ref, acc_ref, c_ref = refs[2 * n_pg:]
    j = pl.program_id(1)
    qbd = q_ref[...]
    nrow = qbd.shape[0]
    brow = brow_ref[...]
    ux = ux_ref[...]

    def log_terms(z, valid):
        nb = z.shape[1] // PAIR
        z = z + jnp.concatenate([brow] * nb, axis=1)
        sp = _softplus(z)
        l = -sp
        if valid is not None:
            l = jnp.where(valid, l, 0.0)
        lst = jnp.concatenate([l[:, bi * PAIR:(bi + 1) * PAIR] for bi in range(nb)], axis=0)
        return z - sp, _dot_hl(lst, ux)

    def weights(la, cx, run, valid):
        nb = la.shape[1] // PAIR
        parts = [None] * nb
        for bi in range(nb - 1, -1, -1):
            blk = cx[bi * nrow:(bi + 1) * nrow, :]
            parts[bi] = blk[:, :PAIR] + run
            run = run + blk[:, PAIR:]
        w = jnp.exp(la + jnp.concatenate(parts, axis=1))
        if valid is not None:
            w = jnp.where(valid, w, 0.0)
        return w.astype(BF16), run

    @pl.when(j == 0)
    def _():
        t_of_row = lax.broadcasted_iota(jnp.int32, (nrow, PAIR), 0) // n_heads
        key = lax.broadcasted_iota(jnp.int32, (nrow, PAIR), 1)
        valid = key < t_of_row
        la, cx = log_terms(_dot_nt(qbd, kn_ref[...]), valid)
        w, run = weights(la, cx, jnp.zeros((nrow, PAIR), F32), valid)
        acc_ref[...] = _dot(w, vn_ref[...])
        c_ref[...] = run

    d_s = qbd.shape[1]
    page_t = lambda r: r[...].reshape(d_s, PAIR).astype(BF16)
    n_grp = 2 if n_pg % 2 == 0 else 1
    per = n_pg // n_grp
    groups = [range(gi * per, (gi + 1) * per) for gi in range(n_grp)]
    zs = [jnp.concatenate([_dot(qbd, page_t(kps[pi])) for pi in grp], axis=1) for grp in groups]
    terms = [log_terms(z, None) for z in zs]
    run = c_ref[...]
    ws = [None] * n_grp
    for gi in range(n_grp - 1, -1, -1):
        ws[gi], run = weights(terms[gi][0], terms[gi][1], run, None)
    c_ref[...] = run
    out = None
    for grp, w in zip(groups, ws):
        for li, pi in enumerate(grp):
            o = _dot_nt(w[:, li * PAIR:(li + 1) * PAIR], page_t(vps[pi]))
            out = o if out is None else out + o
    acc_ref[...] += out

    @pl.when(j == pl.num_programs(1) - 1)
    def _():
        d_s = acc_ref.shape[1]
        head_of_row = lax.broadcasted_iota(jnp.int32, (nrow, d_s), 0) % n_heads
        head_of_lane = lax.broadcasted_iota(jnp.int32, (nrow, d_s), 1) // HEAD_DIM
        sel = jnp.where(head_of_row == head_of_lane, acc_ref[...], 0.0)
        o_ref[...] = jnp.sum(sel.reshape(nrow // n_heads, n_heads, d_s), axis=1)


def sb_sample(q_bd, brow, k_new, v_new, cache_k, cache_v, page_table, n_heads):
    bsz, nrow, d_s = q_bd.shape
    n_pages = page_table.shape[1]
    page = cache_k.shape[3]
    assert page == PAIR
    n_pg = min(SB_PAGES, n_pages)
    steps = n_pages // n_pg
    jj = jnp.arange(PAIR)
    ux = jnp.concatenate([(jj[:, None] > jj[None, :]).astype(F32), jnp.ones((PAIR, PAIR), F32)], axis=1).astype(BF16)

    def page_spec(i):
        return pl.BlockSpec((None, n_heads, HEAD_DIM, page),
                            lambda b, j, pt: (pt[b, n_pages - n_pg * (j + 1) + i], 0, 0, 0))

    per_b = lambda shape: pl.BlockSpec((None,) + shape, lambda b, j, pt: (b, 0, 0))
    grid_spec = pltpu.PrefetchScalarGridSpec(
        num_scalar_prefetch=1,
        grid=(bsz, steps),
        in_specs=[per_b((nrow, d_s)), pl.BlockSpec(brow.shape, lambda b, j, pt: (0, 0)),
                  per_b((PAIR, d_s)), per_b((PAIR, d_s))]
                 + [page_spec(i) for i in range(n_pg)] * 2
                 + [pl.BlockSpec(ux.shape, lambda b, j, pt: (0, 0))],
        out_specs=per_b((nrow // n_heads, d_s)),
        scratch_shapes=[pltpu.VMEM((nrow, d_s), F32), pltpu.VMEM((nrow, PAIR), F32)],
    )
    return pl.pallas_call(
        functools.partial(_sb_sample_kernel, n_pg=n_pg, n_heads=n_heads),
        grid_spec=grid_spec,
        out_shape=jax.ShapeDtypeStruct((bsz, nrow // n_heads, d_s), F32),
        compiler_params=pltpu.CompilerParams(
            dimension_semantics=("arbitrary", "arbitrary"), vmem_limit_bytes=VMEM_LIMIT),
        name="sb_sample",
    )(page_table, q_bd, brow, k_new, v_new, *([cache_k] * n_pg), *([cache_v] * n_pg), ux)


def _merge_kernel(h_ref, oa_ref, ob_ref, wg_ref, wba_ref, wbb_ref, wo_ref, g_ref, b_ref, o_ref, *, alpha):
    h = h_ref[...]
    d = h.shape[1]
    gates = jax.nn.sigmoid(jnp.dot(h.astype(BF16), wg_ref[...], preferred_element_type=F32))
    ma = jnp.dot(oa_ref[...].astype(BF16), wba_ref[...], preferred_element_type=F32)
    mb = jnp.dot(ob_ref[...].astype(BF16), wbb_ref[...], preferred_element_type=F32)
    merged = gates[:, :d] * ma + gates[:, d:] * mb
    y = alpha * h + jnp.dot(merged.astype(BF16), wo_ref[...], preferred_element_type=F32)
    o_ref[...] = _layer_norm(y, g_ref[...], b_ref[...])


def merge(h, o_a, o_b, w_gate_bf, w_ba_bf, w_bb_bf, w_out_bf, g, b, alpha):
    n, d = h.shape
    tm = min(ROW_TILE, n)
    row = lambda a: pl.BlockSpec((tm, a.shape[1]), lambda i: (i, 0))
    ws = [w_gate_bf, w_ba_bf, w_bb_bf, w_out_bf, g.reshape(1, d), b.reshape(1, d)]
    return pl.pallas_call(
        functools.partial(_merge_kernel, alpha=alpha),
        grid=(n // tm,),
        in_specs=[row(h), row(o_a), row(o_b)] + [_full_spec(a, 1) for a in ws],
        out_specs=pl.BlockSpec((tm, d), lambda i: (i, 0)),
        out_shape=jax.ShapeDtypeStruct((n, d), F32),
        compiler_params=pltpu.CompilerParams(
            dimension_semantics=("arbitrary",), vmem_limit_bytes=VMEM_LIMIT),
        name="merge",
    )(h, o_a, o_b, *ws)


def _layer(x, batch, seq, h_last, wkv0, sb_fn, w, alpha):
    n, d = x.shape
    h = ffn_ln(x, w["ffn1_in"], w["ffn1_out"], w["ln1_g"], w["ln1_b"], alpha)
    if h_last is None:
        hl_rows = None
        rows_pad = seq
    else:
        hl_rows = jnp.repeat(h_last, seq, axis=0)
        rows_pad = CHUNK
    r, lw, k, a, b, v, g, q_bf, k_sb, v_sb, k_bf, v_bf = mixer_prep(h, hl_rows, batch, seq, w)
    n_h = w["w0"].shape[-1] // HEAD_DIM
    if wkv0 is None:
        s0 = jnp.zeros((batch, n_h, HEAD_DIM, HEAD_DIM), F32)
    else:
        s0 = wkv0.astype(F32)
    scan_in = [r, lw, k, a, b, v, g]
    if rows_pad != seq:
        pad = lambda t: jnp.pad(t.reshape(batch, seq, -1), ((0, 0), (0, rows_pad - seq), (0, 0))).reshape(
            batch * rows_pad, -1)
        scan_in = [pad(t) for t in scan_in]
    o_a, wkv = rwkv_scan(*scan_in, s0, w["r_k"], w["lnx_g"], w["lnx_b"], batch, rows_pad)
    if rows_pad != seq:
        o_a = o_a.reshape(batch, rows_pad, -1)[:, :seq].reshape(n, -1)
    o_b = sb_fn(q_bf, k_sb, v_sb, k_bf, v_bf)
    x2 = merge(h, o_a, o_b, w["w_gate"], w["w_ba"], w["w_bb"], w["w_out"], w["ln2_g"], w["ln2_b"], alpha)
    x3 = ffn_ln(x2, w["ffn2_in"], w["ffn2_out"], w["ln3_g"], w["ln3_b"], alpha)
    return x3, k_sb, v_sb, wkv, h.reshape(batch, seq, d)[:, -1]


def kernel(x_prompt, x_sample, cache_k, cache_v, state_wkv, state_shift, page_table, ln1_g, ln1_b, ffn1_w_in, ffn1_w_out, w_in, mu_rkv, mu_wag, w0, w_w1, w_w2, a0, a_w1, a_w2, g_w1, g_w2, k_k, k_a, r_k, lnx_g, lnx_b, sb_bias, w_branch, w_out, ln2_g, ln2_b, ffn2_w_in, ffn2_w_out, ln3_g, ln3_b):
    depth = ln1_g.shape[0]
    alpha = (2.0 * depth) ** 0.25
    bp, tp, d = x_prompt.shape
    bs, ts, _ = x_sample.shape
    d_r = w0.shape[-1]
    d_s = (w_in.shape[-1] - 3 * d_r - 2 * d) // 3
    n_hs = d_s // HEAD_DIM
    blk = jnp.arange(d_r) // HEAD_DIM
    e_head = (blk[:, None] == blk[None, :]).astype(BF16)

    xp = x_prompt.reshape(bp * tp, d)
    xs = x_sample.reshape(bs * ts, d)
    outs = [[] for _ in range(8)]
    for l in range(depth):
        row = lambda a: a[l].reshape(1, -1)
        w = dict(
            ffn1_in=ffn1_w_in[l].astype(BF16), ffn1_out=ffn1_w_out[l].astype(BF16),
            ffn2_in=ffn2_w_in[l].astype(BF16), ffn2_out=ffn2_w_out[l].astype(BF16),
            ln1_g=ln1_g[l], ln1_b=ln1_b[l], ln2_g=ln2_g[l], ln2_b=ln2_b[l], ln3_g=ln3_g[l], ln3_b=ln3_b[l],
            w_rkv=w_in[l][:, :3 * d_r].astype(BF16),
            w_sb=w_in[l][:, 3 * d_r:3 * d_r + 3 * d_s].astype(BF16),
            w_gate=w_in[l][:, 3 * d_r + 3 * d_s:].astype(BF16),
            w_w1=w_w1[l].astype(BF16), w_w2=w_w2[l].astype(BF16), a_w1=a_w1[l].astype(BF16),
            a_w2=a_w2[l].astype(BF16), g_w1=g_w1[l].astype(BF16), g_w2=g_w2[l].astype(BF16),
            mu_rkv=row(mu_rkv), mu_wag=mu_wag[l], w0=row(w0), a0=row(a0), k_k=row(k_k), k_a=row(k_a),
            e_head=e_head, r_k=r_k[l].reshape(-1), lnx_g=lnx_g[l], lnx_b=lnx_b[l],
            w_ba=w_branch[l][:d_r].astype(BF16), w_bb=w_branch[l][d_r:].astype(BF16),
            w_out=w_out[l].astype(BF16),
        )
        bias = sb_bias[l].astype(F32)

        def prompt_sb(q_bf, k_sb, v_sb, k_bf, v_bf):
            return sb_prompt(q_bf, k_bf, v_bf, bias, bp, tp)

        def sample_sb(q_bf, k_sb, v_sb, k_bf, v_bf):
            head_of_lane = jnp.arange(d_s) // HEAD_DIM
            onehot = (jnp.arange(n_hs)[:, None] == head_of_lane[None, :])
            q_bd = jnp.where(onehot[None, None], q_bf.reshape(bs, ts, 1, d_s), jnp.zeros((), BF16))
            q_bd = q_bd.reshape(bs, ts * n_hs, d_s)
            brow = jnp.broadcast_to(jnp.tile(bias, ts)[:, None], (ts * n_hs, PAIR))
            padk = lambda t: jnp.pad(t.reshape(bs, ts, d_s), ((0, 0), (0, PAIR - ts), (0, 0)))
            ck = jnp.transpose(cache_k[l], (0, 2, 3, 1))
            cv = jnp.transpose(cache_v[l], (0, 2, 3, 1))
            o = sb_sample(q_bd, brow, padk(k_sb), padk(v_sb), ck, cv, page_table, n_hs)
            return o.reshape(bs * ts, d_s)

        xp, kp, vp, wp, hp = _layer(xp, bp, tp, None, None, prompt_sb, w, alpha)
        xs, ks_, vs_, ws_, hs_ = _layer(xs, bs, ts, state_shift[l].astype(F32), state_wkv[l], sample_sb, w, alpha)
        for lst, val in zip(outs, (jnp.transpose(kp, (0, 3, 1, 2)), jnp.transpose(vp, (0, 3, 1, 2)), wp, hp,
                                   ks_.reshape(bs, ts, n_hs, HEAD_DIM), vs_.reshape(bs, ts, n_hs, HEAD_DIM),
                                   ws_.astype(state_wkv.dtype), hs_)):
            lst.append(val)
    return (xp.reshape(bp, tp, d), xs.reshape(bs, ts, d)) + tuple(jnp.stack(o) for o in outs)
```

```python
import functools

import jax
import jax.numpy as jnp
from jax import lax
from jax.experimental import pallas as pl
from jax.experimental.pallas import tpu as pltpu

F32 = jnp.float32
BF16 = jnp.bfloat16

HEAD_DIM = 64
PAIR = 2 * HEAD_DIM
CHUNK = 64
LN_EPS = 1e-5
LNX_EPS = 64e-5
SB_SCALE = HEAD_DIM ** -0.5

ROW_TILE = 512
FFN_ROWS = 1024
FFN_SUB = 512
SCAN_ROWS = 1024
SCAN_GROUP = 4
SB_TILE = 512
SB_KEYS = 256
SB_PAIRS = 2
SB_PAGES = 16
SB_PREFETCH = 3
VMEM_LIMIT = 56 * 1024 * 1024


def _dot(a, b):
    return jnp.dot(a.astype(BF16), b.astype(BF16), preferred_element_type=F32)


def _dot_nt(a, b):
    return lax.dot_general(a.astype(BF16), b.astype(BF16), (((1,), (1,)), ((), ())),
                           preferred_element_type=F32)


def _split(x):
    hi = x.astype(BF16)
    lo = (x - hi.astype(F32)).astype(BF16)
    return hi, lo


def _dot_hl(a, b_exact):
    hi, lo = _split(a)
    return (jnp.dot(hi, b_exact, preferred_element_type=F32)
            + jnp.dot(lo, b_exact, preferred_element_type=F32))


def _dot3(a, b, nt=False):
    ah, al = _split(a)
    bh, bl = _split(b)
    if nt:
        d = lambda x, y: lax.dot_general(x, y, (((1,), (1,)), ((), ())), preferred_element_type=F32)
    else:
        d = lambda x, y: jnp.dot(x, y, preferred_element_type=F32)
    return d(ah, bh) + d(al, bh) + d(ah, bl)


LOG2E = 1.4426950408889634


def _neg_abs(x):
    bits = pltpu.bitcast(x, jnp.uint32) | jnp.uint32(0x80000000)
    return pltpu.bitcast(bits, F32)


def _softplus(u):
    return jnp.maximum(u, 0.0) + jnp.log1p(jnp.exp(-jnp.abs(u)))


def _layer_norm(y, g, b):
    mu = jnp.mean(y, axis=-1, keepdims=True)
    yc = y - mu
    var = jnp.mean(yc * yc, axis=-1, keepdims=True)
    return yc * lax.rsqrt(var + LN_EPS) * g + b


def _full_spec(a, grid_rank):
    nd = a.ndim
    if grid_rank == 1:
        return pl.BlockSpec(a.shape, lambda i: (0,) * nd)
    if grid_rank == 2:
        return pl.BlockSpec(a.shape, lambda i, j: (0,) * nd)
    return pl.BlockSpec(a.shape, lambda i, j, k: (0,) * nd)


def _ffn_ln_kernel(x_ref, wg_ref, wu_ref, wo_ref, g_ref, b_ref, o_ref, acc_ref, *, alpha, n_ff):
    j = pl.program_id(1)

    @pl.when(j == 0)
    def _():
        acc_ref[...] = jnp.zeros_like(acc_ref)

    xb = x_ref[...].astype(BF16)
    tf = wg_ref.shape[1]
    cuts = list(range(0, tf, FFN_SUB)) + [tf]
    cols = [slice(a, b) for a, b in zip(cuts[:-1], cuts[1:])]
    gu = [(jnp.dot(xb, wg_ref[:, c], preferred_element_type=F32),
           jnp.dot(xb, wu_ref[:, c], preferred_element_type=F32)) for c in cols]
    mids = [(gate * jax.nn.sigmoid(gate) * up).astype(BF16) for gate, up in gu]
    out = jnp.dot(mids[0], wo_ref[cols[0], :], preferred_element_type=F32)
    for mid, c in zip(mids[1:], cols[1:]):
        out = out + jnp.dot(mid, wo_ref[c, :], preferred_element_type=F32)
    acc_ref[...] += out

    @pl.when(j == n_ff - 1)
    def _():
        y = alpha * x_ref[...] + 0.5 * acc_ref[...]
        o_ref[...] = _layer_norm(y, g_ref[...], b_ref[...])


def _ffn_tile(d_ff):
    for n in (2, 1, 11, 22):
        if d_ff % n == 0 and (d_ff // n) % 128 == 0:
            return d_ff // n
    return d_ff


def ffn_ln(x, w_in_bf, w_out_bf, g, b, alpha):
    n, d = x.shape
    d_ff = w_out_bf.shape[0]
    tm = min(FFN_ROWS, n)
    tf = _ffn_tile(d_ff)
    n_ff = d_ff // tf
    return pl.pallas_call(
        functools.partial(_ffn_ln_kernel, alpha=alpha, n_ff=n_ff),
        grid=(n // tm, n_ff),
        in_specs=[
            pl.BlockSpec((tm, d), lambda i, j: (i, 0)),
            pl.BlockSpec((d, tf), lambda i, j: (0, j)),
            pl.BlockSpec((d, tf), lambda i, j: (0, j + n_ff)),
            pl.BlockSpec((tf, d), lambda i, j: (j, 0)),
            pl.BlockSpec((1, d), lambda i, j: (0, 0)),
            pl.BlockSpec((1, d), lambda i, j: (0, 0)),
        ],
        out_specs=pl.BlockSpec((tm, d), lambda i, j: (i, 0)),
        out_shape=jax.ShapeDtypeStruct((n, d), F32),
        scratch_shapes=[pltpu.VMEM((tm, d), F32)],
        compiler_params=pltpu.CompilerParams(
            dimension_semantics=("arbitrary", "arbitrary"), vmem_limit_bytes=VMEM_LIMIT),
        name="ffn_ln",
    )(x, w_in_bf, w_in_bf, w_out_bf, g.reshape(1, d), b.reshape(1, d))


N_PREP_PARAMS = 15
N_PREP_OUTS = 12


def _rwkv_prep_math(h, h_prev, p, p_prev, params, outs, head_major):
    (wrkv_ref, ww1_ref, ww2_ref, aw1_ref, aw2_ref, gw1_ref, gw2_ref,
     mu_rkv_ref, mu_wag_ref, w0_ref, a0_ref, kk_ref, ka_ref, e_ref, wsb_ref) = params
    r_out, lw_out, k_out, a_out, b_out, v_out, g_out, q_sb, k_sb, v_sb, kb_sb, vb_sb = outs
    d_r = w0_ref.shape[-1]
    d_s = kb_sb.shape[-1]
    psb = jnp.dot(h.astype(BF16), wsb_ref[...], preferred_element_type=F32)
    q_sb[...] = (psb[:, :d_s] * SB_SCALE).astype(BF16)
    ksb = psb[:, d_s:2 * d_s]
    vsb = psb[:, 2 * d_s:]
    if head_major:
        k_sb[...] = ksb.T.reshape(k_sb.shape)
        v_sb[...] = vsb.T.reshape(v_sb.shape)
    else:
        k_sb[...] = ksb
        v_sb[...] = vsb
    kb_sb[...] = ksb.astype(BF16)
    vb_sb[...] = vsb.astype(BF16)
    rkv = p + (p_prev - p) * mu_rkv_ref[...]
    r = rkv[:, :d_r]
    k = rkv[:, d_r:2 * d_r]
    v = rkv[:, 2 * d_r:]
    dx = h_prev - h
    xw = h + dx * mu_wag_ref[0:1, :]
    xa = h + dx * mu_wag_ref[1:2, :]
    xg = h + dx * mu_wag_ref[2:3, :]
    lw = _dot(jnp.tanh(_dot(xw, ww1_ref[...])), ww2_ref[...])
    w_log = -_softplus(-(w0_ref[...] + lw)) - 0.5
    a_gate = jax.nn.sigmoid(a0_ref[...] + _dot(_dot(xa, aw1_ref[...]), aw2_ref[...]))
    g = _dot(jax.nn.sigmoid(_dot(xg, gw1_ref[...])), gw2_ref[...])
    kk = k * kk_ref[...]
    ss = _dot_hl(kk * kk, e_ref[...])
    kk = kk / jnp.maximum(jnp.sqrt(ss), 1e-12)
    r_out[...] = r
    lw_out[...] = -jnp.exp(w_log)
    k_out[...] = k * (1.0 + (a_gate - 1.0) * ka_ref[...])
    a_out[...] = -kk
    b_out[...] = kk * a_gate
    v_out[...] = v
    g_out[...] = g


def _rwkv_prep_prompt_kernel(h_ref, *refs, tiles_per_seq):
    params, outs = refs[:N_PREP_PARAMS], refs[N_PREP_PARAMS:N_PREP_PARAMS + N_PREP_OUTS]
    hcarry_ref, pcarry_ref = refs[-2:]
    i = pl.program_id(0)
    h = h_ref[...]
    tm = h.shape[0]

    @pl.when(i % tiles_per_seq == 0)
    def _():
        hcarry_ref[...] = jnp.zeros_like(hcarry_ref)
        pcarry_ref[...] = jnp.zeros_like(pcarry_ref)

    row = lax.broadcasted_iota(jnp.int32, (tm, 1), 0)
    p = jnp.dot(h.astype(BF16), params[0][...], preferred_element_type=F32)
    h_prev = jnp.where(row == 0, hcarry_ref[7:8, :], pltpu.roll(h, 1, 0))
    p_prev = jnp.where(row == 0, pcarry_ref[7:8, :], pltpu.roll(p, 1, 0))
    _rwkv_prep_math(h, h_prev, p, p_prev, params, outs, True)
    hcarry_ref[...] = h[tm - 8:, :]
    pcarry_ref[...] = p[tm - 8:, :]


def _rwkv_prep_sample_kernel(h_ref, hl_ref, *refs, seq):
    params, outs = refs[:N_PREP_PARAMS], refs[N_PREP_PARAMS:N_PREP_PARAMS + N_PREP_OUTS]
    h = h_ref[...]
    tm = h.shape[0]
    row = lax.broadcasted_iota(jnp.int32, (tm, 1), 0)
    first = row % seq == 0
    p = jnp.dot(h.astype(BF16), params[0][...], preferred_element_type=F32)
    p_last = jnp.dot(hl_ref[...].astype(BF16), params[0][...], preferred_element_type=F32)
    h_prev = jnp.where(first, hl_ref[...], pltpu.roll(h, 1, 0))
    p_prev = jnp.where(first, p_last, pltpu.roll(p, 1, 0))
    _rwkv_prep_math(h, h_prev, p, p_prev, params, outs, False)


def mixer_prep(h, h_last_rows, batch, seq, prm):
    n, d = h.shape
    d_r = prm["w0"].shape[-1]
    d_s = prm["w_sb"].shape[1] // 3
    n_h = d_s // HEAD_DIM
    plist = [prm["w_rkv"], prm["w_w1"], prm["w_w2"], prm["a_w1"], prm["a_w2"], prm["g_w1"], prm["g_w2"],
             prm["mu_rkv"], prm["mu_wag"], prm["w0"], prm["a0"], prm["k_k"], prm["k_a"], prm["e_head"],
             prm["w_sb"]]
    assert len(plist) == N_PREP_PARAMS
    if h_last_rows is None:
        tm = min(ROW_TILE, seq)
        kern = functools.partial(_rwkv_prep_prompt_kernel, tiles_per_seq=seq // tm)
        args = [h] + plist
        in_specs = [pl.BlockSpec((tm, d), lambda i: (i, 0))] + [_full_spec(a, 1) for a in plist]
        scratch = [pltpu.VMEM((8, d), F32), pltpu.VMEM((8, 3 * d_r), F32)]
    else:
        tm = n
        kern = functools.partial(_rwkv_prep_sample_kernel, seq=seq)
        args = [h, h_last_rows] + plist
        in_specs = [pl.BlockSpec((tm, d), lambda i: (i, 0))] * 2 + [_full_spec(a, 1) for a in plist]
        scratch = []
    row_spec = lambda w: pl.BlockSpec((tm, w), lambda i: (i, 0))
    if h_last_rows is None:
        tps = seq // tm
        kv_spec = pl.BlockSpec((None, n_h, HEAD_DIM, tm), lambda i: (i // tps, 0, 0, i % tps))
        kv_shape = jax.ShapeDtypeStruct((batch, n_h, HEAD_DIM, seq), F32)
    else:
        kv_spec = row_spec(d_s)
        kv_shape = jax.ShapeDtypeStruct((n, d_s), F32)
    return pl.pallas_call(
        kern,
        grid=(n // tm,),
        in_specs=in_specs,
        out_specs=[row_spec(d_r)] * 7 + [row_spec(d_s), kv_spec, kv_spec, row_spec(d_s), row_spec(d_s)],
        out_shape=[jax.ShapeDtypeStruct((n, d_r), F32)] * 7
                  + [jax.ShapeDtypeStruct((n, d_s), BF16), kv_shape, kv_shape,
                     jax.ShapeDtypeStruct((n, d_s), BF16), jax.ShapeDtypeStruct((n, d_s), BF16)],
        scratch_shapes=scratch,
        compiler_params=pltpu.CompilerParams(
            dimension_semantics=("arbitrary",), vmem_limit_bytes=VMEM_LIMIT),
        name="mixer_prep",
    )(*args)


def _scan_group(n_chunks):
    for g in (SCAN_GROUP, 2, 1):
        if n_chunks % g == 0:
            return g


def _stack_heads(x, m0):
    return jnp.concatenate([jnp.where(m0, x, 0.0), jnp.where(m0, 0.0, x)], axis=0)


def _rwkv_scan_kernel(r_ref, lw_ref, k_ref, a_ref, b_ref, v_ref, g_ref, s0_ref,
                      rk_ref, lg_ref, lb_ref, tri_ref, emean_ref, eones_ref,
                      o_ref, sfin_ref, s_ref, ta_ref, tv_ref, mr_ref, cl_ref, *, n_chunks, n_pairs, n_seq):
    i = pl.program_id(1)
    c2 = 2 * CHUNK
    seq_chunks = n_chunks // n_seq

    @pl.when(i == 0)
    def _():
        zero = jnp.zeros((HEAD_DIM, HEAD_DIM), F32)
        for q in range(n_seq):
            for p in range(n_pairs):
                top = jnp.concatenate([s0_ref[q, 2 * p], zero], axis=1)
                bot = jnp.concatenate([zero, s0_ref[q, 2 * p + 1]], axis=1)
                s_ref[q, p] = jnp.concatenate([top, bot], axis=0)

    lane = lax.broadcasted_iota(jnp.int32, (1, PAIR), 1)
    m0 = lane < HEAD_DIM
    rr = lax.broadcasted_iota(jnp.int32, (c2, c2), 0)
    cc = lax.broadcasted_iota(jnp.int32, (c2, c2), 1)
    strict = cc < rr
    incl = cc <= rr
    eye = (cc == rr).astype(F32)
    tri = tri_ref[...]
    mm = lambda x, y: jnp.dot(x, y, preferred_element_type=F32)
    mm_nt = lambda x, y: lax.dot_general(x, y, (((1,), (1,)), ((), ())), preferred_element_type=F32)

    emean = emean_ref[...]
    eones = eones_ref[...]
    r_k = rk_ref[...]
    lnx_g = lg_ref[...]
    lnx_b = lb_ref[...]

    pairs = range(n_pairs)
    cat0 = lambda x, y: jnp.concatenate([x, y], axis=0)
    stack = lambda x: _stack_heads(x, m0).astype(BF16)
    in_refs = (r_ref, lw_ref, k_ref, a_ref, b_ref, v_ref, g_ref)

    def load_rows(c):
        rows = pl.ds(pl.multiple_of(c * CHUNK, CHUNK), CHUNK)
        return rows, [ref[rows, :] for ref in in_refs]

    def pre_prep(ins, p):
        lanes = slice(p * PAIR, (p + 1) * PAIR)
        r, lw, k, a, b, v, g = (x[:, lanes] for x in ins)
        hi, lo = _split(lw)
        lo2 = (lw - hi.astype(F32) - lo.astype(F32)).astype(BF16)
        cl = mm(tri, hi) + mm(tri, lo) + mm(tri, lo2)
        w_inv = jnp.exp(-cl)
        return dict(cl=cl, vst=stack(v), ast=stack(a * jnp.exp(cl - lw)), rst=stack(r * jnp.exp(cl)),
                    bst=stack(b * w_inv), kst=stack(k * w_inv))

    def state_prep(ins, cl, p):
        lanes = slice(p * PAIR, (p + 1) * PAIR)
        r, lw, k, a, b, v, g = (x[:, lanes] for x in ins)
        cl_end = cl[CHUNK - 1:CHUNK, :]
        w_tail = jnp.exp(cl_end - cl)
        vst_f = _stack_heads(v, m0)
        return dict(r=r, k=k, v=v, g=g, lanes=lanes, w_end=jnp.exp(cl_end), vst_f=vst_f, vst=vst_f.astype(BF16),
                    rst=stack(r * jnp.exp(cl)), btl=stack(b * w_tail), ktl=stack(k * w_tail))

    grp = _scan_group(n_chunks)
    n_grp = n_chunks // grp
    assert n_seq == 1 or n_grp == 1

    def step(g_state, g_pre, slot):
        todo = []
        if g_state is not None:
            seq_of = [0 if n_seq == 1 else j // seq_chunks for j in range(grp)]
            s_all = [[s_ref[q, p] for p in pairs] for q in range(n_seq)]
            done = []
            for j in range(grp):
                s_cur = s_all[seq_of[j]]

                def first(j=j, s_cur=s_cur):
                    rows, ins = load_rows(g_state * grp + j)
                    s = list(s_cur)
                    sb = [x.astype(BF16) for x in s]
                    u = [mm_nt(ta_ref[slot, j, p], sb[p]) + tv_ref[slot, j, p] for p in pairs]
                    return rows, ins, s, sb, u

                def second(st, j=j, s_cur=s_cur):
                    rows, ins, s, sb, u = st
                    e = [state_prep(ins, cl_ref[slot, j, p], p) for p in pairs]
                    uv_t = [cat0(u[p], e[p]["vst_f"]).T.astype(BF16) for p in pairs]
                    for p in pairs:
                        s_cur[p] = s[p] * e[p]["w_end"] + mm(uv_t[p], cat0(e[p]["btl"], e[p]["ktl"]))
                    y = [mm_nt(e[p]["rst"], sb[p])
                         + mm(mr_ref[slot, j, p], cat0(u[p].astype(BF16), e[p]["vst"])) for p in pairs]
                    done.append((rows, e, y))
                todo.append((first, second))

        def run_state_piece(state):
            idx, pending = state
            if idx >= len(todo):
                return state
            if pending is None:
                return idx, todo[idx][0]()
            todo[idx][1](pending)
            return idx + 1, None

        cursor = (0, None)
        if g_pre is not None:
            chains = [(j, p) for j in range(grp) for p in pairs]
            ins_p = [load_rows(g_pre * grp + j)[1] for j in range(grp)]
            cursor = run_state_piece(cursor)
            d = [pre_prep(ins_p[j], p) for j, p in chains]
            m4 = [mm_nt(cat0(x["ast"], x["rst"]), cat0(x["bst"], x["kst"])) for x in d]
            cursor = run_state_piece(cursor)
            nab = [jnp.where(strict, m[:c2, :c2], 0.0) for m in m4]
            mak = [jnp.where(strict, m[:c2, c2:], 0.0).astype(BF16) for m in m4]
            mrbk = [jnp.concatenate([jnp.where(incl, m[c2:, :c2], 0.0), jnp.where(incl, m[c2:, c2:], 0.0)],
                                    axis=1).astype(BF16) for m in m4]
            t = [eye + n for n in nab]
            pw = [n.astype(BF16) for n in nab]
            for _ in range(5):
                pw = [mm(x, x).astype(BF16) for x in pw]
                cursor = run_state_piece(cursor)
                t = [tt + mm(x, tt.astype(BF16)) for x, tt in zip(pw, t)]
                cursor = run_state_piece(cursor)
            mv = [mm(mk, x["vst"]) for mk, x in zip(mak, d)]
            tx = [mm(tt.astype(BF16), jnp.concatenate([x["ast"], m.astype(BF16)], axis=1))
                  for tt, x, m in zip(t, d, mv)]
        while cursor[0] < len(todo):
            cursor = run_state_piece(cursor)
        if g_state is not None:
            for rows, e, y in done:
                yp = [x[:CHUNK, :] + x[CHUNK:, :] for x in y]
                mu = [_dot_hl(x, emean) for x in yp]
                yc = [x - m for x, m in zip(yp, mu)]
                var = [_dot_hl(x * x, emean) for x in yc]
                bonus = [_dot_hl(x["r"] * x["k"] * r_k[:, x["lanes"]], eones) * x["v"] for x in e]
                outs = [(yc[p] * lax.rsqrt(var[p] + LNX_EPS) * lnx_g[:, e[p]["lanes"]] + lnx_b[:, e[p]["lanes"]]
                         + bonus[p]) * e[p]["g"] for p in pairs]
                o_ref[rows, :] = jnp.concatenate(outs, axis=1)
            for q in range(n_seq):
                for p in pairs:
                    s_ref[q, p] = s_all[q][p]
        if g_pre is not None:
            for (j, p), x, txx, mr in zip(chains, d, tx, mrbk):
                ta_ref[1 - slot, j, p] = txx[:, :PAIR].astype(BF16)
                tv_ref[1 - slot, j, p] = txx[:, PAIR:]
                mr_ref[1 - slot, j, p] = mr
                cl_ref[1 - slot, j, p] = x["cl"]

    step(None, 0, 1)

    def body(gi, carry):
        step(gi, gi + 1, lax.rem(gi, 2))
        return carry

    lax.fori_loop(0, n_grp - 1, body, 0)
    step(n_grp - 1, None, (n_grp - 1) % 2)
    for q in range(n_seq):
        for p in range(n_pairs):
            sfin_ref[q, 2 * p] = s_ref[q, p, :HEAD_DIM, :HEAD_DIM]
            sfin_ref[q, 2 * p + 1] = s_ref[q, p, HEAD_DIM:, HEAD_DIM:]


def rwkv_scan(r, lw, k, a, b, v, g, s0, r_k, lnx_g, lnx_b, batch, seq):
    n, d_r = r.shape
    n_pairs = d_r // PAIR
    if seq > CHUNK:
        n_seq, rows = 1, min(SCAN_ROWS, seq)
    else:
        n_seq = next(q for q in (SCAN_GROUP, 2, 1) if batch % q == 0)
        rows = n_seq * seq
    steps = max(seq // rows, 1)
    grp = _scan_group(rows // CHUNK)
    tri = jnp.tril(jnp.ones((CHUNK, CHUNK), F32)).astype(BF16)
    blk = jnp.arange(PAIR) // HEAD_DIM
    same = (blk[:, None] == blk[None, :]).astype(F32)
    emean = (same / HEAD_DIM).astype(BF16)
    eones = same.astype(BF16)
    tok = pl.BlockSpec((rows, d_r), lambda bi, i: (bi * steps + i, 0))
    par = pl.BlockSpec((1, d_r), lambda bi, i: (0, 0))
    st = pl.BlockSpec((n_seq, 2 * n_pairs, HEAD_DIM, HEAD_DIM), lambda bi, i: (bi, 0, 0, 0))
    return pl.pallas_call(
        functools.partial(_rwkv_scan_kernel, n_chunks=rows // CHUNK, n_pairs=n_pairs, n_seq=n_seq),
        grid=(batch // n_seq, steps),
        in_specs=[tok] * 7 + [st, par, par, par, _full_spec(tri, 2), _full_spec(emean, 2), _full_spec(eones, 2)],
        out_specs=[tok, st],
        out_shape=[jax.ShapeDtypeStruct((n, d_r), F32),
                   jax.ShapeDtypeStruct((batch, 2 * n_pairs, HEAD_DIM, HEAD_DIM), F32)],
        scratch_shapes=[pltpu.VMEM((n_seq, n_pairs, PAIR, PAIR), F32),
                        pltpu.VMEM((2, grp, n_pairs, PAIR, PAIR), BF16),
                        pltpu.VMEM((2, grp, n_pairs, PAIR, PAIR), F32),
                        pltpu.VMEM((2, grp, n_pairs, PAIR, 2 * PAIR), BF16),
                        pltpu.VMEM((2, grp, n_pairs, CHUNK, PAIR), F32)],
        compiler_params=pltpu.CompilerParams(
            dimension_semantics=("arbitrary", "arbitrary"), vmem_limit_bytes=VMEM_LIMIT),
        name="rwkv_scan",
    )(r, lw, k, a, b, v, g, s0, r_k.reshape(1, d_r), lnx_g.reshape(1, d_r), lnx_b.reshape(1, d_r),
      tri, emean, eones)


def _sb_prompt_kernel(bias_ref, q_ref, k_ref, v_ref, u_ref, o_ref, acc_ref, c_ref, *, tq, tk, n_hp):
    g = pl.program_id(1)
    i = pl.program_id(2)
    n_sub = tq // tk
    n_heads = 2 * n_hp
    lane = lax.broadcasted_iota(jnp.int32, (1, PAIR), 1)
    m0 = lane < HEAD_DIM
    pair_lanes = [slice((h // 2) * PAIR, (h // 2 + 1) * PAIR) for h in range(n_heads)]
    q_heads = []
    for h in range(n_heads):
        qp = q_ref[:, pair_lanes[h]]
        qh = jnp.where(m0 if h % 2 == 0 else ~m0, qp, jnp.zeros_like(qp))
        bias = jnp.full((1, PAIR), bias_ref[n_heads * g + h], F32)
        b1 = bias.astype(BF16).astype(F32)
        b2 = (bias - b1).astype(BF16).astype(F32)
        b3 = (bias - b1) - b2
        ext = jnp.where(lane == 0, b1, jnp.where(lane == 1, b2, jnp.where(lane == 2, b3, 0.0)))
        q_heads.append(jnp.concatenate([qh, jnp.broadcast_to(ext, qh.shape).astype(BF16)], axis=1))
    k_ext = {nk: jnp.broadcast_to(jnp.where(lane < 3, 1.0, 0.0), (nk, PAIR)).astype(BF16)
             for nk in range(tk, tq + 1, tk)}
    acc_ref[...] = jnp.zeros_like(acc_ref)
    c_ref[...] = jnp.zeros_like(c_ref)
    u = u_ref[...]
    rr = lax.broadcasted_iota(jnp.int32, (tk, tk), 0)
    cc = lax.broadcasted_iota(jnp.int32, (tk, tk), 1)
    earlier = cc < rr
    tile = lambda x: jnp.concatenate([x] * (tk // PAIR), axis=1)

    def block(kb, q0, nq_rows, n_keys_sub, mask_last):
        qrows = slice(q0, q0 + nq_rows)
        nk = n_keys_sub * tk
        rows = pl.ds(pl.multiple_of(kb * tq, tq), nk)
        subs = [slice(j * tk, (j + 1) * tk) for j in range(n_keys_sub)]
        ks = [jnp.concatenate([k_ref[rows, pair_lanes[h]], k_ext[nk]], axis=1) for h in range(n_heads)]
        vs = [v_ref[rows, pair_lanes[h]] for h in range(n_heads)]
        zs = [lax.dot_general(qh[qrows], kh, (((1,), (1,)), ((), ())), preferred_element_type=F32)
              for qh, kh in zip(q_heads, ks)]
        ns = [jnp.maximum(z, 0.0) + jnp.log(1.0 + jnp.exp(_neg_abs(z))) for z in zs]
        masks =[earlier if (mask_last and j == n_keys_sub - 1) else None for j in range(n_keys_sub)]
        nm = [[n[:, s] if m is None else jnp.where(m, n[:, s], 0.0) for s, m in zip(subs, masks)] for n in ns]
        cums = [[jnp.dot(x.astype(BF16), u, preferred_element_type=F32) for x in nh] for nh in nm]
        tots = [[jnp.broadcast_to(cum[:, :1] + x[:, :1], (nq_rows, PAIR)) for cum, x in zip(cm, nh)]
                for cm, nh in zip(cums, nm)]
        ws = []
        for h in range(n_heads):
            run = c_ref[h, qrows]
            parts = [None] * n_keys_sub
            for j in range(n_keys_sub - 1, -1, -1):
                parts[j] = cums[h][j] + tile(run)
                run = run + tots[h][j]
            c_ref[h, qrows] = run
            w = jnp.exp((zs[h] - ns[h]) - jnp.concatenate(parts, axis=1))
            if mask_last:
                w = jnp.concatenate([w[:, s] if m is None else jnp.where(m, w[:, s], 0.0)
                                     for s, m in zip(subs, masks)], axis=1)
            ws.append(w)
        for h in range(n_heads):
            acc_ref[h, qrows] += jnp.dot(ws[h].astype(BF16), vs[h], preferred_element_type=F32)

    for r in range(n_sub):
        block(i, r * tk, tk, r + 1, True)

    def body(it, carry):
        block(i - 1 - it, 0, tq, n_sub, False)
        return carry

    lax.fori_loop(0, i, body, 0)
    o_ref[...] = jnp.concatenate(
        [jnp.where(m0, acc_ref[2 * hp], acc_ref[2 * hp + 1]) for hp in range(n_hp)], axis=1)


def sb_prompt(q_bf, k_bf, v_bf, bias, batch, seq):
    n, d_s = q_bf.shape
    n_hp = SB_PAIRS
    width = n_hp * PAIR
    n_grp = d_s // width
    tq = min(SB_TILE, seq)
    tk = min(SB_KEYS, seq)
    nq = seq // tq
    j = jnp.arange(tk)
    ux = (j[:, None] > j[None, :]).astype(BF16)
    grid_spec = pltpu.PrefetchScalarGridSpec(
        num_scalar_prefetch=1,
        grid=(batch, n_grp, nq),
        in_specs=[
            pl.BlockSpec((tq, width), lambda b, p, i, bias_ref: (b * nq + i, p)),
            pl.BlockSpec((seq, width), lambda b, p, i, bias_ref: (b, p)),
            pl.BlockSpec((seq, width), lambda b, p, i, bias_ref: (b, p)),
            pl.BlockSpec(ux.shape, lambda b, p, i, bias_ref: (0, 0)),
        ],
        out_specs=pl.BlockSpec((tq, width), lambda b, p, i, bias_ref: (b * nq + i, p)),
        scratch_shapes=[pltpu.VMEM((2 * n_hp, tq, PAIR), F32)] * 2,
    )
    return pl.pallas_call(
        functools.partial(_sb_prompt_kernel, tq=tq, tk=tk, n_hp=n_hp),
        grid_spec=grid_spec,
        out_shape=jax.ShapeDtypeStruct((n, d_s), F32),
        compiler_params=pltpu.CompilerParams(
            dimension_semantics=("arbitrary", "arbitrary", "arbitrary"), vmem_limit_bytes=VMEM_LIMIT),
        name="sb_prompt",
    )(bias, q_bf, k_bf, v_bf, ux)


def _sb_sample_kernel(pt_ref, q_ref, brow_ref, kn_ref, vn_ref, ck_hbm, cv_hbm, ux_ref, o_ref,
                      acc_ref, c_ref, kbuf, vbuf, sem, *, n_pg, n_heads, n_pages, depth):
    nbuf = depth + 1
    steps = pl.num_programs(1)
    total = pl.num_programs(0) * steps
    j = pl.program_id(1)
    t = pl.program_id(0) * steps + j

    def page_copies(step, slot):
        bb = step // steps
        base = n_pages - n_pg * (step - bb * steps + 1)
        cps = []
        for i in range(n_pg):
            page = pt_ref[bb, base + i]
            cps.append(pltpu.make_async_copy(ck_hbm.at[page], kbuf.at[slot, i], sem.at[slot]))
            cps.append(pltpu.make_async_copy(cv_hbm.at[page], vbuf.at[slot, i], sem.at[slot]))
        return cps

    @pl.when(t == 0)
    def _():
        for d in range(depth):
            for cp in page_copies(d, d):
                cp.start()

    slot = lax.rem(t, nbuf)
    for cp in page_copies(t, slot):
        cp.wait()
    kps = [kbuf.at[slot, i] for i in range(n_pg)]
    vps = [vbuf.at[slot, i] for i in range(n_pg)]
    qbd = q_ref[...]
    nrow = qbd.shape[0]
    brow = brow_ref[...]
    ux = ux_ref[...]

    def log_terms(z, valid):
        nb = z.shape[1] // PAIR
        z = z + jnp.concatenate([brow] * nb, axis=1)
        sp = _softplus(z)
        l = -sp
        if valid is not None:
            l = jnp.where(valid, l, 0.0)
        lst = jnp.concatenate([l[:, bi * PAIR:(bi + 1) * PAIR] for bi in range(nb)], axis=0)
        return z - sp, _dot_hl(lst, ux)

    def weights(la, cx, run, valid):
        nb = la.shape[1] // PAIR
        parts = [None] * nb
        for bi in range(nb - 1, -1, -1):
            blk = cx[bi * nrow:(bi + 1) * nrow, :]
            parts[bi] = blk[:, :PAIR] + run
            run = run + blk[:, PAIR:]
        w = jnp.exp(la + jnp.concatenate(parts, axis=1))
        if valid is not None:
            w = jnp.where(valid, w, 0.0)
        return w.astype(BF16), run

    @pl.when(j == 0)
    def _():
        t_of_row = lax.broadcasted_iota(jnp.int32, (nrow, PAIR), 0) // n_heads
        key = lax.broadcasted_iota(jnp.int32, (nrow, PAIR), 1)
        valid = key < t_of_row
        la, cx = log_terms(_dot_nt(qbd, kn_ref[...]), valid)
        w, run = weights(la, cx, jnp.zeros((nrow, PAIR), F32), valid)
        acc_ref[...] = _dot(w, vn_ref[...])
        c_ref[...] = run

    d_s = qbd.shape[1]
    page_t = lambda r: r[...].reshape(d_s, PAIR).astype(BF16)
    n_grp = 2 if n_pg % 2 == 0 else 1
    per = n_pg // n_grp
    groups = [range(gi * per, (gi + 1) * per) for gi in range(n_grp)]
    zs = [jnp.concatenate([_dot(qbd, page_t(kps[pi])) for pi in grp], axis=1) for grp in groups]
    terms = [log_terms(z, None) for z in zs]
    run = c_ref[...]
    ws = [None] * n_grp
    for gi in range(n_grp - 1, -1, -1):
        ws[gi], run = weights(terms[gi][0], terms[gi][1], run, None)
    c_ref[...] = run
    out = None
    for grp, w in zip(groups, ws):
        for li, pi in enumerate(grp):
            o = _dot_nt(w[:, li * PAIR:(li + 1) * PAIR], page_t(vps[pi]))
            out = o if out is None else out + o
    acc_ref[...] += out

    nxt = t + depth
    for cp in page_copies(lax.rem(nxt, total), lax.rem(nxt, nbuf)):
        cp.start()

    @pl.when(j == steps - 1)
    def _():
        d_s = acc_ref.shape[1]
        head_of_row = lax.broadcasted_iota(jnp.int32, (nrow, d_s), 0) % n_heads
        head_of_lane = lax.broadcasted_iota(jnp.int32, (nrow, d_s), 1) // HEAD_DIM
        sel = jnp.where(head_of_row == head_of_lane, acc_ref[...], 0.0)
        o_ref[...] = jnp.sum(sel.reshape(nrow // n_heads, n_heads, d_s), axis=1)

    @pl.when(t == total - 1)
    def _():
        for d in range(1, depth + 1):
            for cp in page_copies(lax.rem(t + d, total), lax.rem(t + d, nbuf)):
                cp.wait()


def sb_sample(q_bd, brow, k_new, v_new, cache_k, cache_v, page_table, n_heads):
    bsz, nrow, d_s = q_bd.shape
    n_pages = page_table.shape[1]
    page = cache_k.shape[3]
    assert page == PAIR
    n_pg = min(SB_PAGES, n_pages)
    steps = n_pages // n_pg
    jj = jnp.arange(PAIR)
    ux = jnp.concatenate([(jj[:, None] > jj[None, :]).astype(F32), jnp.ones((PAIR, PAIR), F32)], axis=1).astype(BF16)

    depth = SB_PREFETCH
    assert bsz * steps >= depth
    per_b = lambda shape: pl.BlockSpec((None,) + shape, lambda b, j, pt: (b, 0, 0))
    page_buf = pltpu.VMEM((depth + 1, n_pg, n_heads, HEAD_DIM, page), F32)
    grid_spec = pltpu.PrefetchScalarGridSpec(
        num_scalar_prefetch=1,
        grid=(bsz, steps),
        in_specs=[per_b((nrow, d_s)), pl.BlockSpec(brow.shape, lambda b, j, pt: (0, 0)),
                  per_b((PAIR, d_s)), per_b((PAIR, d_s)),
                  pl.BlockSpec(memory_space=pl.ANY), pl.BlockSpec(memory_space=pl.ANY),
                  pl.BlockSpec(ux.shape, lambda b, j, pt: (0, 0))],
        out_specs=per_b((nrow // n_heads, d_s)),
        scratch_shapes=[pltpu.VMEM((nrow, d_s), F32), pltpu.VMEM((nrow, PAIR), F32), page_buf, page_buf,
                        pltpu.SemaphoreType.DMA((depth + 1,))],
    )
    return pl.pallas_call(
        functools.partial(_sb_sample_kernel, n_pg=n_pg, n_heads=n_heads, n_pages=n_pages, depth=depth),
        grid_spec=grid_spec,
        out_shape=jax.ShapeDtypeStruct((bsz, nrow // n_heads, d_s), F32),
        compiler_params=pltpu.CompilerParams(
            dimension_semantics=("arbitrary", "arbitrary"), vmem_limit_bytes=VMEM_LIMIT),
        name="sb_sample",
    )(page_table, q_bd, brow, k_new, v_new, cache_k, cache_v, ux)


def _merge_kernel(h_ref, oa_ref, ob_ref, wg_ref, wba_ref, wbb_ref, wo_ref, g_ref, b_ref, o_ref, *, alpha):
    h = h_ref[...]
    d = h.shape[1]
    gates = jax.nn.sigmoid(jnp.dot(h.astype(BF16), wg_ref[...], preferred_element_type=F32))
    ma = jnp.dot(oa_ref[...].astype(BF16), wba_ref[...], preferred_element_type=F32)
    mb = jnp.dot(ob_ref[...].astype(BF16), wbb_ref[...], preferred_element_type=F32)
    merged = gates[:, :d] * ma + gates[:, d:] * mb
    y = alpha * h + jnp.dot(merged.astype(BF16), wo_ref[...], preferred_element_type=F32)
    o_ref[...] = _layer_norm(y, g_ref[...], b_ref[...])


def merge(h, o_a, o_b, w_gate_bf, w_ba_bf, w_bb_bf, w_out_bf, g, b, alpha):
    n, d = h.shape
    tm = min(ROW_TILE, n)
    row = lambda a: pl.BlockSpec((tm, a.shape[1]), lambda i: (i, 0))
    ws = [w_gate_bf, w_ba_bf, w_bb_bf, w_out_bf, g.reshape(1, d), b.reshape(1, d)]
    return pl.pallas_call(
        functools.partial(_merge_kernel, alpha=alpha),
        grid=(n // tm,),
        in_specs=[row(h), row(o_a), row(o_b)] + [_full_spec(a, 1) for a in ws],
        out_specs=pl.BlockSpec((tm, d), lambda i: (i, 0)),
        out_shape=jax.ShapeDtypeStruct((n, d), F32),
        compiler_params=pltpu.CompilerParams(
            dimension_semantics=("arbitrary",), vmem_limit_bytes=VMEM_LIMIT),
        name="merge",
    )(h, o_a, o_b, *ws)


def _layer(x, batch, seq, h_last, wkv0, sb_fn, w, alpha):
    n, d = x.shape
    h = ffn_ln(x, w["ffn1_in"], w["ffn1_out"], w["ln1_g"], w["ln1_b"], alpha)
    if h_last is None:
        hl_rows = None
        rows_pad = seq
    else:
        hl_rows = jnp.repeat(h_last, seq, axis=0)
        rows_pad = CHUNK
    r, lw, k, a, b, v, g, q_bf, k_sb, v_sb, k_bf, v_bf = mixer_prep(h, hl_rows, batch, seq, w)
    n_h = w["w0"].shape[-1] // HEAD_DIM
    if wkv0 is None:
        s0 = jnp.zeros((batch, n_h, HEAD_DIM, HEAD_DIM), F32)
    else:
        s0 = wkv0.astype(F32)
    scan_in = [r, lw, k, a, b, v, g]
    if rows_pad != seq:
        pad = lambda t: jnp.pad(t.reshape(batch, seq, -1), ((0, 0), (0, rows_pad - seq), (0, 0))).reshape(
            batch * rows_pad, -1)
        scan_in = [pad(t) for t in scan_in]
    o_a, wkv = rwkv_scan(*scan_in, s0, w["r_k"], w["lnx_g"], w["lnx_b"], batch, rows_pad)
    if rows_pad != seq:
        o_a = o_a.reshape(batch, rows_pad, -1)[:, :seq].reshape(n, -1)
    o_b = sb_fn(q_bf, k_sb, v_sb, k_bf, v_bf)
    x2 = merge(h, o_a, o_b, w["w_gate"], w["w_ba"], w["w_bb"], w["w_out"], w["ln2_g"], w["ln2_b"], alpha)
    x3 = ffn_ln(x2, w["ffn2_in"], w["ffn2_out"], w["ln3_g"], w["ln3_b"], alpha)
    return x3, k_sb, v_sb, wkv, h.reshape(batch, seq, d)[:, -1]


def kernel(x_prompt, x_sample, cache_k, cache_v, state_wkv, state_shift, page_table, ln1_g, ln1_b, ffn1_w_in, ffn1_w_out, w_in, mu_rkv, mu_wag, w0, w_w1, w_w2, a0, a_w1, a_w2, g_w1, g_w2, k_k, k_a, r_k, lnx_g, lnx_b, sb_bias, w_branch, w_out, ln2_g, ln2_b, ffn2_w_in, ffn2_w_out, ln3_g, ln3_b):
    depth = ln1_g.shape[0]
    alpha = (2.0 * depth) ** 0.25
    bp, tp, d = x_prompt.shape
    bs, ts, _ = x_sample.shape
    d_r = w0.shape[-1]
    d_s = (w_in.shape[-1] - 3 * d_r - 2 * d) // 3
    n_hs = d_s // HEAD_DIM
    blk = jnp.arange(d_r) // HEAD_DIM
    e_head = (blk[:, None] == blk[None, :]).astype(BF16)

    xp = x_prompt.reshape(bp * tp, d)
    xs = x_sample.reshape(bs * ts, d)
    outs = [[] for _ in range(8)]
    for l in range(depth):
        row = lambda a: a[l].reshape(1, -1)
        w = dict(
            ffn1_in=ffn1_w_in[l].astype(BF16), ffn1_out=ffn1_w_out[l].astype(BF16),
            ffn2_in=ffn2_w_in[l].astype(BF16), ffn2_out=ffn2_w_out[l].astype(BF16),
            ln1_g=ln1_g[l], ln1_b=ln1_b[l], ln2_g=ln2_g[l], ln2_b=ln2_b[l], ln3_g=ln3_g[l], ln3_b=ln3_b[l],
            w_rkv=w_in[l][:, :3 * d_r].astype(BF16),
            w_sb=w_in[l][:, 3 * d_r:3 * d_r + 3 * d_s].astype(BF16),
            w_gate=w_in[l][:, 3 * d_r + 3 * d_s:].astype(BF16),
            w_w1=w_w1[l].astype(BF16), w_w2=w_w2[l].astype(BF16), a_w1=a_w1[l].astype(BF16),
            a_w2=a_w2[l].astype(BF16), g_w1=g_w1[l].astype(BF16), g_w2=g_w2[l].astype(BF16),
            mu_rkv=row(mu_rkv), mu_wag=mu_wag[l], w0=row(w0), a0=row(a0), k_k=row(k_k), k_a=row(k_a),
            e_head=e_head, r_k=r_k[l].reshape(-1), lnx_g=lnx_g[l], lnx_b=lnx_b[l],
            w_ba=w_branch[l][:d_r].astype(BF16), w_bb=w_branch[l][d_r:].astype(BF16),
            w_out=w_out[l].astype(BF16),
        )
        bias = sb_bias[l].astype(F32)

        def prompt_sb(q_bf, k_sb, v_sb, k_bf, v_bf):
            return sb_prompt(q_bf, k_bf, v_bf, bias, bp, tp)

        def sample_sb(q_bf, k_sb, v_sb, k_bf, v_bf):
            head_of_lane = jnp.arange(d_s) // HEAD_DIM
            onehot = (jnp.arange(n_hs)[:, None] == head_of_lane[None, :])
            q_bd = jnp.where(onehot[None, None], q_bf.reshape(bs, ts, 1, d_s), jnp.zeros((), BF16))
            q_bd = q_bd.reshape(bs, ts * n_hs, d_s)
            brow = jnp.broadcast_to(jnp.tile(bias, ts)[:, None], (ts * n_hs, PAIR))
            padk = lambda t: jnp.pad(t.reshape(bs, ts, d_s), ((0, 0), (0, PAIR - ts), (0, 0)))
            ck = jnp.transpose(cache_k[l], (0, 2, 3, 1))
            cv = jnp.transpose(cache_v[l], (0, 2, 3, 1))
            o = sb_sample(q_bd, brow, padk(k_sb), padk(v_sb), ck, cv, page_table, n_hs)
            return o.reshape(bs * ts, d_s)

        xp, kp, vp, wp, hp = _layer(xp, bp, tp, None, None, prompt_sb, w, alpha)
        xs, ks_, vs_, ws_, hs_ = _layer(xs, bs, ts, state_shift[l].astype(F32), state_wkv[l], sample_sb, w, alpha)
        for lst, val in zip(outs, (jnp.transpose(kp, (0, 3, 1, 2)), jnp.transpose(vp, (0, 3, 1, 2)), wp, hp,
                                   ks_.reshape(bs, ts, n_hs, HEAD_DIM), vs_.reshape(bs, ts, n_hs, HEAD_DIM),
                                   ws_.astype(state_wkv.dtype), hs_)):
            lst.append(val)
    return (xp.reshape(bp, tp, d), xs.reshape(bs, ts, d)) + tuple(jnp.stack(o) for o in outs)
```

```python
import functools

import jax
import jax.numpy as jnp
from jax import lax
from jax.experimental import pallas as pl
from jax.experimental.pallas import tpu as pltpu

F32 = jnp.float32
BF16 = jnp.bfloat16

HEAD_DIM = 64
PAIR = 2 * HEAD_DIM
CHUNK = 64
LN_EPS = 1e-5
LNX_EPS = 64e-5
SB_SCALE = HEAD_DIM ** -0.5

ROW_TILE = 512
FFN_ROWS = 1024
MERGE_ROWS = 1024
FFN_SUB = 512
FFN_RESIDENT_BYTES = 20 * 1024 * 1024
SCAN_ROWS = 1024
SCAN_GROUP = 4
SB_TILE = 512
SB_KEYS = 256
SB_PAIRS = 2
SB_PAGES = 16
SB_PREFETCH = 3
VMEM_LIMIT = 56 * 1024 * 1024


def _dot(a, b):
    return jnp.dot(a.astype(BF16), b.astype(BF16), preferred_element_type=F32)


def _dot_nt(a, b):
    return lax.dot_general(a.astype(BF16), b.astype(BF16), (((1,), (1,)), ((), ())),
                           preferred_element_type=F32)


def _split(x):
    hi = x.astype(BF16)
    lo = (x - hi.astype(F32)).astype(BF16)
    return hi, lo


def _dot_hl(a, b_exact):
    hi, lo = _split(a)
    return (jnp.dot(hi, b_exact, preferred_element_type=F32)
            + jnp.dot(lo, b_exact, preferred_element_type=F32))


def _dot3(a, b, nt=False):
    ah, al = _split(a)
    bh, bl = _split(b)
    if nt:
        d = lambda x, y: lax.dot_general(x, y, (((1,), (1,)), ((), ())), preferred_element_type=F32)
    else:
        d = lambda x, y: jnp.dot(x, y, preferred_element_type=F32)
    return d(ah, bh) + d(al, bh) + d(ah, bl)


LOG2E = 1.4426950408889634


def _neg_abs(x):
    bits = pltpu.bitcast(x, jnp.uint32) | jnp.uint32(0x80000000)
    return pltpu.bitcast(bits, F32)


def _softplus(u):
    return jnp.maximum(u, 0.0) + jnp.log1p(jnp.exp(-jnp.abs(u)))


def _layer_norm(y, g, b):
    mu = jnp.mean(y, axis=-1, keepdims=True)
    yc = y - mu
    var = jnp.mean(yc * yc, axis=-1, keepdims=True)
    return yc * lax.rsqrt(var + LN_EPS) * g + b


def _full_spec(a, grid_rank):
    nd = a.ndim
    if grid_rank == 1:
        return pl.BlockSpec(a.shape, lambda i: (0,) * nd)
    if grid_rank == 2:
        return pl.BlockSpec(a.shape, lambda i, j: (0,) * nd)
    return pl.BlockSpec(a.shape, lambda i, j, k: (0,) * nd)


def _ffn_ln_kernel(x_ref, wg_ref, wu_ref, wo_ref, g_ref, b_ref, o_ref, acc_ref, *, alpha, n_ff):
    j = pl.program_id(1)

    xb = x_ref[...].astype(BF16)
    tf = wg_ref.shape[1]
    cuts = list(range(0, tf, FFN_SUB)) + [tf]
    cols = [slice(a, b) for a, b in zip(cuts[:-1], cuts[1:])]
    gu = [(jnp.dot(xb, wg_ref[:, c], preferred_element_type=F32),
           jnp.dot(xb, wu_ref[:, c], preferred_element_type=F32)) for c in cols]
    mids = [(gate * jax.nn.sigmoid(gate) * up).astype(BF16) for gate, up in gu]
    out = jnp.dot(mids[0], wo_ref[cols[0], :], preferred_element_type=F32)
    for mid, c in zip(mids[1:], cols[1:]):
        out = out + jnp.dot(mid, wo_ref[c, :], preferred_element_type=F32)
    if n_ff == 1:
        o_ref[...] = _layer_norm(alpha * x_ref[...] + 0.5 * out, g_ref[...], b_ref[...])
        return

    @pl.when(j == 0)
    def _():
        acc_ref[...] = out

    @pl.when(j > 0)
    def _():
        acc_ref[...] += out

    @pl.when(j == n_ff - 1)
    def _():
        y = alpha * x_ref[...] + 0.5 * acc_ref[...]
        o_ref[...] = _layer_norm(y, g_ref[...], b_ref[...])


def _ffn_tile(d, d_ff):
    if 3 * d * d_ff * 2 <= FFN_RESIDENT_BYTES:
        return d_ff
    for n in (2, 11, 22):
        if d_ff % n == 0 and (d_ff // n) % 128 == 0:
            return d_ff // n
    return d_ff


def ffn_ln(x, w_in_bf, w_out_bf, g, b, alpha):
    n, d = x.shape
    d_ff = w_out_bf.shape[0]
    tm = min(FFN_ROWS, n)
    tf = _ffn_tile(d, d_ff)
    n_ff = d_ff // tf
    wmode = dict(pipeline_mode=pl.Buffered(1)) if n_ff == 1 else {}
    return pl.pallas_call(
        functools.partial(_ffn_ln_kernel, alpha=alpha, n_ff=n_ff),
        grid=(n // tm, n_ff),
        in_specs=[
            pl.BlockSpec((tm, d), lambda i, j: (i, 0)),
            pl.BlockSpec((d, tf), lambda i, j: (0, j), **wmode),
            pl.BlockSpec((d, tf), lambda i, j: (0, j + n_ff), **wmode),
            pl.BlockSpec((tf, d), lambda i, j: (j, 0), **wmode),
            pl.BlockSpec((1, d), lambda i, j: (0, 0)),
            pl.BlockSpec((1, d), lambda i, j: (0, 0)),
        ],
        out_specs=pl.BlockSpec((tm, d), lambda i, j: (i, 0)),
        out_shape=jax.ShapeDtypeStruct((n, d), F32),
        scratch_shapes=[pltpu.VMEM((tm, d), F32)],
        compiler_params=pltpu.CompilerParams(
            dimension_semantics=("arbitrary", "arbitrary"), vmem_limit_bytes=VMEM_LIMIT),
        name="ffn_ln",
    )(x, w_in_bf, w_in_bf, w_out_bf, g.reshape(1, d), b.reshape(1, d))


N_PREP_PARAMS = 15
N_PREP_OUTS = 12


def _rwkv_prep_math(h, h_prev, p, p_prev, params, outs, head_major):
    (wrkv_ref, ww1_ref, ww2_ref, aw1_ref, aw2_ref, gw1_ref, gw2_ref,
     mu_rkv_ref, mu_wag_ref, w0_ref, a0_ref, kk_ref, ka_ref, e_ref, wsb_ref) = params
    r_out, lw_out, k_out, a_out, b_out, v_out, g_out, q_sb, k_sb, v_sb, kb_sb, vb_sb = outs
    d_r = w0_ref.shape[-1]
    d_s = kb_sb.shape[-1]
    psb = jnp.dot(h.astype(BF16), wsb_ref[...], preferred_element_type=F32)
    q_sb[...] = (psb[:, :d_s] * SB_SCALE).astype(BF16)
    ksb = psb[:, d_s:2 * d_s]
    vsb = psb[:, 2 * d_s:]
    if head_major:
        k_sb[...] = ksb.T.reshape(k_sb.shape)
        v_sb[...] = vsb.T.reshape(v_sb.shape)
    else:
        k_sb[...] = ksb
        v_sb[...] = vsb
    kb_sb[...] = ksb.astype(BF16)
    vb_sb[...] = vsb.astype(BF16)
    rkv = p + (p_prev - p) * mu_rkv_ref[...]
    r = rkv[:, :d_r]
    k = rkv[:, d_r:2 * d_r]
    v = rkv[:, 2 * d_r:]
    dx = h_prev - h
    xw = h + dx * mu_wag_ref[0:1, :]
    xa = h + dx * mu_wag_ref[1:2, :]
    xg = h + dx * mu_wag_ref[2:3, :]
    lw = _dot(jnp.tanh(_dot(xw, ww1_ref[...])), ww2_ref[...])
    w_log = -_softplus(-(w0_ref[...] + lw)) - 0.5
    a_gate = jax.nn.sigmoid(a0_ref[...] + _dot(_dot(xa, aw1_ref[...]), aw2_ref[...]))
    g = _dot(jax.nn.sigmoid(_dot(xg, gw1_ref[...])), gw2_ref[...])
    kk = k * kk_ref[...]
    ss = _dot_hl(kk * kk, e_ref[...])
    kk = kk / jnp.maximum(jnp.sqrt(ss), 1e-12)
    r_out[...] = r
    lw_out[...] = -jnp.exp(w_log)
    k_out[...] = k * (1.0 + (a_gate - 1.0) * ka_ref[...])
    a_out[...] = -kk
    b_out[...] = kk * a_gate
    v_out[...] = v
    g_out[...] = g


def _rwkv_prep_prompt_kernel(h_ref, *refs, tiles_per_seq):
    params, outs = refs[:N_PREP_PARAMS], refs[N_PREP_PARAMS:N_PREP_PARAMS + N_PREP_OUTS]
    hcarry_ref, pcarry_ref = refs[-2:]
    i = pl.program_id(0)
    h = h_ref[...]
    tm = h.shape[0]

    @pl.when(i % tiles_per_seq == 0)
    def _():
        hcarry_ref[...] = jnp.zeros_like(hcarry_ref)
        pcarry_ref[...] = jnp.zeros_like(pcarry_ref)

    row = lax.broadcasted_iota(jnp.int32, (tm, 1), 0)
    p = jnp.dot(h.astype(BF16), params[0][...], preferred_element_type=F32)
    h_prev = jnp.where(row == 0, hcarry_ref[7:8, :], pltpu.roll(h, 1, 0))
    p_prev = jnp.where(row == 0, pcarry_ref[7:8, :], pltpu.roll(p, 1, 0))
    _rwkv_prep_math(h, h_prev, p, p_prev, params, outs, True)
    hcarry_ref[...] = h[tm - 8:, :]
    pcarry_ref[...] = p[tm - 8:, :]


def _rwkv_prep_sample_kernel(h_ref, hl_ref, *refs, seq):
    params, outs = refs[:N_PREP_PARAMS], refs[N_PREP_PARAMS:N_PREP_PARAMS + N_PREP_OUTS]
    h = h_ref[...]
    tm = h.shape[0]
    row = lax.broadcasted_iota(jnp.int32, (tm, 1), 0)
    first = row % seq == 0
    p = jnp.dot(h.astype(BF16), params[0][...], preferred_element_type=F32)
    p_last = jnp.dot(hl_ref[...].astype(BF16), params[0][...], preferred_element_type=F32)
    h_prev = jnp.where(first, hl_ref[...], pltpu.roll(h, 1, 0))
    p_prev = jnp.where(first, p_last, pltpu.roll(p, 1, 0))
    _rwkv_prep_math(h, h_prev, p, p_prev, params, outs, False)


def mixer_prep(h, h_last_rows, batch, seq, prm):
    n, d = h.shape
    d_r = prm["w0"].shape[-1]
    d_s = prm["w_sb"].shape[1] // 3
    n_h = d_s // HEAD_DIM
    plist = [prm["w_rkv"], prm["w_w1"], prm["w_w2"], prm["a_w1"], prm["a_w2"], prm["g_w1"], prm["g_w2"],
             prm["mu_rkv"], prm["mu_wag"], prm["w0"], prm["a0"], prm["k_k"], prm["k_a"], prm["e_head"],
             prm["w_sb"]]
    assert len(plist) == N_PREP_PARAMS
    if h_last_rows is None:
        tm = min(ROW_TILE, seq)
        kern = functools.partial(_rwkv_prep_prompt_kernel, tiles_per_seq=seq // tm)
        args = [h] + plist
        in_specs = [pl.BlockSpec((tm, d), lambda i: (i, 0))] + [_full_spec(a, 1) for a in plist]
        scratch = [pltpu.VMEM((8, d), F32), pltpu.VMEM((8, 3 * d_r), F32)]
    else:
        tm = n
        kern = functools.partial(_rwkv_prep_sample_kernel, seq=seq)
        args = [h, h_last_rows] + plist
        in_specs = [pl.BlockSpec((tm, d), lambda i: (i, 0))] * 2 + [_full_spec(a, 1) for a in plist]
        scratch = []
    row_spec = lambda w: pl.BlockSpec((tm, w), lambda i: (i, 0))
    if h_last_rows is None:
        tps = seq // tm
        kv_spec = pl.BlockSpec((None, n_h, HEAD_DIM, tm), lambda i: (i // tps, 0, 0, i % tps))
        kv_shape = jax.ShapeDtypeStruct((batch, n_h, HEAD_DIM, seq), F32)
    else:
        kv_spec = row_spec(d_s)
        kv_shape = jax.ShapeDtypeStruct((n, d_s), F32)
    return pl.pallas_call(
        kern,
        grid=(n // tm,),
        in_specs=in_specs,
        out_specs=[row_spec(d_r)] * 7 + [row_spec(d_s), kv_spec, kv_spec, row_spec(d_s), row_spec(d_s)],
        out_shape=[jax.ShapeDtypeStruct((n, d_r), F32)] * 7
                  + [jax.ShapeDtypeStruct((n, d_s), BF16), kv_shape, kv_shape,
                     jax.ShapeDtypeStruct((n, d_s), BF16), jax.ShapeDtypeStruct((n, d_s), BF16)],
        scratch_shapes=scratch,
        compiler_params=pltpu.CompilerParams(
            dimension_semantics=("arbitrary",), vmem_limit_bytes=VMEM_LIMIT),
        name="mixer_prep",
    )(*args)


def _scan_group(n_chunks):
    for g in (SCAN_GROUP, 2, 1):
        if n_chunks % g == 0:
            return g


def _stack_heads(x, m0):
    return jnp.concatenate([jnp.where(m0, x, 0.0), jnp.where(m0, 0.0, x)], axis=0)


def _rwkv_scan_kernel(r_ref, lw_ref, k_ref, a_ref, b_ref, v_ref, g_ref, s0_ref,
                      rk_ref, lg_ref, lb_ref, tri_ref, emean_ref, eones_ref,
                      o_ref, sfin_ref, s_ref, ta_ref, tv_ref, mr_ref, cl_ref, *, n_chunks, n_pairs, n_seq):
    i = pl.program_id(1)
    c2 = 2 * CHUNK
    seq_chunks = n_chunks // n_seq

    @pl.when(i == 0)
    def _():
        zero = jnp.zeros((HEAD_DIM, HEAD_DIM), F32)
        for q in range(n_seq):
            for p in range(n_pairs):
                top = jnp.concatenate([s0_ref[q, 2 * p], zero], axis=1)
                bot = jnp.concatenate([zero, s0_ref[q, 2 * p + 1]], axis=1)
                s_ref[q, p] = jnp.concatenate([top, bot], axis=0)

    lane = lax.broadcasted_iota(jnp.int32, (1, PAIR), 1)
    m0 = lane < HEAD_DIM
    rr = lax.broadcasted_iota(jnp.int32, (c2, c2), 0)
    cc = lax.broadcasted_iota(jnp.int32, (c2, c2), 1)
    strict = cc < rr
    incl = cc <= rr
    eye = (cc == rr).astype(F32)
    tri = tri_ref[...]
    mm = lambda x, y: jnp.dot(x, y, preferred_element_type=F32)
    mm_nt = lambda x, y: lax.dot_general(x, y, (((1,), (1,)), ((), ())), preferred_element_type=F32)

    emean = emean_ref[...]
    eones = eones_ref[...]
    r_k = rk_ref[...]
    lnx_g = lg_ref[...]
    lnx_b = lb_ref[...]

    pairs = range(n_pairs)
    cat0 = lambda x, y: jnp.concatenate([x, y], axis=0)
    stack = lambda x: _stack_heads(x, m0).astype(BF16)
    in_refs = (r_ref, lw_ref, k_ref, a_ref, b_ref, v_ref, g_ref)

    def load_rows(c):
        rows = pl.ds(pl.multiple_of(c * CHUNK, CHUNK), CHUNK)
        return rows, [ref[rows, :] for ref in in_refs]

    def pre_prep(ins, p):
        lanes = slice(p * PAIR, (p + 1) * PAIR)
        r, lw, k, a, b, v, g = (x[:, lanes] for x in ins)
        hi, lo = _split(lw)
        lo2 = (lw - hi.astype(F32) - lo.astype(F32)).astype(BF16)
        cl = mm(tri, hi) + mm(tri, lo) + mm(tri, lo2)
        w_inv = jnp.exp(-cl)
        return dict(cl=cl, vst=stack(v), ast=stack(a * jnp.exp(cl - lw)), rst=stack(r * jnp.exp(cl)),
                    bst=stack(b * w_inv), kst=stack(k * w_inv))

    def state_prep(ins, cl, p):
        lanes = slice(p * PAIR, (p + 1) * PAIR)
        r, lw, k, a, b, v, g = (x[:, lanes] for x in ins)
        cl_end = cl[CHUNK - 1:CHUNK, :]
        w_tail = jnp.exp(cl_end - cl)
        vst_f = _stack_heads(v, m0)
        return dict(r=r, k=k, v=v, g=g, lanes=lanes, w_end=jnp.exp(cl_end), vst_f=vst_f, vst=vst_f.astype(BF16),
                    rst=stack(r * jnp.exp(cl)), btl=stack(b * w_tail), ktl=stack(k * w_tail))

    grp = _scan_group(n_chunks)
    n_grp = n_chunks // grp
    assert n_seq == 1 or n_grp == 1

    def step(g_state, g_pre, slot):
        todo = []
        if g_state is not None:
            seq_of = [0 if n_seq == 1 else j // seq_chunks for j in range(grp)]
            s_all = [[s_ref[q, p] for p in pairs] for q in range(n_seq)]
            done = []
            for j in range(grp):
                s_cur = s_all[seq_of[j]]

                def first(j=j, s_cur=s_cur):
                    rows, ins = load_rows(g_state * grp + j)
                    s = list(s_cur)
                    sb = [x.astype(BF16) for x in s]
                    u = [mm_nt(ta_ref[slot, j, p], sb[p]) + tv_ref[slot, j, p] for p in pairs]
                    return rows, ins, s, sb, u

                def second(st, j=j, s_cur=s_cur):
                    rows, ins, s, sb, u = st
                    e = [state_prep(ins, cl_ref[slot, j, p], p) for p in pairs]
                    uv_t = [cat0(u[p], e[p]["vst_f"]).T.astype(BF16) for p in pairs]
                    for p in pairs:
                        s_cur[p] = s[p] * e[p]["w_end"] + mm(uv_t[p], cat0(e[p]["btl"], e[p]["ktl"]))
                    y = [mm_nt(e[p]["rst"], sb[p])
                         + mm(mr_ref[slot, j, p], cat0(u[p].astype(BF16), e[p]["vst"])) for p in pairs]
                    done.append((rows, e, y))
                todo.append((first, second))

        def run_state_piece(state):
            idx, pending = state
            if idx >= len(todo):
                return state
            if pending is None:
                return idx, todo[idx][0]()
            todo[idx][1](pending)
            return idx + 1, None

        cursor = (0, None)
        if g_pre is not None:
            chains = [(j, p) for j in range(grp) for p in pairs]
            ins_p = [load_rows(g_pre * grp + j)[1] for j in range(grp)]
            cursor = run_state_piece(cursor)
            d = [pre_prep(ins_p[j], p) for j, p in chains]
            m4 = [mm_nt(cat0(x["ast"], x["rst"]), cat0(x["bst"], x["kst"])) for x in d]
            cursor = run_state_piece(cursor)
            nab = [jnp.where(strict, m[:c2, :c2], 0.0) for m in m4]
            mak = [jnp.where(strict, m[:c2, c2:], 0.0).astype(BF16) for m in m4]
            mrbk = [jnp.concatenate([jnp.where(incl, m[c2:, :c2], 0.0), jnp.where(incl, m[c2:, c2:], 0.0)],
                                    axis=1).astype(BF16) for m in m4]
            t = [eye + n for n in nab]
            pw = [n.astype(BF16) for n in nab]
            for _ in range(5):
                pw = [mm(x, x).astype(BF16) for x in pw]
                cursor = run_state_piece(cursor)
                t = [tt + mm(x, tt.astype(BF16)) for x, tt in zip(pw, t)]
                cursor = run_state_piece(cursor)
            mv = [mm(mk, x["vst"]) for mk, x in zip(mak, d)]
            tx = [mm(tt.astype(BF16), jnp.concatenate([x["ast"], m.astype(BF16)], axis=1))
                  for tt, x, m in zip(t, d, mv)]
        while cursor[0] < len(todo):
            cursor = run_state_piece(cursor)
        if g_state is not None:
            for rows, e, y in done:
                yp = [x[:CHUNK, :] + x[CHUNK:, :] for x in y]
                mu = [_dot_hl(x, emean) for x in yp]
                yc = [x - m for x, m in zip(yp, mu)]
                var = [_dot_hl(x * x, emean) for x in yc]
                bonus = [_dot_hl(x["r"] * x["k"] * r_k[:, x["lanes"]], eones) * x["v"] for x in e]
                outs = [(yc[p] * lax.rsqrt(var[p] + LNX_EPS) * lnx_g[:, e[p]["lanes"]] + lnx_b[:, e[p]["lanes"]]
                         + bonus[p]) * e[p]["g"] for p in pairs]
                o_ref[rows, :] = jnp.concatenate(outs, axis=1)
            for q in range(n_seq):
                for p in pairs:
                    s_ref[q, p] = s_all[q][p]
        if g_pre is not None:
            for (j, p), x, txx, mr in zip(chains, d, tx, mrbk):
                ta_ref[1 - slot, j, p] = txx[:, :PAIR].astype(BF16)
                tv_ref[1 - slot, j, p] = txx[:, PAIR:]
                mr_ref[1 - slot, j, p] = mr
                cl_ref[1 - slot, j, p] = x["cl"]

    step(None, 0, 1)

    def body(gi, carry):
        step(gi, gi + 1, lax.rem(gi, 2))
        return carry

    lax.fori_loop(0, n_grp - 1, body, 0)
    step(n_grp - 1, None, (n_grp - 1) % 2)
    for q in range(n_seq):
        for p in range(n_pairs):
            sfin_ref[q, 2 * p] = s_ref[q, p, :HEAD_DIM, :HEAD_DIM]
            sfin_ref[q, 2 * p + 1] = s_ref[q, p, HEAD_DIM:, HEAD_DIM:]


def rwkv_scan(r, lw, k, a, b, v, g, s0, r_k, lnx_g, lnx_b, batch, seq):
    n, d_r = r.shape
    n_pairs = d_r // PAIR
    if seq > CHUNK:
        n_seq, rows = 1, min(SCAN_ROWS, seq)
    else:
        n_seq = next(q for q in (SCAN_GROUP, 2, 1) if batch % q == 0)
        rows = n_seq * seq
    steps = max(seq // rows, 1)
    grp = _scan_group(rows // CHUNK)
    tri = jnp.tril(jnp.ones((CHUNK, CHUNK), F32)).astype(BF16)
    blk = jnp.arange(PAIR) // HEAD_DIM
    same = (blk[:, None] == blk[None, :]).astype(F32)
    emean = (same / HEAD_DIM).astype(BF16)
    eones = same.astype(BF16)
    tok = pl.BlockSpec((rows, d_r), lambda bi, i: (bi * steps + i, 0))
    par = pl.BlockSpec((1, d_r), lambda bi, i: (0, 0))
    st = pl.BlockSpec((n_seq, 2 * n_pairs, HEAD_DIM, HEAD_DIM), lambda bi, i: (bi, 0, 0, 0))
    return pl.pallas_call(
        functools.partial(_rwkv_scan_kernel, n_chunks=rows // CHUNK, n_pairs=n_pairs, n_seq=n_seq),
        grid=(batch // n_seq, steps),
        in_specs=[tok] * 7 + [st, par, par, par, _full_spec(tri, 2), _full_spec(emean, 2), _full_spec(eones, 2)],
        out_specs=[tok, st],
        out_shape=[jax.ShapeDtypeStruct((n, d_r), F32),
                   jax.ShapeDtypeStruct((batch, 2 * n_pairs, HEAD_DIM, HEAD_DIM), F32)],
        scratch_shapes=[pltpu.VMEM((n_seq, n_pairs, PAIR, PAIR), F32),
                        pltpu.VMEM((2, grp, n_pairs, PAIR, PAIR), BF16),
                        pltpu.VMEM((2, grp, n_pairs, PAIR, PAIR), F32),
                        pltpu.VMEM((2, grp, n_pairs, PAIR, 2 * PAIR), BF16),
                        pltpu.VMEM((2, grp, n_pairs, CHUNK, PAIR), F32)],
        compiler_params=pltpu.CompilerParams(
            dimension_semantics=("arbitrary", "arbitrary"), vmem_limit_bytes=VMEM_LIMIT),
        name="rwkv_scan",
    )(r, lw, k, a, b, v, g, s0, r_k.reshape(1, d_r), lnx_g.reshape(1, d_r), lnx_b.reshape(1, d_r),
      tri, emean, eones)


def _sb_prompt_kernel(bias_ref, q_ref, k_ref, v_ref, u_ref, o_ref, acc_ref, c_ref, *, tq, tk, n_hp):
    g = pl.program_id(1)
    i = pl.program_id(2)
    n_sub = tq // tk
    n_heads = 2 * n_hp
    lane = lax.broadcasted_iota(jnp.int32, (1, PAIR), 1)
    m0 = lane < HEAD_DIM
    pair_lanes = [slice((h // 2) * PAIR, (h // 2 + 1) * PAIR) for h in range(n_heads)]
    q_heads = []
    for h in range(n_heads):
        qp = q_ref[:, pair_lanes[h]]
        qh = jnp.where(m0 if h % 2 == 0 else ~m0, qp, jnp.zeros_like(qp))
        bias = jnp.full((1, PAIR), bias_ref[n_heads * g + h], F32)
        b1 = bias.astype(BF16).astype(F32)
        b2 = (bias - b1).astype(BF16).astype(F32)
        b3 = (bias - b1) - b2
        ext = jnp.where(lane == 0, b1, jnp.where(lane == 1, b2, jnp.where(lane == 2, b3, 0.0)))
        q_heads.append(jnp.concatenate([qh, jnp.broadcast_to(ext, qh.shape).astype(BF16)], axis=1))
    k_ext = {nk: jnp.broadcast_to(jnp.where(lane < 3, 1.0, 0.0), (nk, PAIR)).astype(BF16)
             for nk in range(tk, tq + 1, tk)}
    acc_ref[...] = jnp.zeros_like(acc_ref)
    c_ref[...] = jnp.zeros_like(c_ref)
    u = u_ref[...]
    rr = lax.broadcasted_iota(jnp.int32, (tk, tk), 0)
    cc = lax.broadcasted_iota(jnp.int32, (tk, tk), 1)
    earlier = cc < rr
    tile = lambda x: jnp.concatenate([x] * (tk // PAIR), axis=1)

    def logits(kb, q0, nq_rows, n_keys_sub):
        nk = n_keys_sub * tk
        rows = pl.ds(pl.multiple_of(kb * tq, tq), nk)
        ks = [jnp.concatenate([k_ref[rows, pair_lanes[h]], k_ext[nk]], axis=1) for h in range(n_heads)]
        return [lax.dot_general(qh[q0:q0 + nq_rows], kh, (((1,), (1,)), ((), ())), preferred_element_type=F32)
                for qh, kh in zip(q_heads, ks)]

    def attend(zs, kb, q0, nq_rows, n_keys_sub, mask_last):
        qrows = slice(q0, q0 + nq_rows)
        nk = n_keys_sub * tk
        rows = pl.ds(pl.multiple_of(kb * tq, tq), nk)
        subs = [slice(j * tk, (j + 1) * tk) for j in range(n_keys_sub)]
        vs = [v_ref[rows, pair_lanes[h]] for h in range(n_heads)]
        ns = [jnp.maximum(z, 0.0) + jnp.log(1.0 + jnp.exp(_neg_abs(z))) for z in zs]
        masks =[earlier if (mask_last and j == n_keys_sub - 1) else None for j in range(n_keys_sub)]
        nm = [[n[:, s] if m is None else jnp.where(m, n[:, s], 0.0) for s, m in zip(subs, masks)] for n in ns]
        cums = [[jnp.dot(x.astype(BF16), u, preferred_element_type=F32) for x in nh] for nh in nm]
        tots = [[jnp.broadcast_to(cum[:, :1] + x[:, :1], (nq_rows, PAIR)) for cum, x in zip(cm, nh)]
                for cm, nh in zip(cums, nm)]
        ws = []
        for h in range(n_heads):
            run = c_ref[h, qrows]
            parts = [None] * n_keys_sub
            for j in range(n_keys_sub - 1, -1, -1):
                parts[j] = cums[h][j] + tile(run)
                run = run + tots[h][j]
            c_ref[h, qrows] = run
            w = jnp.exp((zs[h] - ns[h]) - jnp.concatenate(parts, axis=1))
            if mask_last:
                w = jnp.concatenate([w[:, s] if m is None else jnp.where(m, w[:, s], 0.0)
                                     for s, m in zip(subs, masks)], axis=1)
            ws.append(w)
        for h in range(n_heads):
            acc_ref[h, qrows] += jnp.dot(ws[h].astype(BF16), vs[h], preferred_element_type=F32)

    for r in range(n_sub):
        attend(logits(i, r * tk, tk, r + 1), i, r * tk, tk, r + 1, True)

    def body(it, carry):
        kb = i - 1 - it
        attend(logits(kb, 0, tq, n_sub), kb, 0, tq, n_sub, False)
        return carry

    lax.fori_loop(0, i, body, 0)
    o_ref[...] = jnp.concatenate(
        [jnp.where(m0, acc_ref[2 * hp], acc_ref[2 * hp + 1]) for hp in range(n_hp)], axis=1)


def sb_prompt(q_bf, k_bf, v_bf, bias, batch, seq):
    n, d_s = q_bf.shape
    n_hp = SB_PAIRS
    width = n_hp * PAIR
    n_grp = d_s // width
    tq = min(SB_TILE, seq)
    tk = min(SB_KEYS, seq)
    nq = seq // tq
    j = jnp.arange(tk)
    ux = (j[:, None] > j[None, :]).astype(BF16)
    grid_spec = pltpu.PrefetchScalarGridSpec(
        num_scalar_prefetch=1,
        grid=(batch, n_grp, nq),
        in_specs=[
            pl.BlockSpec((tq, width), lambda b, p, i, bias_ref: (b * nq + i, p)),
            pl.BlockSpec((seq, width), lambda b, p, i, bias_ref: (b, p)),
            pl.BlockSpec((seq, width), lambda b, p, i, bias_ref: (b, p)),
            pl.BlockSpec(ux.shape, lambda b, p, i, bias_ref: (0, 0)),
        ],
        out_specs=pl.BlockSpec((tq, width), lambda b, p, i, bias_ref: (b * nq + i, p)),
        scratch_shapes=[pltpu.VMEM((2 * n_hp, tq, PAIR), F32)] * 2,
    )
    return pl.pallas_call(
        functools.partial(_sb_prompt_kernel, tq=tq, tk=tk, n_hp=n_hp),
        grid_spec=grid_spec,
        out_shape=jax.ShapeDtypeStruct((n, d_s), F32),
        compiler_params=pltpu.CompilerParams(
            dimension_semantics=("arbitrary", "arbitrary", "arbitrary"), vmem_limit_bytes=VMEM_LIMIT),
        name="sb_prompt",
    )(bias, q_bf, k_bf, v_bf, ux)


def _sb_sample_kernel(pt_ref, q_ref, brow_ref, kn_ref, vn_ref, ck_hbm, cv_hbm, ux_ref, o_ref,
                      acc_ref, c_ref, kbuf, vbuf, sem, *, n_pg, n_heads, n_pages, depth):
    nbuf = depth + 1
    steps = pl.num_programs(1)
    total = pl.num_programs(0) * steps
    j = pl.program_id(1)
    t = pl.program_id(0) * steps + j

    def page_copies(step, slot):
        bb = step // steps
        base = n_pages - n_pg * (step - bb * steps + 1)
        cps = []
        for i in range(n_pg):
            page = pt_ref[bb, base + i]
            cps.append(pltpu.make_async_copy(ck_hbm.at[page], kbuf.at[slot, i], sem.at[slot]))
            cps.append(pltpu.make_async_copy(cv_hbm.at[page], vbuf.at[slot, i], sem.at[slot]))
        return cps

    @pl.when(t == 0)
    def _():
        for d in range(depth):
            for cp in page_copies(d, d):
                cp.start()

    slot = lax.rem(t, nbuf)
    for cp in page_copies(t, slot):
        cp.wait()
    kps = [kbuf.at[slot, i] for i in range(n_pg)]
    vps = [vbuf.at[slot, i] for i in range(n_pg)]
    qbd = q_ref[...]
    nrow = qbd.shape[0]
    brow = brow_ref[...]
    ux = ux_ref[...]

    def log_terms(z, valid):
        nb = z.shape[1] // PAIR
        z = z + jnp.concatenate([brow] * nb, axis=1)
        sp = _softplus(z)
        l = -sp
        if valid is not None:
            l = jnp.where(valid, l, 0.0)
        lst = jnp.concatenate([l[:, bi * PAIR:(bi + 1) * PAIR] for bi in range(nb)], axis=0)
        return z - sp, _dot_hl(lst, ux)

    def weights(la, cx, run, valid):
        nb = la.shape[1] // PAIR
        parts = [None] * nb
        for bi in range(nb - 1, -1, -1):
            blk = cx[bi * nrow:(bi + 1) * nrow, :]
            parts[bi] = blk[:, :PAIR] + run
            run = run + blk[:, PAIR:]
        w = jnp.exp(la + jnp.concatenate(parts, axis=1))
        if valid is not None:
            w = jnp.where(valid, w, 0.0)
        return w.astype(BF16), run

    @pl.when(j == 0)
    def _():
        t_of_row = lax.broadcasted_iota(jnp.int32, (nrow, PAIR), 0) // n_heads
        key = lax.broadcasted_iota(jnp.int32, (nrow, PAIR), 1)
        valid = key < t_of_row
        la, cx = log_terms(_dot_nt(qbd, kn_ref[...]), valid)
        w, run = weights(la, cx, jnp.zeros((nrow, PAIR), F32), valid)
        acc_ref[...] = _dot(w, vn_ref[...])
        c_ref[...] = run

    d_s = qbd.shape[1]
    page_t = lambda r: r[...].reshape(d_s, PAIR).astype(BF16)
    n_grp = 2 if n_pg % 2 == 0 else 1
    per = n_pg // n_grp
    groups = [range(gi * per, (gi + 1) * per) for gi in range(n_grp)]
    zs = [jnp.concatenate([_dot(qbd, page_t(kps[pi])) for pi in grp], axis=1) for grp in groups]
    terms = [log_terms(z, None) for z in zs]
    run = c_ref[...]
    ws = [None] * n_grp
    for gi in range(n_grp - 1, -1, -1):
        ws[gi], run = weights(terms[gi][0], terms[gi][1], run, None)
    c_ref[...] = run
    out = None
    for grp, w in zip(groups, ws):
        for li, pi in enumerate(grp):
            o = _dot_nt(w[:, li * PAIR:(li + 1) * PAIR], page_t(vps[pi]))
            out = o if out is None else out + o
    acc_ref[...] += out

    nxt = t + depth
    for cp in page_copies(lax.rem(nxt, total), lax.rem(nxt, nbuf)):
        cp.start()

    @pl.when(j == steps - 1)
    def _():
        d_s = acc_ref.shape[1]
        head_of_row = lax.broadcasted_iota(jnp.int32, (nrow, d_s), 0) % n_heads
        head_of_lane = lax.broadcasted_iota(jnp.int32, (nrow, d_s), 1) // HEAD_DIM
        sel = jnp.where(head_of_row == head_of_lane, acc_ref[...], 0.0)
        o_ref[...] = jnp.sum(sel.reshape(nrow // n_heads, n_heads, d_s), axis=1)

    @pl.when(t == total - 1)
    def _():
        for d in range(1, depth + 1):
            for cp in page_copies(lax.rem(t + d, total), lax.rem(t + d, nbuf)):
                cp.wait()


def sb_sample(q_bd, brow, k_new, v_new, cache_k, cache_v, page_table, n_heads):
    bsz, nrow, d_s = q_bd.shape
    n_pages = page_table.shape[1]
    page = cache_k.shape[3]
    assert page == PAIR
    n_pg = min(SB_PAGES, n_pages)
    steps = n_pages // n_pg
    jj = jnp.arange(PAIR)
    ux = jnp.concatenate([(jj[:, None] > jj[None, :]).astype(F32), jnp.ones((PAIR, PAIR), F32)], axis=1).astype(BF16)

    depth = SB_PREFETCH
    assert bsz * steps >= depth
    per_b = lambda shape: pl.BlockSpec((None,) + shape, lambda b, j, pt: (b, 0, 0))
    page_buf = pltpu.VMEM((depth + 1, n_pg, n_heads, HEAD_DIM, page), F32)
    grid_spec = pltpu.PrefetchScalarGridSpec(
        num_scalar_prefetch=1,
        grid=(bsz, steps),
        in_specs=[per_b((nrow, d_s)), pl.BlockSpec(brow.shape, lambda b, j, pt: (0, 0)),
                  per_b((PAIR, d_s)), per_b((PAIR, d_s)),
                  pl.BlockSpec(memory_space=pl.ANY), pl.BlockSpec(memory_space=pl.ANY),
                  pl.BlockSpec(ux.shape, lambda b, j, pt: (0, 0))],
        out_specs=per_b((nrow // n_heads, d_s)),
        scratch_shapes=[pltpu.VMEM((nrow, d_s), F32), pltpu.VMEM((nrow, PAIR), F32), page_buf, page_buf,
                        pltpu.SemaphoreType.DMA((depth + 1,))],
    )
    return pl.pallas_call(
        functools.partial(_sb_sample_kernel, n_pg=n_pg, n_heads=n_heads, n_pages=n_pages, depth=depth),
        grid_spec=grid_spec,
        out_shape=jax.ShapeDtypeStruct((bsz, nrow // n_heads, d_s), F32),
        compiler_params=pltpu.CompilerParams(
            dimension_semantics=("arbitrary", "arbitrary"), vmem_limit_bytes=VMEM_LIMIT),
        name="sb_sample",
    )(page_table, q_bd, brow, k_new, v_new, cache_k, cache_v, ux)


def _merge_kernel(h_ref, oa_ref, ob_ref, wg_ref, wba_ref, wbb_ref, wo_ref, g_ref, b_ref, o_ref, *, alpha):
    h = h_ref[...]
    d = h.shape[1]
    gates = jax.nn.sigmoid(jnp.dot(h.astype(BF16), wg_ref[...], preferred_element_type=F32))
    ma = jnp.dot(oa_ref[...].astype(BF16), wba_ref[...], preferred_element_type=F32)
    mb = jnp.dot(ob_ref[...].astype(BF16), wbb_ref[...], preferred_element_type=F32)
    merged = gates[:, :d] * ma + gates[:, d:] * mb
    y = alpha * h + jnp.dot(merged.astype(BF16), wo_ref[...], preferred_element_type=F32)
    o_ref[...] = _layer_norm(y, g_ref[...], b_ref[...])


def merge(h, o_a, o_b, w_gate_bf, w_ba_bf, w_bb_bf, w_out_bf, g, b, alpha):
    n, d = h.shape
    tm = min(MERGE_ROWS, n)
    row = lambda a: pl.BlockSpec((tm, a.shape[1]), lambda i: (i, 0))
    ws = [w_gate_bf, w_ba_bf, w_bb_bf, w_out_bf, g.reshape(1, d), b.reshape(1, d)]
    const = lambda a: pl.BlockSpec(a.shape, lambda i: (0,) * a.ndim, pipeline_mode=pl.Buffered(1))
    return pl.pallas_call(
        functools.partial(_merge_kernel, alpha=alpha),
        grid=(n // tm,),
        in_specs=[row(h), row(o_a), row(o_b)] + [const(a) for a in ws],
        out_specs=pl.BlockSpec((tm, d), lambda i: (i, 0)),
        out_shape=jax.ShapeDtypeStruct((n, d), F32),
        compiler_params=pltpu.CompilerParams(
            dimension_semantics=("arbitrary",), vmem_limit_bytes=VMEM_LIMIT),
        name="merge",
    )(h, o_a, o_b, *ws)


def _layer(x, batch, seq, h_last, wkv0, sb_fn, w, alpha):
    n, d = x.shape
    h = ffn_ln(x, w["ffn1_in"], w["ffn1_out"], w["ln1_g"], w["ln1_b"], alpha)
    if h_last is None:
        hl_rows = None
        rows_pad = seq
    else:
        hl_rows = jnp.repeat(h_last, seq, axis=0)
        rows_pad = CHUNK
    r, lw, k, a, b, v, g, q_bf, k_sb, v_sb, k_bf, v_bf = mixer_prep(h, hl_rows, batch, seq, w)
    n_h = w["w0"].shape[-1] // HEAD_DIM
    if wkv0 is None:
        s0 = jnp.zeros((batch, n_h, HEAD_DIM, HEAD_DIM), F32)
    else:
        s0 = wkv0.astype(F32)
    scan_in = [r, lw, k, a, b, v, g]
    if rows_pad != seq:
        pad = lambda t: jnp.pad(t.reshape(batch, seq, -1), ((0, 0), (0, rows_pad - seq), (0, 0))).reshape(
            batch * rows_pad, -1)
        scan_in = [pad(t) for t in scan_in]
    o_a, wkv = rwkv_scan(*scan_in, s0, w["r_k"], w["lnx_g"], w["lnx_b"], batch, rows_pad)
    if rows_pad != seq:
        o_a = o_a.reshape(batch, rows_pad, -1)[:, :seq].reshape(n, -1)
    o_b = sb_fn(q_bf, k_sb, v_sb, k_bf, v_bf)
    x2 = merge(h, o_a, o_b, w["w_gate"], w["w_ba"], w["w_bb"], w["w_out"], w["ln2_g"], w["ln2_b"], alpha)
    x3 = ffn_ln(x2, w["ffn2_in"], w["ffn2_out"], w["ln3_g"], w["ln3_b"], alpha)
    return x3, k_sb, v_sb, wkv, h.reshape(batch, seq, d)[:, -1]


def kernel(x_prompt, x_sample, cache_k, cache_v, state_wkv, state_shift, page_table, ln1_g, ln1_b, ffn1_w_in, ffn1_w_out, w_in, mu_rkv, mu_wag, w0, w_w1, w_w2, a0, a_w1, a_w2, g_w1, g_w2, k_k, k_a, r_k, lnx_g, lnx_b, sb_bias, w_branch, w_out, ln2_g, ln2_b, ffn2_w_in, ffn2_w_out, ln3_g, ln3_b):
    depth = ln1_g.shape[0]
    alpha = (2.0 * depth) ** 0.25
    bp, tp, d = x_prompt.shape
    bs, ts, _ = x_sample.shape
    d_r = w0.shape[-1]
    d_s = (w_in.shape[-1] - 3 * d_r - 2 * d) // 3
    n_hs = d_s // HEAD_DIM
    blk = jnp.arange(d_r) // HEAD_DIM
    e_head = (blk[:, None] == blk[None, :]).astype(BF16)

    xp = x_prompt.reshape(bp * tp, d)
    xs = x_sample.reshape(bs * ts, d)
    outs = [[] for _ in range(8)]
    for l in range(depth):
        row = lambda a: a[l].reshape(1, -1)
        w = dict(
            ffn1_in=ffn1_w_in[l].astype(BF16), ffn1_out=ffn1_w_out[l].astype(BF16),
            ffn2_in=ffn2_w_in[l].astype(BF16), ffn2_out=ffn2_w_out[l].astype(BF16),
            ln1_g=ln1_g[l], ln1_b=ln1_b[l], ln2_g=ln2_g[l], ln2_b=ln2_b[l], ln3_g=ln3_g[l], ln3_b=ln3_b[l],
            w_rkv=w_in[l][:, :3 * d_r].astype(BF16),
            w_sb=w_in[l][:, 3 * d_r:3 * d_r + 3 * d_s].astype(BF16),
            w_gate=w_in[l][:, 3 * d_r + 3 * d_s:].astype(BF16),
            w_w1=w_w1[l].astype(BF16), w_w2=w_w2[l].astype(BF16), a_w1=a_w1[l].astype(BF16),
            a_w2=a_w2[l].astype(BF16), g_w1=g_w1[l].astype(BF16), g_w2=g_w2[l].astype(BF16),
            mu_rkv=row(mu_rkv), mu_wag=mu_wag[l], w0=row(w0), a0=row(a0), k_k=row(k_k), k_a=row(k_a),
            e_head=e_head, r_k=r_k[l].reshape(-1), lnx_g=lnx_g[l], lnx_b=lnx_b[l],
            w_ba=w_branch[l][:d_r].astype(BF16), w_bb=w_branch[l][d_r:].astype(BF16),
            w_out=w_out[l].astype(BF16),
        )
        bias = sb_bias[l].astype(F32)

        def prompt_sb(q_bf, k_sb, v_sb, k_bf, v_bf):
            return sb_prompt(q_bf, k_bf, v_bf, bias, bp, tp)

        def sample_sb(q_bf, k_sb, v_sb, k_bf, v_bf):
            head_of_lane = jnp.arange(d_s) // HEAD_DIM
            onehot = (jnp.arange(n_hs)[:, None] == head_of_lane[None, :])
            q_bd = jnp.where(onehot[None, None], q_bf.reshape(bs, ts, 1, d_s), jnp.zeros((), BF16))
            q_bd = q_bd.reshape(bs, ts * n_hs, d_s)
            brow = jnp.broadcast_to(jnp.tile(bias, ts)[:, None], (ts * n_hs, PAIR))
            padk = lambda t: jnp.pad(t.reshape(bs, ts, d_s), ((0, 0), (0, PAIR - ts), (0, 0)))
            ck = jnp.transpose(cache_k[l], (0, 2, 3, 1))
            cv = jnp.transpose(cache_v[l], (0, 2, 3, 1))
            o = sb_sample(q_bd, brow, padk(k_sb), padk(v_sb), ck, cv, page_table, n_hs)
            return o.reshape(bs * ts, d_s)

        xp, kp, vp, wp, hp = _layer(xp, bp, tp, None, None, prompt_sb, w, alpha)
        xs, ks_, vs_, ws_, hs_ = _layer(xs, bs, ts, state_shift[l].astype(F32), state_wkv[l], sample_sb, w, alpha)
        for lst, val in zip(outs, (jnp.transpose(kp, (0, 3, 1, 2)), jnp.transpose(vp, (0, 3, 1, 2)), wp, hp,
                                   ks_.reshape(bs, ts, n_hs, HEAD_DIM), vs_.reshape(bs, ts, n_hs, HEAD_DIM),
                                   ws_.astype(state_wkv.dtype), hs_)):
            lst.append(val)
    return (xp.reshape(bp, tp, d), xs.reshape(bs, ts, d)) + tuple(jnp.stack(o) for o in outs)
```

```python
import functools

import jax
import jax.numpy as jnp
from jax import lax
from jax.experimental import pallas as pl
from jax.experimental.pallas import tpu as pltpu

F32 = jnp.float32
BF16 = jnp.bfloat16

HEAD_DIM = 64
PAIR = 2 * HEAD_DIM
CHUNK = 64
LN_EPS = 1e-5
LNX_EPS = 64e-5
SB_SCALE = HEAD_DIM ** -0.5

ROW_TILE = 512
FFN_ROWS = 1024
MERGE_ROWS = 1024
FFN_SUB = 512
FFN_RESIDENT_BYTES = 20 * 1024 * 1024
SCAN_ROWS = 1024
SCAN_GROUP = 4
SB_TILE = 512
SB_KEYS = 256
SB_PAIRS = 2
SB_PAGES = 16
SB_PREFETCH = 3
VMEM_LIMIT = 56 * 1024 * 1024


def _dot(a, b):
    return jnp.dot(a.astype(BF16), b.astype(BF16), preferred_element_type=F32)


def _dot_nt(a, b):
    return lax.dot_general(a.astype(BF16), b.astype(BF16), (((1,), (1,)), ((), ())),
                           preferred_element_type=F32)


def _split(x):
    hi = x.astype(BF16)
    lo = (x - hi.astype(F32)).astype(BF16)
    return hi, lo


def _dot_hl(a, b_exact):
    hi, lo = _split(a)
    return (jnp.dot(hi, b_exact, preferred_element_type=F32)
            + jnp.dot(lo, b_exact, preferred_element_type=F32))


def _dot3(a, b, nt=False):
    ah, al = _split(a)
    bh, bl = _split(b)
    if nt:
        d = lambda x, y: lax.dot_general(x, y, (((1,), (1,)), ((), ())), preferred_element_type=F32)
    else:
        d = lambda x, y: jnp.dot(x, y, preferred_element_type=F32)
    return d(ah, bh) + d(al, bh) + d(ah, bl)


LOG2E = 1.4426950408889634


def _neg_abs(x):
    bits = pltpu.bitcast(x, jnp.uint32) | jnp.uint32(0x80000000)
    return pltpu.bitcast(bits, F32)


def _softplus(u):
    return jnp.maximum(u, 0.0) + jnp.log1p(jnp.exp(-jnp.abs(u)))


def _layer_norm(y, g, b):
    mu = jnp.mean(y, axis=-1, keepdims=True)
    yc = y - mu
    var = jnp.mean(yc * yc, axis=-1, keepdims=True)
    return yc * lax.rsqrt(var + LN_EPS) * g + b


def _full_spec(a, grid_rank):
    nd = a.ndim
    if grid_rank == 1:
        return pl.BlockSpec(a.shape, lambda i: (0,) * nd)
    if grid_rank == 2:
        return pl.BlockSpec(a.shape, lambda i, j: (0,) * nd)
    return pl.BlockSpec(a.shape, lambda i, j, k: (0,) * nd)


def _ffn_ln_kernel(x_ref, wg_ref, wu_ref, wo_ref, g_ref, b_ref, o_ref, acc_ref, *, alpha, n_ff):
    j = pl.program_id(1)

    xb = x_ref[...].astype(BF16)
    tf = wg_ref.shape[1]
    cuts = list(range(0, tf, FFN_SUB)) + [tf]
    cols = [slice(a, b) for a, b in zip(cuts[:-1], cuts[1:])]
    gu = [(jnp.dot(xb, wg_ref[:, c], preferred_element_type=F32),
           jnp.dot(xb, wu_ref[:, c], preferred_element_type=F32)) for c in cols]
    mids = [(gate * jax.nn.sigmoid(gate) * up).astype(BF16) for gate, up in gu]
    out = jnp.dot(mids[0], wo_ref[cols[0], :], preferred_element_type=F32)
    for mid, c in zip(mids[1:], cols[1:]):
        out = out + jnp.dot(mid, wo_ref[c, :], preferred_element_type=F32)
    if n_ff == 1:
        o_ref[...] = _layer_norm(alpha * x_ref[...] + 0.5 * out, g_ref[...], b_ref[...])
        return

    @pl.when(j == 0)
    def _():
        acc_ref[...] = out

    @pl.when(j > 0)
    def _():
        acc_ref[...] += out

    @pl.when(j == n_ff - 1)
    def _():
        y = alpha * x_ref[...] + 0.5 * acc_ref[...]
        o_ref[...] = _layer_norm(y, g_ref[...], b_ref[...])


def _ffn_tile(d, d_ff):
    if 3 * d * d_ff * 2 <= FFN_RESIDENT_BYTES:
        return d_ff
    for n in (2, 11, 22):
        if d_ff % n == 0 and (d_ff // n) % 128 == 0:
            return d_ff // n
    return d_ff


def ffn_ln(x, w_in_bf, w_out_bf, g, b, alpha):
    n, d = x.shape
    d_ff = w_out_bf.shape[0]
    tm = min(FFN_ROWS, n)
    tf = _ffn_tile(d, d_ff)
    n_ff = d_ff // tf
    wmode = dict(pipeline_mode=pl.Buffered(1)) if n_ff == 1 else {}
    return pl.pallas_call(
        functools.partial(_ffn_ln_kernel, alpha=alpha, n_ff=n_ff),
        grid=(n // tm, n_ff),
        in_specs=[
            pl.BlockSpec((tm, d), lambda i, j: (i, 0)),
            pl.BlockSpec((d, tf), lambda i, j: (0, j), **wmode),
            pl.BlockSpec((d, tf), lambda i, j: (0, j + n_ff), **wmode),
            pl.BlockSpec((tf, d), lambda i, j: (j, 0), **wmode),
            pl.BlockSpec((1, d), lambda i, j: (0, 0)),
            pl.BlockSpec((1, d), lambda i, j: (0, 0)),
        ],
        out_specs=pl.BlockSpec((tm, d), lambda i, j: (i, 0)),
        out_shape=jax.ShapeDtypeStruct((n, d), F32),
        scratch_shapes=[pltpu.VMEM((tm, d), F32)],
        compiler_params=pltpu.CompilerParams(
            dimension_semantics=("arbitrary", "arbitrary"), vmem_limit_bytes=VMEM_LIMIT),
        name="ffn_ln",
    )(x, w_in_bf, w_in_bf, w_out_bf, g.reshape(1, d), b.reshape(1, d))


N_PREP_PARAMS = 15
N_PREP_OUTS = 12


def _rwkv_prep_math(h, h_prev, p, p_prev, params, outs, head_major):
    (wrkv_ref, ww1_ref, ww2_ref, aw1_ref, aw2_ref, gw1_ref, gw2_ref,
     mu_rkv_ref, mu_wag_ref, w0_ref, a0_ref, kk_ref, ka_ref, e_ref, wsb_ref) = params
    r_out, lw_out, k_out, a_out, b_out, v_out, g_out, q_sb, k_sb, v_sb, kb_sb, vb_sb = outs
    d_r = w0_ref.shape[-1]
    d_s = kb_sb.shape[-1]
    psb = jnp.dot(h.astype(BF16), wsb_ref[...], preferred_element_type=F32)
    q_sb[...] = (psb[:, :d_s] * SB_SCALE).astype(BF16)
    ksb = psb[:, d_s:2 * d_s]
    vsb = psb[:, 2 * d_s:]
    if head_major:
        k_sb[...] = ksb.T.reshape(k_sb.shape)
        v_sb[...] = vsb.T.reshape(v_sb.shape)
    else:
        k_sb[...] = ksb
        v_sb[...] = vsb
    kb_sb[...] = ksb.astype(BF16)
    vb_sb[...] = vsb.astype(BF16)
    rkv = p + (p_prev - p) * mu_rkv_ref[...]
    r = rkv[:, :d_r]
    k = rkv[:, d_r:2 * d_r]
    v = rkv[:, 2 * d_r:]
    dx = h_prev - h
    xw = h + dx * mu_wag_ref[0:1, :]
    xa = h + dx * mu_wag_ref[1:2, :]
    xg = h + dx * mu_wag_ref[2:3, :]
    lw = _dot(jnp.tanh(_dot(xw, ww1_ref[...])), ww2_ref[...])
    w_log = -_softplus(-(w0_ref[...] + lw)) - 0.5
    a_gate = jax.nn.sigmoid(a0_ref[...] + _dot(_dot(xa, aw1_ref[...]), aw2_ref[...]))
    g = _dot(jax.nn.sigmoid(_dot(xg, gw1_ref[...])), gw2_ref[...])
    kk = k * kk_ref[...]
    ss = _dot_hl(kk * kk, e_ref[...])
    kk = kk / jnp.maximum(jnp.sqrt(ss), 1e-12)
    r_out[...] = r
    lw_out[...] = -jnp.exp(w_log)
    k_out[...] = k * (1.0 + (a_gate - 1.0) * ka_ref[...])
    a_out[...] = -kk
    b_out[...] = kk * a_gate
    v_out[...] = v
    g_out[...] = g


def _rwkv_prep_prompt_kernel(h_ref, *refs, tiles_per_seq):
    params, outs = refs[:N_PREP_PARAMS], refs[N_PREP_PARAMS:N_PREP_PARAMS + N_PREP_OUTS]
    hcarry_ref, pcarry_ref = refs[-2:]
    i = pl.program_id(0)
    h = h_ref[...]
    tm = h.shape[0]

    @pl.when(i % tiles_per_seq == 0)
    def _():
        hcarry_ref[...] = jnp.zeros_like(hcarry_ref)
        pcarry_ref[...] = jnp.zeros_like(pcarry_ref)

    row = lax.broadcasted_iota(jnp.int32, (tm, 1), 0)
    p = jnp.dot(h.astype(BF16), params[0][...], preferred_element_type=F32)
    h_prev = jnp.where(row == 0, hcarry_ref[7:8, :], pltpu.roll(h, 1, 0))
    p_prev = jnp.where(row == 0, pcarry_ref[7:8, :], pltpu.roll(p, 1, 0))
    _rwkv_prep_math(h, h_prev, p, p_prev, params, outs, True)
    hcarry_ref[...] = h[tm - 8:, :]
    pcarry_ref[...] = p[tm - 8:, :]


def _rwkv_prep_sample_kernel(h_ref, hl_ref, *refs, seq):
    params, outs = refs[:N_PREP_PARAMS], refs[N_PREP_PARAMS:N_PREP_PARAMS + N_PREP_OUTS]
    h = h_ref[...]
    tm = h.shape[0]
    row = lax.broadcasted_iota(jnp.int32, (tm, 1), 0)
    first = row % seq == 0
    p = jnp.dot(h.astype(BF16), params[0][...], preferred_element_type=F32)
    p_last = jnp.dot(hl_ref[...].astype(BF16), params[0][...], preferred_element_type=F32)
    h_prev = jnp.where(first, hl_ref[...], pltpu.roll(h, 1, 0))
    p_prev = jnp.where(first, p_last, pltpu.roll(p, 1, 0))
    _rwkv_prep_math(h, h_prev, p, p_prev, params, outs, False)


def mixer_prep(h, h_last_rows, batch, seq, prm):
    n, d = h.shape
    d_r = prm["w0"].shape[-1]
    d_s = prm["w_sb"].shape[1] // 3
    n_h = d_s // HEAD_DIM
    plist = [prm["w_rkv"], prm["w_w1"], prm["w_w2"], prm["a_w1"], prm["a_w2"], prm["g_w1"], prm["g_w2"],
             prm["mu_rkv"], prm["mu_wag"], prm["w0"], prm["a0"], prm["k_k"], prm["k_a"], prm["e_head"],
             prm["w_sb"]]
    assert len(plist) == N_PREP_PARAMS
    if h_last_rows is None:
        tm = min(ROW_TILE, seq)
        kern = functools.partial(_rwkv_prep_prompt_kernel, tiles_per_seq=seq // tm)
        args = [h] + plist
        in_specs = [pl.BlockSpec((tm, d), lambda i: (i, 0))] + [_full_spec(a, 1) for a in plist]
        scratch = [pltpu.VMEM((8, d), F32), pltpu.VMEM((8, 3 * d_r), F32)]
    else:
        tm = n
        kern = functools.partial(_rwkv_prep_sample_kernel, seq=seq)
        args = [h, h_last_rows] + plist
        in_specs = [pl.BlockSpec((tm, d), lambda i: (i, 0))] * 2 + [_full_spec(a, 1) for a in plist]
        scratch = []
    row_spec = lambda w: pl.BlockSpec((tm, w), lambda i: (i, 0))
    if h_last_rows is None:
        tps = seq // tm
        kv_spec = pl.BlockSpec((None, n_h, HEAD_DIM, tm), lambda i: (i // tps, 0, 0, i % tps))
        kv_shape = jax.ShapeDtypeStruct((batch, n_h, HEAD_DIM, seq), F32)
    else:
        kv_spec = row_spec(d_s)
        kv_shape = jax.ShapeDtypeStruct((n, d_s), F32)
    return pl.pallas_call(
        kern,
        grid=(n // tm,),
        in_specs=in_specs,
        out_specs=[row_spec(d_r)] * 7 + [row_spec(d_s), kv_spec, kv_spec, row_spec(d_s), row_spec(d_s)],
        out_shape=[jax.ShapeDtypeStruct((n, d_r), F32)] * 7
                  + [jax.ShapeDtypeStruct((n, d_s), BF16), kv_shape, kv_shape,
                     jax.ShapeDtypeStruct((n, d_s), BF16), jax.ShapeDtypeStruct((n, d_s), BF16)],
        scratch_shapes=scratch,
        compiler_params=pltpu.CompilerParams(
            dimension_semantics=("arbitrary",), vmem_limit_bytes=VMEM_LIMIT),
        name="mixer_prep",
    )(*args)


def _scan_group(n_chunks):
    for g in (SCAN_GROUP, 2, 1):
        if n_chunks % g == 0:
            return g


def _stack_heads(x, m0):
    return jnp.concatenate([jnp.where(m0, x, 0.0), jnp.where(m0, 0.0, x)], axis=0)


def _rwkv_scan_kernel(r_ref, lw_ref, k_ref, a_ref, b_ref, v_ref, g_ref, s0_ref,
                      rk_ref, lg_ref, lb_ref, tri_ref, emean_ref, eones_ref,
                      o_ref, sfin_ref, s_ref, ta_ref, tv_ref, mr_ref, cl_ref, *, n_chunks, n_pairs, n_seq):
    i = pl.program_id(1)
    c2 = 2 * CHUNK
    seq_chunks = n_chunks // n_seq

    @pl.when(i == 0)
    def _():
        zero = jnp.zeros((HEAD_DIM, HEAD_DIM), F32)
        for q in range(n_seq):
            for p in range(n_pairs):
                top = jnp.concatenate([s0_ref[q, 2 * p], zero], axis=1)
                bot = jnp.concatenate([zero, s0_ref[q, 2 * p + 1]], axis=1)
                s_ref[q, p] = jnp.concatenate([top, bot], axis=0)

    lane = lax.broadcasted_iota(jnp.int32, (1, PAIR), 1)
    m0 = lane < HEAD_DIM
    rr = lax.broadcasted_iota(jnp.int32, (c2, c2), 0)
    cc = lax.broadcasted_iota(jnp.int32, (c2, c2), 1)
    strict = cc < rr
    incl = cc <= rr
    eye = (cc == rr).astype(F32)
    tri = tri_ref[...]
    mm = lambda x, y: jnp.dot(x, y, preferred_element_type=F32)
    mm_nt = lambda x, y: lax.dot_general(x, y, (((1,), (1,)), ((), ())), preferred_element_type=F32)

    emean = emean_ref[...]
    eones = eones_ref[...]
    r_k = rk_ref[...]
    lnx_g = lg_ref[...]
    lnx_b = lb_ref[...]

    pairs = range(n_pairs)
    cat0 = lambda x, y: jnp.concatenate([x, y], axis=0)
    stack = lambda x: _stack_heads(x, m0).astype(BF16)
    in_refs = (r_ref, lw_ref, k_ref, a_ref, b_ref, v_ref, g_ref)

    def load_rows(c):
        rows = pl.ds(pl.multiple_of(c * CHUNK, CHUNK), CHUNK)
        return rows, [ref[rows, :] for ref in in_refs]

    def pre_prep(ins, p):
        lanes = slice(p * PAIR, (p + 1) * PAIR)
        r, lw, k, a, b, v, g = (x[:, lanes] for x in ins)
        hi, lo = _split(lw)
        lo2 = (lw - hi.astype(F32) - lo.astype(F32)).astype(BF16)
        cl = mm(tri, hi) + mm(tri, lo) + mm(tri, lo2)
        w_inv = jnp.exp(-cl)
        return dict(cl=cl, vst=stack(v), ast=stack(a * jnp.exp(cl - lw)), rst=stack(r * jnp.exp(cl)),
                    bst=stack(b * w_inv), kst=stack(k * w_inv))

    def state_prep(ins, cl, p):
        lanes = slice(p * PAIR, (p + 1) * PAIR)
        r, lw, k, a, b, v, g = (x[:, lanes] for x in ins)
        cl_end = cl[CHUNK - 1:CHUNK, :]
        w_tail = jnp.exp(cl_end - cl)
        vst_f = _stack_heads(v, m0)
        return dict(r=r, k=k, v=v, g=g, lanes=lanes, w_end=jnp.exp(cl_end), vst_f=vst_f, vst=vst_f.astype(BF16),
                    rst=stack(r * jnp.exp(cl)), btl=stack(b * w_tail), ktl=stack(k * w_tail))

    grp = _scan_group(n_chunks)
    n_grp = n_chunks // grp
    assert n_seq == 1 or n_grp == 1

    def step(g_state, g_pre, slot):
        todo = []
        if g_state is not None:
            seq_of = [0 if n_seq == 1 else j // seq_chunks for j in range(grp)]
            s_all = [[s_ref[q, p] for p in pairs] for q in range(n_seq)]
            done = []
            for j in range(grp):
                s_cur = s_all[seq_of[j]]

                def first(j=j, s_cur=s_cur):
                    rows, ins = load_rows(g_state * grp + j)
                    s = list(s_cur)
                    sb = [x.astype(BF16) for x in s]
                    u = [mm_nt(ta_ref[slot, j, p], sb[p]) + tv_ref[slot, j, p] for p in pairs]
                    return rows, ins, s, sb, u

                def second(st, j=j, s_cur=s_cur):
                    rows, ins, s, sb, u = st
                    e = [state_prep(ins, cl_ref[slot, j, p], p) for p in pairs]
                    uv_t = [cat0(u[p], e[p]["vst_f"]).T.astype(BF16) for p in pairs]
                    for p in pairs:
                        s_cur[p] = s[p] * e[p]["w_end"] + mm(uv_t[p], cat0(e[p]["btl"], e[p]["ktl"]))
                    y = [mm_nt(e[p]["rst"], sb[p])
                         + mm(mr_ref[slot, j, p], cat0(u[p].astype(BF16), e[p]["vst"])) for p in pairs]
                    done.append((rows, e, y))
                todo.append((first, second))

        def run_state_piece(state):
            idx, pending = state
            if idx >= len(todo):
                return state
            if pending is None:
                return idx, todo[idx][0]()
            todo[idx][1](pending)
            return idx + 1, None

        cursor = (0, None)
        if g_pre is not None:
            chains = [(j, p) for j in range(grp) for p in pairs]
            ins_p = [load_rows(g_pre * grp + j)[1] for j in range(grp)]
            cursor = run_state_piece(cursor)
            d = [pre_prep(ins_p[j], p) for j, p in chains]
            m4 = [mm_nt(cat0(x["ast"], x["rst"]), cat0(x["bst"], x["kst"])) for x in d]
            cursor = run_state_piece(cursor)
            nab = [jnp.where(strict, m[:c2, :c2], 0.0) for m in m4]
            mak = [jnp.where(strict, m[:c2, c2:], 0.0).astype(BF16) for m in m4]
            mrbk = [jnp.concatenate([jnp.where(incl, m[c2:, :c2], 0.0), jnp.where(incl, m[c2:, c2:], 0.0)],
                                    axis=1).astype(BF16) for m in m4]
            t = [eye + n for n in nab]
            pw = [n.astype(BF16) for n in nab]
            for _ in range(5):
                pw = [mm(x, x).astype(BF16) for x in pw]
                cursor = run_state_piece(cursor)
                t = [tt + mm(x, tt.astype(BF16)) for x, tt in zip(pw, t)]
                cursor = run_state_piece(cursor)
            mv = [mm(mk, x["vst"]) for mk, x in zip(mak, d)]
            tx = [mm(tt.astype(BF16), jnp.concatenate([x["ast"], m.astype(BF16)], axis=1))
                  for tt, x, m in zip(t, d, mv)]
        while cursor[0] < len(todo):
            cursor = run_state_piece(cursor)
        if g_state is not None:
            for rows, e, y in done:
                yp = [x[:CHUNK, :] + x[CHUNK:, :] for x in y]
                mu = [_dot(x, emean) for x in yp]
                yc = [x - m for x, m in zip(yp, mu)]
                var = [_dot(x * x, emean) for x in yc]
                bonus = [_dot(x["r"] * x["k"] * r_k[:, x["lanes"]], eones) * x["v"] for x in e]
                outs = [(yc[p] * lax.rsqrt(var[p] + LNX_EPS) * lnx_g[:, e[p]["lanes"]] + lnx_b[:, e[p]["lanes"]]
                         + bonus[p]) * e[p]["g"] for p in pairs]
                o_ref[rows, :] = jnp.concatenate(outs, axis=1)
            for q in range(n_seq):
                for p in pairs:
                    s_ref[q, p] = s_all[q][p]
        if g_pre is not None:
            for (j, p), x, txx, mr in zip(chains, d, tx, mrbk):
                ta_ref[1 - slot, j, p] = txx[:, :PAIR].astype(BF16)
                tv_ref[1 - slot, j, p] = txx[:, PAIR:]
                mr_ref[1 - slot, j, p] = mr
                cl_ref[1 - slot, j, p] = x["cl"]

    step(None, 0, 1)

    def body(gi, carry):
        step(gi, gi + 1, lax.rem(gi, 2))
        return carry

    lax.fori_loop(0, n_grp - 1, body, 0)
    step(n_grp - 1, None, (n_grp - 1) % 2)
    for q in range(n_seq):
        for p in range(n_pairs):
            sfin_ref[q, 2 * p] = s_ref[q, p, :HEAD_DIM, :HEAD_DIM]
            sfin_ref[q, 2 * p + 1] = s_ref[q, p, HEAD_DIM:, HEAD_DIM:]


def rwkv_scan(r, lw, k, a, b, v, g, s0, r_k, lnx_g, lnx_b, batch, seq):
    n, d_r = r.shape
    n_pairs = d_r // PAIR
    if seq > CHUNK:
        n_seq, rows = 1, min(SCAN_ROWS, seq)
    else:
        n_seq = next(q for q in (SCAN_GROUP, 2, 1) if batch % q == 0)
        rows = n_seq * seq
    steps = max(seq // rows, 1)
    grp = _scan_group(rows // CHUNK)
    tri = jnp.tril(jnp.ones((CHUNK, CHUNK), F32)).astype(BF16)
    blk = jnp.arange(PAIR) // HEAD_DIM
    same = (blk[:, None] == blk[None, :]).astype(F32)
    emean = (same / HEAD_DIM).astype(BF16)
    eones = same.astype(BF16)
    tok = pl.BlockSpec((rows, d_r), lambda bi, i: (bi * steps + i, 0))
    par = pl.BlockSpec((1, d_r), lambda bi, i: (0, 0))
    st = pl.BlockSpec((n_seq, 2 * n_pairs, HEAD_DIM, HEAD_DIM), lambda bi, i: (bi, 0, 0, 0))
    return pl.pallas_call(
        functools.partial(_rwkv_scan_kernel, n_chunks=rows // CHUNK, n_pairs=n_pairs, n_seq=n_seq),
        grid=(batch // n_seq, steps),
        in_specs=[tok] * 7 + [st, par, par, par, _full_spec(tri, 2), _full_spec(emean, 2), _full_spec(eones, 2)],
        out_specs=[tok, st],
        out_shape=[jax.ShapeDtypeStruct((n, d_r), F32),
                   jax.ShapeDtypeStruct((batch, 2 * n_pairs, HEAD_DIM, HEAD_DIM), F32)],
        scratch_shapes=[pltpu.VMEM((n_seq, n_pairs, PAIR, PAIR), F32),
                        pltpu.VMEM((2, grp, n_pairs, PAIR, PAIR), BF16),
                        pltpu.VMEM((2, grp, n_pairs, PAIR, PAIR), F32),
                        pltpu.VMEM((2, grp, n_pairs, PAIR, 2 * PAIR), BF16),
                        pltpu.VMEM((2, grp, n_pairs, CHUNK, PAIR), F32)],
        compiler_params=pltpu.CompilerParams(
            dimension_semantics=("arbitrary", "arbitrary"), vmem_limit_bytes=VMEM_LIMIT),
        name="rwkv_scan",
    )(r, lw, k, a, b, v, g, s0, r_k.reshape(1, d_r), lnx_g.reshape(1, d_r), lnx_b.reshape(1, d_r),
      tri, emean, eones)


def _sb_prompt_kernel(bias_ref, q_ref, k_ref, v_ref, u_ref, o_ref, acc_ref, c_ref, *, tq, tk, n_hp):
    g = pl.program_id(1)
    i = pl.program_id(2)
    n_sub = tq // tk
    n_heads = 2 * n_hp
    lane = lax.broadcasted_iota(jnp.int32, (1, PAIR), 1)
    m0 = lane < HEAD_DIM
    pair_lanes = [slice((h // 2) * PAIR, (h // 2 + 1) * PAIR) for h in range(n_heads)]
    q_heads = []
    for h in range(n_heads):
        qp = q_ref[:, pair_lanes[h]]
        qh = jnp.where(m0 if h % 2 == 0 else ~m0, qp, jnp.zeros_like(qp))
        bias = jnp.full((1, PAIR), bias_ref[n_heads * g + h], F32)
        b1 = bias.astype(BF16).astype(F32)
        b2 = (bias - b1).astype(BF16).astype(F32)
        b3 = (bias - b1) - b2
        ext = jnp.where(lane == 0, b1, jnp.where(lane == 1, b2, jnp.where(lane == 2, b3, 0.0)))
        q_heads.append(jnp.concatenate([qh, jnp.broadcast_to(ext, qh.shape).astype(BF16)], axis=1))
    k_ext = {nk: jnp.broadcast_to(jnp.where(lane < 3, 1.0, 0.0), (nk, PAIR)).astype(BF16)
             for nk in range(tk, tq + 1, tk)}
    acc_ref[...] = jnp.zeros_like(acc_ref)
    c_ref[...] = jnp.zeros_like(c_ref)
    u = u_ref[...]
    rr = lax.broadcasted_iota(jnp.int32, (tk, tk), 0)
    cc = lax.broadcasted_iota(jnp.int32, (tk, tk), 1)
    earlier = cc < rr
    tile = lambda x: jnp.concatenate([x] * (tk // PAIR), axis=1)

    def logits(kb, q0, nq_rows, n_keys_sub):
        nk = n_keys_sub * tk
        rows = pl.ds(pl.multiple_of(kb * tq, tq), nk)
        ks = [jnp.concatenate([k_ref[rows, pair_lanes[h]], k_ext[nk]], axis=1) for h in range(n_heads)]
        return [lax.dot_general(qh[q0:q0 + nq_rows], kh, (((1,), (1,)), ((), ())), preferred_element_type=F32)
                for qh, kh in zip(q_heads, ks)]

    def attend(zs, kb, q0, nq_rows, n_keys_sub, mask_last):
        qrows = slice(q0, q0 + nq_rows)
        nk = n_keys_sub * tk
        rows = pl.ds(pl.multiple_of(kb * tq, tq), nk)
        subs = [slice(j * tk, (j + 1) * tk) for j in range(n_keys_sub)]
        vs = [v_ref[rows, pair_lanes[h]] for h in range(n_heads)]
        ns = [jnp.maximum(z, 0.0) + jnp.log(1.0 + jnp.exp(_neg_abs(z))) for z in zs]
        masks =[earlier if (mask_last and j == n_keys_sub - 1) else None for j in range(n_keys_sub)]
        nm = [[n[:, s] if m is None else jnp.where(m, n[:, s], 0.0) for s, m in zip(subs, masks)] for n in ns]
        cums = [[jnp.dot(x.astype(BF16), u, preferred_element_type=F32) for x in nh] for nh in nm]
        tots = [[jnp.broadcast_to(cum[:, :1] + x[:, :1], (nq_rows, PAIR)) for cum, x in zip(cm, nh)]
                for cm, nh in zip(cums, nm)]
        ws = []
        for h in range(n_heads):
            run = c_ref[h, qrows]
            parts = [None] * n_keys_sub
            for j in range(n_keys_sub - 1, -1, -1):
                parts[j] = cums[h][j] + tile(run)
                run = run + tots[h][j]
            c_ref[h, qrows] = run
            w = jnp.exp((zs[h] - ns[h]) - jnp.concatenate(parts, axis=1))
            if mask_last:
                w = jnp.concatenate([w[:, s] if m is None else jnp.where(m, w[:, s], 0.0)
                                     for s, m in zip(subs, masks)], axis=1)
            ws.append(w)
        for h in range(n_heads):
            acc_ref[h, qrows] += jnp.dot(ws[h].astype(BF16), vs[h], preferred_element_type=F32)

    for r in range(n_sub):
        attend(logits(i, r * tk, tk, r + 1), i, r * tk, tk, r + 1, True)

    def body(it, carry):
        kb = i - 1 - it
        attend(logits(kb, 0, tq, n_sub), kb, 0, tq, n_sub, False)
        return carry

    lax.fori_loop(0, i, body, 0)
    o_ref[...] = jnp.concatenate(
        [jnp.where(m0, acc_ref[2 * hp], acc_ref[2 * hp + 1]) for hp in range(n_hp)], axis=1)


def sb_prompt(q_bf, k_bf, v_bf, bias, batch, seq):
    n, d_s = q_bf.shape
    n_hp = SB_PAIRS
    width = n_hp * PAIR
    n_grp = d_s // width
    tq = min(SB_TILE, seq)
    tk = min(SB_KEYS, seq)
    nq = seq // tq
    j = jnp.arange(tk)
    ux = (j[:, None] > j[None, :]).astype(BF16)
    grid_spec = pltpu.PrefetchScalarGridSpec(
        num_scalar_prefetch=1,
        grid=(batch, n_grp, nq),
        in_specs=[
            pl.BlockSpec((tq, width), lambda b, p, i, bias_ref: (b * nq + i, p)),
            pl.BlockSpec((seq, width), lambda b, p, i, bias_ref: (b, p)),
            pl.BlockSpec((seq, width), lambda b, p, i, bias_ref: (b, p)),
            pl.BlockSpec(ux.shape, lambda b, p, i, bias_ref: (0, 0)),
        ],
        out_specs=pl.BlockSpec((tq, width), lambda b, p, i, bias_ref: (b * nq + i, p)),
        scratch_shapes=[pltpu.VMEM((2 * n_hp, tq, PAIR), F32)] * 2,
    )
    return pl.pallas_call(
        functools.partial(_sb_prompt_kernel, tq=tq, tk=tk, n_hp=n_hp),
        grid_spec=grid_spec,
        out_shape=jax.ShapeDtypeStruct((n, d_s), F32),
        compiler_params=pltpu.CompilerParams(
            dimension_semantics=("arbitrary", "arbitrary", "arbitrary"), vmem_limit_bytes=VMEM_LIMIT),
        name="sb_prompt",
    )(bias, q_bf, k_bf, v_bf, ux)


def _sb_sample_kernel(pt_ref, q_ref, brow_ref, kn_ref, vn_ref, ck_hbm, cv_hbm, ux_ref, o_ref,
                      acc_ref, c_ref, kbuf, vbuf, sem, *, n_pg, n_heads, n_pages, depth):
    nbuf = depth + 1
    steps = pl.num_programs(1)
    total = pl.num_programs(0) * steps
    j = pl.program_id(1)
    t = pl.program_id(0) * steps + j

    def page_copies(step, slot):
        bb = step // steps
        base = n_pages - n_pg * (step - bb * steps + 1)
        cps = []
        for i in range(n_pg):
            page = pt_ref[bb, base + i]
            cps.append(pltpu.make_async_copy(ck_hbm.at[page], kbuf.at[slot, i], sem.at[slot]))
            cps.append(pltpu.make_async_copy(cv_hbm.at[page], vbuf.at[slot, i], sem.at[slot]))
        return cps

    @pl.when(t == 0)
    def _():
        for d in range(depth):
            for cp in page_copies(d, d):
                cp.start()

    slot = lax.rem(t, nbuf)
    for cp in page_copies(t, slot):
        cp.wait()
    kps = [kbuf.at[slot, i] for i in range(n_pg)]
    vps = [vbuf.at[slot, i] for i in range(n_pg)]
    qbd = q_ref[...]
    nrow = qbd.shape[0]
    brow = brow_ref[...]
    ux = ux_ref[...]

    def log_terms(z, valid):
        nb = z.shape[1] // PAIR
        z = z + jnp.concatenate([brow] * nb, axis=1)
        sp = _softplus(z)
        l = -sp
        if valid is not None:
            l = jnp.where(valid, l, 0.0)
        lst = jnp.concatenate([l[:, bi * PAIR:(bi + 1) * PAIR] for bi in range(nb)], axis=0)
        return z - sp, _dot_hl(lst, ux)

    def weights(la, cx, run, valid):
        nb = la.shape[1] // PAIR
        parts = [None] * nb
        for bi in range(nb - 1, -1, -1):
            blk = cx[bi * nrow:(bi + 1) * nrow, :]
            parts[bi] = blk[:, :PAIR] + run
            run = run + blk[:, PAIR:]
        w = jnp.exp(la + jnp.concatenate(parts, axis=1))
        if valid is not None:
            w = jnp.where(valid, w, 0.0)
        return w.astype(BF16), run

    @pl.when(j == 0)
    def _():
        t_of_row = lax.broadcasted_iota(jnp.int32, (nrow, PAIR), 0) // n_heads
        key = lax.broadcasted_iota(jnp.int32, (nrow, PAIR), 1)
        valid = key < t_of_row
        la, cx = log_terms(_dot_nt(qbd, kn_ref[...]), valid)
        w, run = weights(la, cx, jnp.zeros((nrow, PAIR), F32), valid)
        acc_ref[...] = _dot(w, vn_ref[...])
        c_ref[...] = run

    d_s = qbd.shape[1]
    page_t = lambda r: r[...].reshape(d_s, PAIR).astype(BF16)
    n_grp = 2 if n_pg % 2 == 0 else 1
    per = n_pg // n_grp
    groups = [range(gi * per, (gi + 1) * per) for gi in range(n_grp)]
    zs = [jnp.concatenate([_dot(qbd, page_t(kps[pi])) for pi in grp], axis=1) for grp in groups]
    terms = [log_terms(z, None) for z in zs]
    run = c_ref[...]
    ws = [None] * n_grp
    for gi in range(n_grp - 1, -1, -1):
        ws[gi], run = weights(terms[gi][0], terms[gi][1], run, None)
    c_ref[...] = run
    out = None
    for grp, w in zip(groups, ws):
        for li, pi in enumerate(grp):
            o = _dot_nt(w[:, li * PAIR:(li + 1) * PAIR], page_t(vps[pi]))
            out = o if out is None else out + o
    acc_ref[...] += out

    nxt = t + depth
    for cp in page_copies(lax.rem(nxt, total), lax.rem(nxt, nbuf)):
        cp.start()

    @pl.when(j == steps - 1)
    def _():
        d_s = acc_ref.shape[1]
        head_of_row = lax.broadcasted_iota(jnp.int32, (nrow, d_s), 0) % n_heads
        head_of_lane = lax.broadcasted_iota(jnp.int32, (nrow, d_s), 1) // HEAD_DIM
        sel = jnp.where(head_of_row == head_of_lane, acc_ref[...], 0.0)
        o_ref[...] = jnp.sum(sel.reshape(nrow // n_heads, n_heads, d_s), axis=1)

    @pl.when(t == total - 1)
    def _():
        for d in range(1, depth + 1):
            for cp in page_copies(lax.rem(t + d, total), lax.rem(t + d, nbuf)):
                cp.wait()


def sb_sample(q_bd, brow, k_new, v_new, cache_k, cache_v, page_table, n_heads):
    bsz, nrow, d_s = q_bd.shape
    n_pages = page_table.shape[1]
    page = cache_k.shape[3]
    assert page == PAIR
    n_pg = min(SB_PAGES, n_pages)
    steps = n_pages // n_pg
    jj = jnp.arange(PAIR)
    ux = jnp.concatenate([(jj[:, None] > jj[None, :]).astype(F32), jnp.ones((PAIR, PAIR), F32)], axis=1).astype(BF16)

    depth = SB_PREFETCH
    assert bsz * steps >= depth
    per_b = lambda shape: pl.BlockSpec((None,) + shape, lambda b, j, pt: (b, 0, 0))
    page_buf = pltpu.VMEM((depth + 1, n_pg, n_heads, HEAD_DIM, page), F32)
    grid_spec = pltpu.PrefetchScalarGridSpec(
        num_scalar_prefetch=1,
        grid=(bsz, steps),
        in_specs=[per_b((nrow, d_s)), pl.BlockSpec(brow.shape, lambda b, j, pt: (0, 0)),
                  per_b((PAIR, d_s)), per_b((PAIR, d_s)),
                  pl.BlockSpec(memory_space=pl.ANY), pl.BlockSpec(memory_space=pl.ANY),
                  pl.BlockSpec(ux.shape, lambda b, j, pt: (0, 0))],
        out_specs=per_b((nrow // n_heads, d_s)),
        scratch_shapes=[pltpu.VMEM((nrow, d_s), F32), pltpu.VMEM((nrow, PAIR), F32), page_buf, page_buf,
                        pltpu.SemaphoreType.DMA((depth + 1,))],
    )
    return pl.pallas_call(
        functools.partial(_sb_sample_kernel, n_pg=n_pg, n_heads=n_heads, n_pages=n_pages, depth=depth),
        grid_spec=grid_spec,
        out_shape=jax.ShapeDtypeStruct((bsz, nrow // n_heads, d_s), F32),
        compiler_params=pltpu.CompilerParams(
            dimension_semantics=("arbitrary", "arbitrary"), vmem_limit_bytes=VMEM_LIMIT),
        name="sb_sample",
    )(page_table, q_bd, brow, k_new, v_new, cache_k, cache_v, ux)


def _merge_kernel(h_ref, oa_ref, ob_ref, wg_ref, wba_ref, wbb_ref, wo_ref, g_ref, b_ref, o_ref, *, alpha):
    h = h_ref[...]
    d = h.shape[1]
    gates = jax.nn.sigmoid(jnp.dot(h.astype(BF16), wg_ref[...], preferred_element_type=F32))
    ma = jnp.dot(oa_ref[...].astype(BF16), wba_ref[...], preferred_element_type=F32)
    mb = jnp.dot(ob_ref[...].astype(BF16), wbb_ref[...], preferred_element_type=F32)
    merged = gates[:, :d] * ma + gates[:, d:] * mb
    y = alpha * h + jnp.dot(merged.astype(BF16), wo_ref[...], preferred_element_type=F32)
    o_ref[...] = _layer_norm(y, g_ref[...], b_ref[...])


def merge(h, o_a, o_b, w_gate_bf, w_ba_bf, w_bb_bf, w_out_bf, g, b, alpha):
    n, d = h.shape
    tm = min(MERGE_ROWS, n)
    row = lambda a: pl.BlockSpec((tm, a.shape[1]), lambda i: (i, 0))
    ws = [w_gate_bf, w_ba_bf, w_bb_bf, w_out_bf, g.reshape(1, d), b.reshape(1, d)]
    const = lambda a: pl.BlockSpec(a.shape, lambda i: (0,) * a.ndim, pipeline_mode=pl.Buffered(1))
    return pl.pallas_call(
        functools.partial(_merge_kernel, alpha=alpha),
        grid=(n // tm,),
        in_specs=[row(h), row(o_a), row(o_b)] + [const(a) for a in ws],
        out_specs=pl.BlockSpec((tm, d), lambda i: (i, 0)),
        out_shape=jax.ShapeDtypeStruct((n, d), F32),
        compiler_params=pltpu.CompilerParams(
            dimension_semantics=("arbitrary",), vmem_limit_bytes=VMEM_LIMIT),
        name="merge",
    )(h, o_a, o_b, *ws)


def _layer(x, batch, seq, h_last, wkv0, sb_fn, w, alpha):
    n, d = x.shape
    h = ffn_ln(x, w["ffn1_in"], w["ffn1_out"], w["ln1_g"], w["ln1_b"], alpha)
    if h_last is None:
        hl_rows = None
        rows_pad = seq
    else:
        hl_rows = jnp.repeat(h_last, seq, axis=0)
        rows_pad = CHUNK
    r, lw, k, a, b, v, g, q_bf, k_sb, v_sb, k_bf, v_bf = mixer_prep(h, hl_rows, batch, seq, w)
    n_h = w["w0"].shape[-1] // HEAD_DIM
    if wkv0 is None:
        s0 = jnp.zeros((batch, n_h, HEAD_DIM, HEAD_DIM), F32)
    else:
        s0 = wkv0.astype(F32)
    scan_in = [r, lw, k, a, b, v, g]
    if rows_pad != seq:
        pad = lambda t: jnp.pad(t.reshape(batch, seq, -1), ((0, 0), (0, rows_pad - seq), (0, 0))).reshape(
            batch * rows_pad, -1)
        scan_in = [pad(t) for t in scan_in]
    o_a, wkv = rwkv_scan(*scan_in, s0, w["r_k"], w["lnx_g"], w["lnx_b"], batch, rows_pad)
    if rows_pad != seq:
        o_a = o_a.reshape(batch, rows_pad, -1)[:, :seq].reshape(n, -1)
    o_b = sb_fn(q_bf, k_sb, v_sb, k_bf, v_bf)
    x2 = merge(h, o_a, o_b, w["w_gate"], w["w_ba"], w["w_bb"], w["w_out"], w["ln2_g"], w["ln2_b"], alpha)
    x3 = ffn_ln(x2, w["ffn2_in"], w["ffn2_out"], w["ln3_g"], w["ln3_b"], alpha)
    return x3, k_sb, v_sb, wkv, h.reshape(batch, seq, d)[:, -1]


def kernel(x_prompt, x_sample, cache_k, cache_v, state_wkv, state_shift, page_table, ln1_g, ln1_b, ffn1_w_in, ffn1_w_out, w_in, mu_rkv, mu_wag, w0, w_w1, w_w2, a0, a_w1, a_w2, g_w1, g_w2, k_k, k_a, r_k, lnx_g, lnx_b, sb_bias, w_branch, w_out, ln2_g, ln2_b, ffn2_w_in, ffn2_w_out, ln3_g, ln3_b):
    depth = ln1_g.shape[0]
    alpha = (2.0 * depth) ** 0.25
    bp, tp, d = x_prompt.shape
    bs, ts, _ = x_sample.shape
    d_r = w0.shape[-1]
    d_s = (w_in.shape[-1] - 3 * d_r - 2 * d) // 3
    n_hs = d_s // HEAD_DIM
    blk = jnp.arange(d_r) // HEAD_DIM
    e_head = (blk[:, None] == blk[None, :]).astype(BF16)

    xp = x_prompt.reshape(bp * tp, d)
    xs = x_sample.reshape(bs * ts, d)
    outs = [[] for _ in range(8)]
    for l in range(depth):
        row = lambda a: a[l].reshape(1, -1)
        w = dict(
            ffn1_in=ffn1_w_in[l].astype(BF16), ffn1_out=ffn1_w_out[l].astype(BF16),
            ffn2_in=ffn2_w_in[l].astype(BF16), ffn2_out=ffn2_w_out[l].astype(BF16),
            ln1_g=ln1_g[l], ln1_b=ln1_b[l], ln2_g=ln2_g[l], ln2_b=ln2_b[l], ln3_g=ln3_g[l], ln3_b=ln3_b[l],
            w_rkv=w_in[l][:, :3 * d_r].astype(BF16),
            w_sb=w_in[l][:, 3 * d_r:3 * d_r + 3 * d_s].astype(BF16),
            w_gate=w_in[l][:, 3 * d_r + 3 * d_s:].astype(BF16),
            w_w1=w_w1[l].astype(BF16), w_w2=w_w2[l].astype(BF16), a_w1=a_w1[l].astype(BF16),
            a_w2=a_w2[l].astype(BF16), g_w1=g_w1[l].astype(BF16), g_w2=g_w2[l].astype(BF16),
            mu_rkv=row(mu_rkv), mu_wag=mu_wag[l], w0=row(w0), a0=row(a0), k_k=row(k_k), k_a=row(k_a),
            e_head=e_head, r_k=r_k[l].reshape(-1), lnx_g=lnx_g[l], lnx_b=lnx_b[l],
            w_ba=w_branch[l][:d_r].astype(BF16), w_bb=w_branch[l][d_r:].astype(BF16),
            w_out=w_out[l].astype(BF16),
        )
        bias = sb_bias[l].astype(F32)

        def prompt_sb(q_bf, k_sb, v_sb, k_bf, v_bf):
            return sb_prompt(q_bf, k_bf, v_bf, bias, bp, tp)

        def sample_sb(q_bf, k_sb, v_sb, k_bf, v_bf):
            head_of_lane = jnp.arange(d_s) // HEAD_DIM
            onehot = (jnp.arange(n_hs)[:, None] == head_of_lane[None, :])
            q_bd = jnp.where(onehot[None, None], q_bf.reshape(bs, ts, 1, d_s), jnp.zeros((), BF16))
            q_bd = q_bd.reshape(bs, ts * n_hs, d_s)
            brow = jnp.broadcast_to(jnp.tile(bias, ts)[:, None], (ts * n_hs, PAIR))
            padk = lambda t: jnp.pad(t.reshape(bs, ts, d_s), ((0, 0), (0, PAIR - ts), (0, 0)))
            ck = jnp.transpose(cache_k[l], (0, 2, 3, 1))
            cv = jnp.transpose(cache_v[l], (0, 2, 3, 1))
            o = sb_sample(q_bd, brow, padk(k_sb), padk(v_sb), ck, cv, page_table, n_hs)
            return o.reshape(bs * ts, d_s)

        xp, kp, vp, wp, hp = _layer(xp, bp, tp, None, None, prompt_sb, w, alpha)
        xs, ks_, vs_, ws_, hs_ = _layer(xs, bs, ts, state_shift[l].astype(F32), state_wkv[l], sample_sb, w, alpha)
        for lst, val in zip(outs, (jnp.transpose(kp, (0, 3, 1, 2)), jnp.transpose(vp, (0, 3, 1, 2)), wp, hp,
                                   ks_.reshape(bs, ts, n_hs, HEAD_DIM), vs_.reshape(bs, ts, n_hs, HEAD_DIM),
                                   ws_.astype(state_wkv.dtype), hs_)):
            lst.append(val)
    return (xp.reshape(bp, tp, d), xs.reshape(bs, ts, d)) + tuple(jnp.stack(o) for o in outs)
```

```python
import functools

import jax
import jax.numpy as jnp
from jax import lax
from jax.experimental import pallas as pl
from jax.experimental.pallas import tpu as pltpu

F32 = jnp.float32
BF16 = jnp.bfloat16

HEAD_DIM = 64
PAIR = 2 * HEAD_DIM
CHUNK = 64
LN_EPS = 1e-5
LNX_EPS = 64e-5
SB_SCALE = HEAD_DIM ** -0.5

ROW_TILE = 512
FFN_ROWS = 1024
MERGE_ROWS = 1024
FFN_SUB = 512
FFN_RESIDENT_BYTES = 20 * 1024 * 1024
SCAN_ROWS = 1024
SCAN_GROUP = 4
SB_TILE = 512
SB_KEYS = 256
SB_PAIRS = 4
SB_PAGES = 16
SB_PREFETCH = 3
VMEM_LIMIT = 56 * 1024 * 1024


def _dot(a, b):
    return jnp.dot(a.astype(BF16), b.astype(BF16), preferred_element_type=F32)


def _dot_nt(a, b):
    return lax.dot_general(a.astype(BF16), b.astype(BF16), (((1,), (1,)), ((), ())),
                           preferred_element_type=F32)


def _split(x):
    hi = x.astype(BF16)
    lo = (x - hi.astype(F32)).astype(BF16)
    return hi, lo


def _dot_hl(a, b_exact):
    hi, lo = _split(a)
    return (jnp.dot(hi, b_exact, preferred_element_type=F32)
            + jnp.dot(lo, b_exact, preferred_element_type=F32))


def _dot3(a, b, nt=False):
    ah, al = _split(a)
    bh, bl = _split(b)
    if nt:
        d = lambda x, y: lax.dot_general(x, y, (((1,), (1,)), ((), ())), preferred_element_type=F32)
    else:
        d = lambda x, y: jnp.dot(x, y, preferred_element_type=F32)
    return d(ah, bh) + d(al, bh) + d(ah, bl)


LOG2E = 1.4426950408889634


def _neg_abs(x):
    bits = pltpu.bitcast(x, jnp.uint32) | jnp.uint32(0x80000000)
    return pltpu.bitcast(bits, F32)


def _softplus(u):
    return jnp.maximum(u, 0.0) + jnp.log1p(jnp.exp(-jnp.abs(u)))


def _layer_norm(y, g, b):
    mu = jnp.mean(y, axis=-1, keepdims=True)
    yc = y - mu
    var = jnp.mean(yc * yc, axis=-1, keepdims=True)
    return yc * lax.rsqrt(var + LN_EPS) * g + b


def _full_spec(a, grid_rank):
    nd = a.ndim
    if grid_rank == 1:
        return pl.BlockSpec(a.shape, lambda i: (0,) * nd)
    if grid_rank == 2:
        return pl.BlockSpec(a.shape, lambda i, j: (0,) * nd)
    return pl.BlockSpec(a.shape, lambda i, j, k: (0,) * nd)


def _ffn_ln_kernel(x_ref, wg_ref, wu_ref, wo_ref, g_ref, b_ref, o_ref, acc_ref, *, alpha, n_ff):
    j = pl.program_id(1)

    xb = x_ref[...].astype(BF16)
    tf = wg_ref.shape[1]
    cuts = list(range(0, tf, FFN_SUB)) + [tf]
    cols = [slice(a, b) for a, b in zip(cuts[:-1], cuts[1:])]
    gu = [(jnp.dot(xb, wg_ref[:, c], preferred_element_type=F32),
           jnp.dot(xb, wu_ref[:, c], preferred_element_type=F32)) for c in cols]
    mids = [(gate * jax.nn.sigmoid(gate) * up).astype(BF16) for gate, up in gu]
    out = jnp.dot(mids[0], wo_ref[cols[0], :], preferred_element_type=F32)
    for mid, c in zip(mids[1:], cols[1:]):
        out = out + jnp.dot(mid, wo_ref[c, :], preferred_element_type=F32)
    if n_ff == 1:
        o_ref[...] = _layer_norm(alpha * x_ref[...] + 0.5 * out, g_ref[...], b_ref[...])
        return

    @pl.when(j == 0)
    def _():
        acc_ref[...] = out

    @pl.when(j > 0)
    def _():
        acc_ref[...] += out

    @pl.when(j == n_ff - 1)
    def _():
        y = alpha * x_ref[...] + 0.5 * acc_ref[...]
        o_ref[...] = _layer_norm(y, g_ref[...], b_ref[...])


def _ffn_tile(d, d_ff):
    if 3 * d * d_ff * 2 <= FFN_RESIDENT_BYTES:
        return d_ff
    for n in (2, 11, 22):
        if d_ff % n == 0 and (d_ff // n) % 128 == 0:
            return d_ff // n
    return d_ff


def ffn_ln(x, w_in_bf, w_out_bf, g, b, alpha):
    n, d = x.shape
    d_ff = w_out_bf.shape[0]
    tm = min(FFN_ROWS, n)
    tf = _ffn_tile(d, d_ff)
    n_ff = d_ff // tf
    wmode = dict(pipeline_mode=pl.Buffered(1)) if n_ff == 1 else {}
    return pl.pallas_call(
        functools.partial(_ffn_ln_kernel, alpha=alpha, n_ff=n_ff),
        grid=(n // tm, n_ff),
        in_specs=[
            pl.BlockSpec((tm, d), lambda i, j: (i, 0)),
            pl.BlockSpec((d, tf), lambda i, j: (0, j), **wmode),
            pl.BlockSpec((d, tf), lambda i, j: (0, j + n_ff), **wmode),
            pl.BlockSpec((tf, d), lambda i, j: (j, 0), **wmode),
            pl.BlockSpec((1, d), lambda i, j: (0, 0)),
            pl.BlockSpec((1, d), lambda i, j: (0, 0)),
        ],
        out_specs=pl.BlockSpec((tm, d), lambda i, j: (i, 0)),
        out_shape=jax.ShapeDtypeStruct((n, d), F32),
        scratch_shapes=[pltpu.VMEM((tm, d), F32)],
        compiler_params=pltpu.CompilerParams(
            dimension_semantics=("arbitrary", "arbitrary"), vmem_limit_bytes=VMEM_LIMIT),
        name="ffn_ln",
    )(x, w_in_bf, w_in_bf, w_out_bf, g.reshape(1, d), b.reshape(1, d))


N_PREP_PARAMS = 15
N_PREP_OUTS = 12


def _rwkv_prep_math(h, h_prev, p, p_prev, params, outs, head_major):
    (wrkv_ref, ww1_ref, ww2_ref, aw1_ref, aw2_ref, gw1_ref, gw2_ref,
     mu_rkv_ref, mu_wag_ref, w0_ref, a0_ref, kk_ref, ka_ref, e_ref, wsb_ref) = params
    r_out, lw_out, k_out, a_out, b_out, v_out, g_out, q_sb, k_sb, v_sb, kb_sb, vb_sb = outs
    d_r = w0_ref.shape[-1]
    d_s = kb_sb.shape[-1]
    psb = jnp.dot(h.astype(BF16), wsb_ref[...], preferred_element_type=F32)
    q_sb[...] = (psb[:, :d_s] * SB_SCALE).astype(BF16)
    ksb = psb[:, d_s:2 * d_s]
    vsb = psb[:, 2 * d_s:]
    if head_major:
        k_sb[...] = ksb.T.reshape(k_sb.shape)
        v_sb[...] = vsb.T.reshape(v_sb.shape)
    else:
        k_sb[...] = ksb
        v_sb[...] = vsb
    kb_sb[...] = ksb.astype(BF16)
    vb_sb[...] = vsb.astype(BF16)
    rkv = p + (p_prev - p) * mu_rkv_ref[...]
    r = rkv[:, :d_r]
    k = rkv[:, d_r:2 * d_r]
    v = rkv[:, 2 * d_r:]
    dx = h_prev - h
    xw = h + dx * mu_wag_ref[0:1, :]
    xa = h + dx * mu_wag_ref[1:2, :]
    xg = h + dx * mu_wag_ref[2:3, :]
    lw = _dot(jnp.tanh(_dot(xw, ww1_ref[...])), ww2_ref[...])
    w_log = -_softplus(-(w0_ref[...] + lw)) - 0.5
    a_gate = jax.nn.sigmoid(a0_ref[...] + _dot(_dot(xa, aw1_ref[...]), aw2_ref[...]))
    g = _dot(jax.nn.sigmoid(_dot(xg, gw1_ref[...])), gw2_ref[...])
    kk = k * kk_ref[...]
    ss = _dot_hl(kk * kk, e_ref[...])
    kk = kk / jnp.maximum(jnp.sqrt(ss), 1e-12)
    r_out[...] = r
    lw_out[...] = -jnp.exp(w_log)
    k_out[...] = k * (1.0 + (a_gate - 1.0) * ka_ref[...])
    a_out[...] = -kk
    b_out[...] = kk * a_gate
    v_out[...] = v
    g_out[...] = g


def _rwkv_prep_prompt_kernel(h_ref, *refs, tiles_per_seq):
    params, outs = refs[:N_PREP_PARAMS], refs[N_PREP_PARAMS:N_PREP_PARAMS + N_PREP_OUTS]
    hcarry_ref, pcarry_ref = refs[-2:]
    i = pl.program_id(0)
    h = h_ref[...]
    tm = h.shape[0]

    @pl.when(i % tiles_per_seq == 0)
    def _():
        hcarry_ref[...] = jnp.zeros_like(hcarry_ref)
        pcarry_ref[...] = jnp.zeros_like(pcarry_ref)

    row = lax.broadcasted_iota(jnp.int32, (tm, 1), 0)
    p = jnp.dot(h.astype(BF16), params[0][...], preferred_element_type=F32)
    h_prev = jnp.where(row == 0, hcarry_ref[7:8, :], pltpu.roll(h, 1, 0))
    p_prev = jnp.where(row == 0, pcarry_ref[7:8, :], pltpu.roll(p, 1, 0))
    _rwkv_prep_math(h, h_prev, p, p_prev, params, outs, True)
    hcarry_ref[...] = h[tm - 8:, :]
    pcarry_ref[...] = p[tm - 8:, :]


def _rwkv_prep_sample_kernel(h_ref, hl_ref, *refs, seq):
    params, outs = refs[:N_PREP_PARAMS], refs[N_PREP_PARAMS:N_PREP_PARAMS + N_PREP_OUTS]
    h = h_ref[...]
    tm = h.shape[0]
    row = lax.broadcasted_iota(jnp.int32, (tm, 1), 0)
    first = row % seq == 0
    p = jnp.dot(h.astype(BF16), params[0][...], preferred_element_type=F32)
    p_last = jnp.dot(hl_ref[...].astype(BF16), params[0][...], preferred_element_type=F32)
    h_prev = jnp.where(first, hl_ref[...], pltpu.roll(h, 1, 0))
    p_prev = jnp.where(first, p_last, pltpu.roll(p, 1, 0))
    _rwkv_prep_math(h, h_prev, p, p_prev, params, outs, False)


def mixer_prep(h, h_last_rows, batch, seq, prm):
    n, d = h.shape
    d_r = prm["w0"].shape[-1]
    d_s = prm["w_sb"].shape[1] // 3
    n_h = d_s // HEAD_DIM
    plist = [prm["w_rkv"], prm["w_w1"], prm["w_w2"], prm["a_w1"], prm["a_w2"], prm["g_w1"], prm["g_w2"],
             prm["mu_rkv"], prm["mu_wag"], prm["w0"], prm["a0"], prm["k_k"], prm["k_a"], prm["e_head"],
             prm["w_sb"]]
    assert len(plist) == N_PREP_PARAMS
    if h_last_rows is None:
        tm = min(ROW_TILE, seq)
        kern = functools.partial(_rwkv_prep_prompt_kernel, tiles_per_seq=seq // tm)
        args = [h] + plist
        in_specs = [pl.BlockSpec((tm, d), lambda i: (i, 0))] + [_full_spec(a, 1) for a in plist]
        scratch = [pltpu.VMEM((8, d), F32), pltpu.VMEM((8, 3 * d_r), F32)]
    else:
        tm = n
        kern = functools.partial(_rwkv_prep_sample_kernel, seq=seq)
        args = [h, h_last_rows] + plist
        in_specs = [pl.BlockSpec((tm, d), lambda i: (i, 0))] * 2 + [_full_spec(a, 1) for a in plist]
        scratch = []
    row_spec = lambda w: pl.BlockSpec((tm, w), lambda i: (i, 0))
    if h_last_rows is None:
        tps = seq // tm
        kv_spec = pl.BlockSpec((None, n_h, HEAD_DIM, tm), lambda i: (i // tps, 0, 0, i % tps))
        kv_shape = jax.ShapeDtypeStruct((batch, n_h, HEAD_DIM, seq), F32)
    else:
        kv_spec = row_spec(d_s)
        kv_shape = jax.ShapeDtypeStruct((n, d_s), F32)
    return pl.pallas_call(
        kern,
        grid=(n // tm,),
        in_specs=in_specs,
        out_specs=[row_spec(d_r)] * 7 + [row_spec(d_s), kv_spec, kv_spec, row_spec(d_s), row_spec(d_s)],
        out_shape=[jax.ShapeDtypeStruct((n, d_r), F32)] * 7
                  + [jax.ShapeDtypeStruct((n, d_s), BF16), kv_shape, kv_shape,
                     jax.ShapeDtypeStruct((n, d_s), BF16), jax.ShapeDtypeStruct((n, d_s), BF16)],
        scratch_shapes=scratch,
        compiler_params=pltpu.CompilerParams(
            dimension_semantics=("arbitrary",), vmem_limit_bytes=VMEM_LIMIT),
        name="mixer_prep",
    )(*args)


def _scan_group(n_chunks):
    for g in (SCAN_GROUP, 2, 1):
        if n_chunks % g == 0:
            return g


def _stack_heads(x, m0):
    return jnp.concatenate([jnp.where(m0, x, 0.0), jnp.where(m0, 0.0, x)], axis=0)


def _rwkv_scan_kernel(r_ref, lw_ref, k_ref, a_ref, b_ref, v_ref, g_ref, s0_ref,
                      rk_ref, lg_ref, lb_ref, tri_ref, emean_ref, eones_ref,
                      o_ref, sfin_ref, s_ref, ta_ref, tv_ref, mr_ref, cl_ref, *, n_chunks, n_pairs, n_seq):
    i = pl.program_id(1)
    c2 = 2 * CHUNK
    seq_chunks = n_chunks // n_seq

    @pl.when(i == 0)
    def _():
        zero = jnp.zeros((HEAD_DIM, HEAD_DIM), F32)
        for q in range(n_seq):
            for p in range(n_pairs):
                top = jnp.concatenate([s0_ref[q, 2 * p], zero], axis=1)
                bot = jnp.concatenate([zero, s0_ref[q, 2 * p + 1]], axis=1)
                s_ref[q, p] = jnp.concatenate([top, bot], axis=0)

    lane = lax.broadcasted_iota(jnp.int32, (1, PAIR), 1)
    m0 = lane < HEAD_DIM
    rr = lax.broadcasted_iota(jnp.int32, (c2, c2), 0)
    cc = lax.broadcasted_iota(jnp.int32, (c2, c2), 1)
    strict = cc < rr
    incl = cc <= rr
    eye = (cc == rr).astype(F32)
    tri = tri_ref[...]
    mm = lambda x, y: jnp.dot(x, y, preferred_element_type=F32)
    mm_nt = lambda x, y: lax.dot_general(x, y, (((1,), (1,)), ((), ())), preferred_element_type=F32)

    emean = emean_ref[...]
    eones = eones_ref[...]
    r_k = rk_ref[...]
    lnx_g = lg_ref[...]
    lnx_b = lb_ref[...]

    pairs = range(n_pairs)
    cat0 = lambda x, y: jnp.concatenate([x, y], axis=0)
    stack = lambda x: _stack_heads(x, m0).astype(BF16)
    in_refs = (r_ref, lw_ref, k_ref, a_ref, b_ref, v_ref, g_ref)

    def load_rows(c):
        rows = pl.ds(pl.multiple_of(c * CHUNK, CHUNK), CHUNK)
        return rows, [ref[rows, :] for ref in in_refs]

    def pre_prep(ins, p):
        lanes = slice(p * PAIR, (p + 1) * PAIR)
        r, lw, k, a, b, v, g = (x[:, lanes] for x in ins)
        hi, lo = _split(lw)
        lo2 = (lw - hi.astype(F32) - lo.astype(F32)).astype(BF16)
        cl = mm(tri, hi) + mm(tri, lo) + mm(tri, lo2)
        w_inv = jnp.exp(-cl)
        return dict(cl=cl, vst=stack(v), ast=stack(a * jnp.exp(cl - lw)), rst=stack(r * jnp.exp(cl)),
                    bst=stack(b * w_inv), kst=stack(k * w_inv))

    def state_prep(ins, cl, p):
        lanes = slice(p * PAIR, (p + 1) * PAIR)
        r, lw, k, a, b, v, g = (x[:, lanes] for x in ins)
        cl_end = cl[CHUNK - 1:CHUNK, :]
        w_tail = jnp.exp(cl_end - cl)
        vst_f = _stack_heads(v, m0)
        return dict(r=r, k=k, v=v, g=g, lanes=lanes, w_end=jnp.exp(cl_end), vst_f=vst_f, vst=vst_f.astype(BF16),
                    rst=stack(r * jnp.exp(cl)), btl=stack(b * w_tail), ktl=stack(k * w_tail))

    grp = _scan_group(n_chunks)
    n_grp = n_chunks // grp
    assert n_seq == 1 or n_grp == 1

    def step(g_state, g_pre, slot):
        todo = []
        if g_state is not None:
            seq_of = [0 if n_seq == 1 else j // seq_chunks for j in range(grp)]
            s_all = [[s_ref[q, p] for p in pairs] for q in range(n_seq)]
            done = []
            for j in range(grp):
                s_cur = s_all[seq_of[j]]

                def first(j=j, s_cur=s_cur):
                    rows, ins = load_rows(g_state * grp + j)
                    s = list(s_cur)
                    sb = [x.astype(BF16) for x in s]
                    u = [mm_nt(ta_ref[slot, j, p], sb[p]) + tv_ref[slot, j, p] for p in pairs]
                    return rows, ins, s, sb, u

                def second(st, j=j, s_cur=s_cur):
                    rows, ins, s, sb, u = st
                    e = [state_prep(ins, cl_ref[slot, j, p], p) for p in pairs]
                    uv_t = [cat0(u[p], e[p]["vst_f"]).T.astype(BF16) for p in pairs]
                    for p in pairs:
                        s_cur[p] = s[p] * e[p]["w_end"] + mm(uv_t[p], cat0(e[p]["btl"], e[p]["ktl"]))
                    y = [mm_nt(e[p]["rst"], sb[p])
                         + mm(mr_ref[slot, j, p], cat0(u[p].astype(BF16), e[p]["vst"])) for p in pairs]
                    done.append((rows, e, y))
                todo.append((first, second))

        def run_state_piece(state):
            idx, pending = state
            if idx >= len(todo):
                return state
            if pending is None:
                return idx, todo[idx][0]()
            todo[idx][1](pending)
            return idx + 1, None

        cursor = (0, None)
        if g_pre is not None:
            chains = [(j, p) for j in range(grp) for p in pairs]
            ins_p = [load_rows(g_pre * grp + j)[1] for j in range(grp)]
            cursor = run_state_piece(cursor)
            d = [pre_prep(ins_p[j], p) for j, p in chains]
            m4 = [mm_nt(cat0(x["ast"], x["rst"]), cat0(x["bst"], x["kst"])) for x in d]
            cursor = run_state_piece(cursor)
            nab = [jnp.where(strict, m[:c2, :c2], 0.0) for m in m4]
            mak = [jnp.where(strict, m[:c2, c2:], 0.0).astype(BF16) for m in m4]
            mrbk = [jnp.concatenate([jnp.where(incl, m[c2:, :c2], 0.0), jnp.where(incl, m[c2:, c2:], 0.0)],
                                    axis=1).astype(BF16) for m in m4]
            t = [eye + n for n in nab]
            pw = [n.astype(BF16) for n in nab]
            for _ in range(5):
                pw = [mm(x, x).astype(BF16) for x in pw]
                cursor = run_state_piece(cursor)
                t = [tt + mm(x, tt.astype(BF16)) for x, tt in zip(pw, t)]
                cursor = run_state_piece(cursor)
            mv = [mm(mk, x["vst"]) for mk, x in zip(mak, d)]
            tx = [mm(tt.astype(BF16), jnp.concatenate([x["ast"], m.astype(BF16)], axis=1))
                  for tt, x, m in zip(t, d, mv)]
        while cursor[0] < len(todo):
            cursor = run_state_piece(cursor)
        if g_state is not None:
            for rows, e, y in done:
                yp = [x[:CHUNK, :] + x[CHUNK:, :] for x in y]
                mu = [_dot(x, emean) for x in yp]
                yc = [x - m for x, m in zip(yp, mu)]
                var = [_dot(x * x, emean) for x in yc]
                bonus = [_dot(x["r"] * x["k"] * r_k[:, x["lanes"]], eones) * x["v"] for x in e]
                outs = [(yc[p] * lax.rsqrt(var[p] + LNX_EPS) * lnx_g[:, e[p]["lanes"]] + lnx_b[:, e[p]["lanes"]]
                         + bonus[p]) * e[p]["g"] for p in pairs]
                o_ref[rows, :] = jnp.concatenate(outs, axis=1)
            for q in range(n_seq):
                for p in pairs:
                    s_ref[q, p] = s_all[q][p]
        if g_pre is not None:
            for (j, p), x, txx, mr in zip(chains, d, tx, mrbk):
                ta_ref[1 - slot, j, p] = txx[:, :PAIR].astype(BF16)
                tv_ref[1 - slot, j, p] = txx[:, PAIR:]
                mr_ref[1 - slot, j, p] = mr
                cl_ref[1 - slot, j, p] = x["cl"]

    step(None, 0, 1)

    def body(gi, carry):
        step(gi, gi + 1, lax.rem(gi, 2))
        return carry

    lax.fori_loop(0, n_grp - 1, body, 0)
    step(n_grp - 1, None, (n_grp - 1) % 2)
    for q in range(n_seq):
        for p in range(n_pairs):
            sfin_ref[q, 2 * p] = s_ref[q, p, :HEAD_DIM, :HEAD_DIM]
            sfin_ref[q, 2 * p + 1] = s_ref[q, p, HEAD_DIM:, HEAD_DIM:]


def rwkv_scan(r, lw, k, a, b, v, g, s0, r_k, lnx_g, lnx_b, batch, seq):
    n, d_r = r.shape
    n_pairs = d_r // PAIR
    if seq > CHUNK:
        n_seq, rows = 1, min(SCAN_ROWS, seq)
    else:
        n_seq = next(q for q in (SCAN_GROUP, 2, 1) if batch % q == 0)
        rows = n_seq * seq
    steps = max(seq // rows, 1)
    grp = _scan_group(rows // CHUNK)
    tri = jnp.tril(jnp.ones((CHUNK, CHUNK), F32)).astype(BF16)
    blk = jnp.arange(PAIR) // HEAD_DIM
    same = (blk[:, None] == blk[None, :]).astype(F32)
    emean = (same / HEAD_DIM).astype(BF16)
    eones = same.astype(BF16)
    tok = pl.BlockSpec((rows, d_r), lambda bi, i: (bi * steps + i, 0))
    par = pl.BlockSpec((1, d_r), lambda bi, i: (0, 0))
    st = pl.BlockSpec((n_seq, 2 * n_pairs, HEAD_DIM, HEAD_DIM), lambda bi, i: (bi, 0, 0, 0))
    return pl.pallas_call(
        functools.partial(_rwkv_scan_kernel, n_chunks=rows // CHUNK, n_pairs=n_pairs, n_seq=n_seq),
        grid=(batch // n_seq, steps),
        in_specs=[tok] * 7 + [st, par, par, par, _full_spec(tri, 2), _full_spec(emean, 2), _full_spec(eones, 2)],
        out_specs=[tok, st],
        out_shape=[jax.ShapeDtypeStruct((n, d_r), F32),
                   jax.ShapeDtypeStruct((batch, 2 * n_pairs, HEAD_DIM, HEAD_DIM), F32)],
        scratch_shapes=[pltpu.VMEM((n_seq, n_pairs, PAIR, PAIR), F32),
                        pltpu.VMEM((2, grp, n_pairs, PAIR, PAIR), BF16),
                        pltpu.VMEM((2, grp, n_pairs, PAIR, PAIR), F32),
                        pltpu.VMEM((2, grp, n_pairs, PAIR, 2 * PAIR), BF16),
                        pltpu.VMEM((2, grp, n_pairs, CHUNK, PAIR), F32)],
        compiler_params=pltpu.CompilerParams(
            dimension_semantics=("arbitrary", "arbitrary"), vmem_limit_bytes=VMEM_LIMIT),
        name="rwkv_scan",
    )(r, lw, k, a, b, v, g, s0, r_k.reshape(1, d_r), lnx_g.reshape(1, d_r), lnx_b.reshape(1, d_r),
      tri, emean, eones)


def _sb_prompt_kernel(bias_ref, q_ref, k_ref, v_ref, u_ref, o_ref, acc_ref, c_ref, *, tq, tk, n_hp):
    g = pl.program_id(1)
    i = pl.program_id(2)
    n_sub = tq // tk
    n_heads = 2 * n_hp
    lane = lax.broadcasted_iota(jnp.int32, (1, PAIR), 1)
    m0 = lane < HEAD_DIM
    pair_lanes = [slice((h // 2) * PAIR, (h // 2 + 1) * PAIR) for h in range(n_heads)]
    q_heads = []
    for h in range(n_heads):
        qp = q_ref[:, pair_lanes[h]]
        qh = jnp.where(m0 if h % 2 == 0 else ~m0, qp, jnp.zeros_like(qp))
        bias = jnp.full((1, PAIR), bias_ref[n_heads * g + h], F32)
        b1 = bias.astype(BF16).astype(F32)
        b2 = (bias - b1).astype(BF16).astype(F32)
        b3 = (bias - b1) - b2
        ext = jnp.where(lane == 0, b1, jnp.where(lane == 1, b2, jnp.where(lane == 2, b3, 0.0)))
        q_heads.append(jnp.concatenate([qh, jnp.broadcast_to(ext, qh.shape).astype(BF16)], axis=1))
    k_ext = {nk: jnp.broadcast_to(jnp.where(lane < 3, 1.0, 0.0), (nk, PAIR)).astype(BF16)
             for nk in range(tk, tq + 1, tk)}
    acc_ref[...] = jnp.zeros_like(acc_ref)
    c_ref[...] = jnp.zeros_like(c_ref)
    u = u_ref[...]
    rr = lax.broadcasted_iota(jnp.int32, (tk, tk), 0)
    cc = lax.broadcasted_iota(jnp.int32, (tk, tk), 1)
    earlier = cc < rr
    tile = lambda x: jnp.concatenate([x] * (tk // PAIR), axis=1)

    def logits(kb, q0, nq_rows, n_keys_sub):
        nk = n_keys_sub * tk
        rows = pl.ds(pl.multiple_of(kb * tq, tq), nk)
        ks = [jnp.concatenate([k_ref[rows, pair_lanes[h]], k_ext[nk]], axis=1) for h in range(n_heads)]
        return [lax.dot_general(qh[q0:q0 + nq_rows], kh, (((1,), (1,)), ((), ())), preferred_element_type=F32)
                for qh, kh in zip(q_heads, ks)]

    def attend(zs, kb, q0, nq_rows, n_keys_sub, mask_last):
        qrows = slice(q0, q0 + nq_rows)
        nk = n_keys_sub * tk
        rows = pl.ds(pl.multiple_of(kb * tq, tq), nk)
        subs = [slice(j * tk, (j + 1) * tk) for j in range(n_keys_sub)]
        vs = [v_ref[rows, pair_lanes[h]] for h in range(n_heads)]
        ns = [jnp.maximum(z, 0.0) + jnp.log(1.0 + jnp.exp(_neg_abs(z))) for z in zs]
        masks =[earlier if (mask_last and j == n_keys_sub - 1) else None for j in range(n_keys_sub)]
        nm = [[n[:, s] if m is None else jnp.where(m, n[:, s], 0.0) for s, m in zip(subs, masks)] for n in ns]
        cums = [[jnp.dot(x.astype(BF16), u, preferred_element_type=F32) for x in nh] for nh in nm]
        tots = [[jnp.broadcast_to(cum[:, :1] + x[:, :1], (nq_rows, PAIR)) for cum, x in zip(cm, nh)]
                for cm, nh in zip(cums, nm)]
        ws = []
        for h in range(n_heads):
            run = c_ref[h, qrows]
            parts = [None] * n_keys_sub
            for j in range(n_keys_sub - 1, -1, -1):
                parts[j] = cums[h][j] + tile(run)
                run = run + tots[h][j]
            c_ref[h, qrows] = run
            w = jnp.exp((zs[h] - ns[h]) - jnp.concatenate(parts, axis=1))
            if mask_last:
                w = jnp.concatenate([w[:, s] if m is None else jnp.where(m, w[:, s], 0.0)
                                     for s, m in zip(subs, masks)], axis=1)
            ws.append(w)
        for h in range(n_heads):
            acc_ref[h, qrows] += jnp.dot(ws[h].astype(BF16), vs[h], preferred_element_type=F32)

    for r in range(n_sub):
        attend(logits(i, r * tk, tk, r + 1), i, r * tk, tk, r + 1, True)

    def body(it, carry):
        kb = i - 1 - it
        attend(logits(kb, 0, tq, n_sub), kb, 0, tq, n_sub, False)
        return carry

    lax.fori_loop(0, i, body, 0)
    o_ref[...] = jnp.concatenate(
        [jnp.where(m0, acc_ref[2 * hp], acc_ref[2 * hp + 1]) for hp in range(n_hp)], axis=1)


def sb_prompt(q_bf, k_bf, v_bf, bias, batch, seq):
    n, d_s = q_bf.shape
    n_hp = SB_PAIRS
    width = n_hp * PAIR
    n_grp = d_s // width
    tq = min(SB_TILE, seq)
    tk = min(SB_KEYS, seq)
    nq = seq // tq
    j = jnp.arange(tk)
    ux = (j[:, None] > j[None, :]).astype(BF16)
    grid_spec = pltpu.PrefetchScalarGridSpec(
        num_scalar_prefetch=1,
        grid=(batch, n_grp, nq),
        in_specs=[
            pl.BlockSpec((tq, width), lambda b, p, i, bias_ref: (b * nq + i, p)),
            pl.BlockSpec((seq, width), lambda b, p, i, bias_ref: (b, p)),
            pl.BlockSpec((seq, width), lambda b, p, i, bias_ref: (b, p)),
            pl.BlockSpec(ux.shape, lambda b, p, i, bias_ref: (0, 0)),
        ],
        out_specs=pl.BlockSpec((tq, width), lambda b, p, i, bias_ref: (b * nq + i, p)),
        scratch_shapes=[pltpu.VMEM((2 * n_hp, tq, PAIR), F32)] * 2,
    )
    return pl.pallas_call(
        functools.partial(_sb_prompt_kernel, tq=tq, tk=tk, n_hp=n_hp),
        grid_spec=grid_spec,
        out_shape=jax.ShapeDtypeStruct((n, d_s), F32),
        compiler_params=pltpu.CompilerParams(
            dimension_semantics=("arbitrary", "arbitrary", "arbitrary"), vmem_limit_bytes=VMEM_LIMIT),
        name="sb_prompt",
    )(bias, q_bf, k_bf, v_bf, ux)


def _sb_sample_kernel(pt_ref, q_ref, brow_ref, kn_ref, vn_ref, ck_hbm, cv_hbm, ux_ref, o_ref,
                      acc_ref, c_ref, kbuf, vbuf, sem, *, n_pg, n_heads, n_pages, depth):
    nbuf = depth + 1
    steps = pl.num_programs(1)
    total = pl.num_programs(0) * steps
    j = pl.program_id(1)
    t = pl.program_id(0) * steps + j

    def page_copies(step, slot):
        bb = step // steps
        base = n_pages - n_pg * (step - bb * steps + 1)
        cps = []
        for i in range(n_pg):
            page = pt_ref[bb, base + i]
            cps.append(pltpu.make_async_copy(ck_hbm.at[page], kbuf.at[slot, i], sem.at[slot]))
            cps.append(pltpu.make_async_copy(cv_hbm.at[page], vbuf.at[slot, i], sem.at[slot]))
        return cps

    @pl.when(t == 0)
    def _():
        for d in range(depth):
            for cp in page_copies(d, d):
                cp.start()

    slot = lax.rem(t, nbuf)
    for cp in page_copies(t, slot):
        cp.wait()
    kps = [kbuf.at[slot, i] for i in range(n_pg)]
    vps = [vbuf.at[slot, i] for i in range(n_pg)]
    qbd = q_ref[...]
    nrow = qbd.shape[0]
    brow = brow_ref[...]
    ux = ux_ref[...]

    def log_terms(z, valid):
        nb = z.shape[1] // PAIR
        z = z + jnp.concatenate([brow] * nb, axis=1)
        sp = _softplus(z)
        l = -sp
        if valid is not None:
            l = jnp.where(valid, l, 0.0)
        lst = jnp.concatenate([l[:, bi * PAIR:(bi + 1) * PAIR] for bi in range(nb)], axis=0)
        return z - sp, _dot_hl(lst, ux)

    def weights(la, cx, run, valid):
        nb = la.shape[1] // PAIR
        parts = [None] * nb
        for bi in range(nb - 1, -1, -1):
            blk = cx[bi * nrow:(bi + 1) * nrow, :]
            parts[bi] = blk[:, :PAIR] + run
            run = run + blk[:, PAIR:]
        w = jnp.exp(la + jnp.concatenate(parts, axis=1))
        if valid is not None:
            w = jnp.where(valid, w, 0.0)
        return w.astype(BF16), run

    @pl.when(j == 0)
    def _():
        t_of_row = lax.broadcasted_iota(jnp.int32, (nrow, PAIR), 0) // n_heads
        key = lax.broadcasted_iota(jnp.int32, (nrow, PAIR), 1)
        valid = key < t_of_row
        la, cx = log_terms(_dot_nt(qbd, kn_ref[...]), valid)
        w, run = weights(la, cx, jnp.zeros((nrow, PAIR), F32), valid)
        acc_ref[...] = _dot(w, vn_ref[...])
        c_ref[...] = run

    d_s = qbd.shape[1]
    page_t = lambda r: r[...].reshape(d_s, PAIR).astype(BF16)
    n_grp = 2 if n_pg % 2 == 0 else 1
    per = n_pg // n_grp
    groups = [range(gi * per, (gi + 1) * per) for gi in range(n_grp)]
    zs = [jnp.concatenate([_dot(qbd, page_t(kps[pi])) for pi in grp], axis=1) for grp in groups]
    terms = [log_terms(z, None) for z in zs]
    run = c_ref[...]
    ws = [None] * n_grp
    for gi in range(n_grp - 1, -1, -1):
        ws[gi], run = weights(terms[gi][0], terms[gi][1], run, None)
    c_ref[...] = run
    out = None
    for grp, w in zip(groups, ws):
        for li, pi in enumerate(grp):
            o = _dot_nt(w[:, li * PAIR:(li + 1) * PAIR], page_t(vps[pi]))
            out = o if out is None else out + o
    acc_ref[...] += out

    nxt = t + depth
    for cp in page_copies(lax.rem(nxt, total), lax.rem(nxt, nbuf)):
        cp.start()

    @pl.when(j == steps - 1)
    def _():
        d_s = acc_ref.shape[1]
        head_of_row = lax.broadcasted_iota(jnp.int32, (nrow, d_s), 0) % n_heads
        head_of_lane = lax.broadcasted_iota(jnp.int32, (nrow, d_s), 1) // HEAD_DIM
        sel = jnp.where(head_of_row == head_of_lane, acc_ref[...], 0.0)
        o_ref[...] = jnp.sum(sel.reshape(nrow // n_heads, n_heads, d_s), axis=1)

    @pl.when(t == total - 1)
    def _():
        for d in range(1, depth + 1):
            for cp in page_copies(lax.rem(t + d, total), lax.rem(t + d, nbuf)):
                cp.wait()


def sb_sample(q_bd, brow, k_new, v_new, cache_k, cache_v, page_table, n_heads):
    bsz, nrow, d_s = q_bd.shape
    n_pages = page_table.shape[1]
    page = cache_k.shape[3]
    assert page == PAIR
    n_pg = min(SB_PAGES, n_pages)
    steps = n_pages // n_pg
    jj = jnp.arange(PAIR)
    ux = jnp.concatenate([(jj[:, None] > jj[None, :]).astype(F32), jnp.ones((PAIR, PAIR), F32)], axis=1).astype(BF16)

    depth = SB_PREFETCH
    assert bsz * steps >= depth
    per_b = lambda shape: pl.BlockSpec((None,) + shape, lambda b, j, pt: (b, 0, 0))
    page_buf = pltpu.VMEM((depth + 1, n_pg, n_heads, HEAD_DIM, page), F32)
    grid_spec = pltpu.PrefetchScalarGridSpec(
        num_scalar_prefetch=1,
        grid=(bsz, steps),
        in_specs=[per_b((nrow, d_s)), pl.BlockSpec(brow.shape, lambda b, j, pt: (0, 0)),
                  per_b((PAIR, d_s)), per_b((PAIR, d_s)),
                  pl.BlockSpec(memory_space=pl.ANY), pl.BlockSpec(memory_space=pl.ANY),
                  pl.BlockSpec(ux.shape, lambda b, j, pt: (0, 0))],
        out_specs=per_b((nrow // n_heads, d_s)),
        scratch_shapes=[pltpu.VMEM((nrow, d_s), F32), pltpu.VMEM((nrow, PAIR), F32), page_buf, page_buf,
                        pltpu.SemaphoreType.DMA((depth + 1,))],
    )
    return pl.pallas_call(
        functools.partial(_sb_sample_kernel, n_pg=n_pg, n_heads=n_heads, n_pages=n_pages, depth=depth),
        grid_spec=grid_spec,
        out_shape=jax.ShapeDtypeStruct((bsz, nrow // n_heads, d_s), F32),
        compiler_params=pltpu.CompilerParams(
            dimension_semantics=("arbitrary", "arbitrary"), vmem_limit_bytes=VMEM_LIMIT),
        name="sb_sample",
    )(page_table, q_bd, brow, k_new, v_new, cache_k, cache_v, ux)


def _merge_kernel(h_ref, oa_ref, ob_ref, wg_ref, wba_ref, wbb_ref, wo_ref, g_ref, b_ref, o_ref, *, alpha):
    h = h_ref[...]
    d = h.shape[1]
    gates = jax.nn.sigmoid(jnp.dot(h.astype(BF16), wg_ref[...], preferred_element_type=F32))
    ma = jnp.dot(oa_ref[...].astype(BF16), wba_ref[...], preferred_element_type=F32)
    mb = jnp.dot(ob_ref[...].astype(BF16), wbb_ref[...], preferred_element_type=F32)
    merged = gates[:, :d] * ma + gates[:, d:] * mb
    y = alpha * h + jnp.dot(merged.astype(BF16), wo_ref[...], preferred_element_type=F32)
    o_ref[...] = _layer_norm(y, g_ref[...], b_ref[...])


def merge(h, o_a, o_b, w_gate_bf, w_ba_bf, w_bb_bf, w_out_bf, g, b, alpha):
    n, d = h.shape
    tm = min(MERGE_ROWS, n)
    row = lambda a: pl.BlockSpec((tm, a.shape[1]), lambda i: (i, 0))
    ws = [w_gate_bf, w_ba_bf, w_bb_bf, w_out_bf, g.reshape(1, d), b.reshape(1, d)]
    const = lambda a: pl.BlockSpec(a.shape, lambda i: (0,) * a.ndim, pipeline_mode=pl.Buffered(1))
    return pl.pallas_call(
        functools.partial(_merge_kernel, alpha=alpha),
        grid=(n // tm,),
        in_specs=[row(h), row(o_a), row(o_b)] + [const(a) for a in ws],
        out_specs=pl.BlockSpec((tm, d), lambda i: (i, 0)),
        out_shape=jax.ShapeDtypeStruct((n, d), F32),
        compiler_params=pltpu.CompilerParams(
            dimension_semantics=("arbitrary",), vmem_limit_bytes=VMEM_LIMIT),
        name="merge",
    )(h, o_a, o_b, *ws)


def _layer(x, batch, seq, h_last, wkv0, sb_fn, w, alpha):
    n, d = x.shape
    h = ffn_ln(x, w["ffn1_in"], w["ffn1_out"], w["ln1_g"], w["ln1_b"], alpha)
    if h_last is None:
        hl_rows = None
        rows_pad = seq
    else:
        hl_rows = jnp.repeat(h_last, seq, axis=0)
        rows_pad = CHUNK
    r, lw, k, a, b, v, g, q_bf, k_sb, v_sb, k_bf, v_bf = mixer_prep(h, hl_rows, batch, seq, w)
    n_h = w["w0"].shape[-1] // HEAD_DIM
    if wkv0 is None:
        s0 = jnp.zeros((batch, n_h, HEAD_DIM, HEAD_DIM), F32)
    else:
        s0 = wkv0.astype(F32)
    scan_in = [r, lw, k, a, b, v, g]
    if rows_pad != seq:
        pad = lambda t: jnp.pad(t.reshape(batch, seq, -1), ((0, 0), (0, rows_pad - seq), (0, 0))).reshape(
            batch * rows_pad, -1)
        scan_in = [pad(t) for t in scan_in]
    o_a, wkv = rwkv_scan(*scan_in, s0, w["r_k"], w["lnx_g"], w["lnx_b"], batch, rows_pad)
    if rows_pad != seq:
        o_a = o_a.reshape(batch, rows_pad, -1)[:, :seq].reshape(n, -1)
    o_b = sb_fn(q_bf, k_sb, v_sb, k_bf, v_bf)
    x2 = merge(h, o_a, o_b, w["w_gate"], w["w_ba"], w["w_bb"], w["w_out"], w["ln2_g"], w["ln2_b"], alpha)
    x3 = ffn_ln(x2, w["ffn2_in"], w["ffn2_out"], w["ln3_g"], w["ln3_b"], alpha)
    return x3, k_sb, v_sb, wkv, h.reshape(batch, seq, d)[:, -1]


def kernel(x_prompt, x_sample, cache_k, cache_v, state_wkv, state_shift, page_table, ln1_g, ln1_b, ffn1_w_in, ffn1_w_out, w_in, mu_rkv, mu_wag, w0, w_w1, w_w2, a0, a_w1, a_w2, g_w1, g_w2, k_k, k_a, r_k, lnx_g, lnx_b, sb_bias, w_branch, w_out, ln2_g, ln2_b, ffn2_w_in, ffn2_w_out, ln3_g, ln3_b):
    depth = ln1_g.shape[0]
    alpha = (2.0 * depth) ** 0.25
    bp, tp, d = x_prompt.shape
    bs, ts, _ = x_sample.shape
    d_r = w0.shape[-1]
    d_s = (w_in.shape[-1] - 3 * d_r - 2 * d) // 3
    n_hs = d_s // HEAD_DIM
    blk = jnp.arange(d_r) // HEAD_DIM
    e_head = (blk[:, None] == blk[None, :]).astype(BF16)

    xp = x_prompt.reshape(bp * tp, d)
    xs = x_sample.reshape(bs * ts, d)
    outs = [[] for _ in range(8)]
    for l in range(depth):
        row = lambda a: a[l].reshape(1, -1)
        w = dict(
            ffn1_in=ffn1_w_in[l].astype(BF16), ffn1_out=ffn1_w_out[l].astype(BF16),
            ffn2_in=ffn2_w_in[l].astype(BF16), ffn2_out=ffn2_w_out[l].astype(BF16),
            ln1_g=ln1_g[l], ln1_b=ln1_b[l], ln2_g=ln2_g[l], ln2_b=ln2_b[l], ln3_g=ln3_g[l], ln3_b=ln3_b[l],
            w_rkv=w_in[l][:, :3 * d_r].astype(BF16),
            w_sb=w_in[l][:, 3 * d_r:3 * d_r + 3 * d_s].astype(BF16),
            w_gate=w_in[l][:, 3 * d_r + 3 * d_s:].astype(BF16),
            w_w1=w_w1[l].astype(BF16), w_w2=w_w2[l].astype(BF16), a_w1=a_w1[l].astype(BF16),
            a_w2=a_w2[l].astype(BF16), g_w1=g_w1[l].astype(BF16), g_w2=g_w2[l].astype(BF16),
            mu_rkv=row(mu_rkv), mu_wag=mu_wag[l], w0=row(w0), a0=row(a0), k_k=row(k_k), k_a=row(k_a),
            e_head=e_head, r_k=r_k[l].reshape(-1), lnx_g=lnx_g[l], lnx_b=lnx_b[l],
            w_ba=w_branch[l][:d_r].astype(BF16), w_bb=w_branch[l][d_r:].astype(BF16),
            w_out=w_out[l].astype(BF16),
        )
        bias = sb_bias[l].astype(F32)

        def prompt_sb(q_bf, k_sb, v_sb, k_bf, v_bf):
            return sb_prompt(q_bf, k_bf, v_bf, bias, bp, tp)

        def sample_sb(q_bf, k_sb, v_sb, k_bf, v_bf):
            head_of_lane = jnp.arange(d_s) // HEAD_DIM
            onehot = (jnp.arange(n_hs)[:, None] == head_of_lane[None, :])
            q_bd = jnp.where(onehot[None, None], q_bf.reshape(bs, ts, 1, d_s), jnp.zeros((), BF16))
            q_bd = q_bd.reshape(bs, ts * n_hs, d_s)
            brow = jnp.broadcast_to(jnp.tile(bias, ts)[:, None], (ts * n_hs, PAIR))
            padk = lambda t: jnp.pad(t.reshape(bs, ts, d_s), ((0, 0), (0, PAIR - ts), (0, 0)))
            ck = jnp.transpose(cache_k[l], (0, 2, 3, 1))
            cv = jnp.transpose(cache_v[l], (0, 2, 3, 1))
            o = sb_sample(q_bd, brow, padk(k_sb), padk(v_sb), ck, cv, page_table, n_hs)
            return o.reshape(bs * ts, d_s)

        xp, kp, vp, wp, hp = _layer(xp, bp, tp, None, None, prompt_sb, w, alpha)
        xs, ks_, vs_, ws_, hs_ = _layer(xs, bs, ts, state_shift[l].astype(F32), state_wkv[l], sample_sb, w, alpha)
        for lst, val in zip(outs, (jnp.transpose(kp, (0, 3, 1, 2)), jnp.transpose(vp, (0, 3, 1, 2)), wp, hp,
                                   ks_.reshape(bs, ts, n_hs, HEAD_DIM), vs_.reshape(bs, ts, n_hs, HEAD_DIM),
                                   ws_.astype(state_wkv.dtype), hs_)):
            lst.append(val)
    return (xp.reshape(bp, tp, d), xs.reshape(bs, ts, d)) + tuple(jnp.stack(o) for o in outs)
```

```python
import functools

import jax
import jax.numpy as jnp
from jax import lax
from jax.experimental import pallas as pl
from jax.experimental.pallas import tpu as pltpu

F32 = jnp.float32
BF16 = jnp.bfloat16

HEAD_DIM = 64
PAIR = 2 * HEAD_DIM
CHUNK = 64
LN_EPS = 1e-5
LNX_EPS = 64e-5
SB_SCALE = HEAD_DIM ** -0.5

ROW_TILE = 512
FFN_ROWS = 1024
MERGE_ROWS = 1024
FFN_SUB = 512
FFN_RESIDENT_BYTES = 20 * 1024 * 1024
SCAN_ROWS = 1024
SCAN_GROUP = 4
SB_TILE = 512
SB_KEYS = 256
SB_PAIRS = 4
SB_PAGES = 16
SB_PREFETCH = 3
VMEM_LIMIT = 56 * 1024 * 1024


def _dot(a, b):
    return jnp.dot(a.astype(BF16), b.astype(BF16), preferred_element_type=F32)


def _dot_nt(a, b):
    return lax.dot_general(a.astype(BF16), b.astype(BF16), (((1,), (1,)), ((), ())),
                           preferred_element_type=F32)


def _split(x):
    hi = x.astype(BF16)
    lo = (x - hi.astype(F32)).astype(BF16)
    return hi, lo


def _dot_hl(a, b_exact):
    hi, lo = _split(a)
    return (jnp.dot(hi, b_exact, preferred_element_type=F32)
            + jnp.dot(lo, b_exact, preferred_element_type=F32))


def _dot3(a, b, nt=False):
    ah, al = _split(a)
    bh, bl = _split(b)
    if nt:
        d = lambda x, y: lax.dot_general(x, y, (((1,), (1,)), ((), ())), preferred_element_type=F32)
    else:
        d = lambda x, y: jnp.dot(x, y, preferred_element_type=F32)
    return d(ah, bh) + d(al, bh) + d(ah, bl)


LOG2E = 1.4426950408889634


def _neg_abs(x):
    bits = pltpu.bitcast(x, jnp.uint32) | jnp.uint32(0x80000000)
    return pltpu.bitcast(bits, F32)


def _softplus(u):
    return jnp.maximum(u, 0.0) + jnp.log1p(jnp.exp(-jnp.abs(u)))


def _layer_norm(y, g, b):
    mu = jnp.mean(y, axis=-1, keepdims=True)
    yc = y - mu
    var = jnp.mean(yc * yc, axis=-1, keepdims=True)
    return yc * lax.rsqrt(var + LN_EPS) * g + b


def _full_spec(a, grid_rank):
    nd = a.ndim
    if grid_rank == 1:
        return pl.BlockSpec(a.shape, lambda i: (0,) * nd)
    if grid_rank == 2:
        return pl.BlockSpec(a.shape, lambda i, j: (0,) * nd)
    return pl.BlockSpec(a.shape, lambda i, j, k: (0,) * nd)


def _ffn_ln_kernel(x_ref, wg_ref, wu_ref, wo_ref, g_ref, b_ref, o_ref, acc_ref, *, alpha, n_ff):
    j = pl.program_id(1)

    xb = x_ref[...].astype(BF16)
    tf = wg_ref.shape[1]
    cuts = list(range(0, tf, FFN_SUB)) + [tf]
    cols = [slice(a, b) for a, b in zip(cuts[:-1], cuts[1:])]
    gu = [(jnp.dot(xb, wg_ref[:, c], preferred_element_type=F32),
           jnp.dot(xb, wu_ref[:, c], preferred_element_type=F32)) for c in cols]
    mids = [(gate * jax.nn.sigmoid(gate) * up).astype(BF16) for gate, up in gu]
    out = jnp.dot(mids[0], wo_ref[cols[0], :], preferred_element_type=F32)
    for mid, c in zip(mids[1:], cols[1:]):
        out = out + jnp.dot(mid, wo_ref[c, :], preferred_element_type=F32)
    if n_ff == 1:
        o_ref[...] = _layer_norm(alpha * x_ref[...] + 0.5 * out, g_ref[...], b_ref[...])
        return

    @pl.when(j == 0)
    def _():
        acc_ref[...] = out

    @pl.when(j > 0)
    def _():
        acc_ref[...] += out

    @pl.when(j == n_ff - 1)
    def _():
        y = alpha * x_ref[...] + 0.5 * acc_ref[...]
        o_ref[...] = _layer_norm(y, g_ref[...], b_ref[...])


def _ffn_tile(d, d_ff):
    if 3 * d * d_ff * 2 <= FFN_RESIDENT_BYTES:
        return d_ff
    for n in (2, 11, 22):
        if d_ff % n == 0 and (d_ff // n) % 128 == 0:
            return d_ff // n
    return d_ff


def ffn_ln(x, w_in_bf, w_out_bf, g, b, alpha):
    n, d = x.shape
    d_ff = w_out_bf.shape[0]
    tm = min(FFN_ROWS, n)
    tf = _ffn_tile(d, d_ff)
    n_ff = d_ff // tf
    wmode = dict(pipeline_mode=pl.Buffered(1)) if n_ff == 1 else {}
    return pl.pallas_call(
        functools.partial(_ffn_ln_kernel, alpha=alpha, n_ff=n_ff),
        grid=(n // tm, n_ff),
        in_specs=[
            pl.BlockSpec((tm, d), lambda i, j: (i, 0)),
            pl.BlockSpec((d, tf), lambda i, j: (0, j), **wmode),
            pl.BlockSpec((d, tf), lambda i, j: (0, j + n_ff), **wmode),
            pl.BlockSpec((tf, d), lambda i, j: (j, 0), **wmode),
            pl.BlockSpec((1, d), lambda i, j: (0, 0)),
            pl.BlockSpec((1, d), lambda i, j: (0, 0)),
        ],
        out_specs=pl.BlockSpec((tm, d), lambda i, j: (i, 0)),
        out_shape=jax.ShapeDtypeStruct((n, d), F32),
        scratch_shapes=[pltpu.VMEM((tm, d), F32)],
        compiler_params=pltpu.CompilerParams(
            dimension_semantics=("arbitrary", "arbitrary"), vmem_limit_bytes=VMEM_LIMIT),
        name="ffn_ln",
    )(x, w_in_bf, w_in_bf, w_out_bf, g.reshape(1, d), b.reshape(1, d))


N_PREP_PARAMS = 15
N_PREP_OUTS = 12


def _rwkv_prep_math(h, h_prev, p, p_prev, params, outs, head_major):
    (wrkv_ref, ww1_ref, ww2_ref, aw1_ref, aw2_ref, gw1_ref, gw2_ref,
     mu_rkv_ref, mu_wag_ref, w0_ref, a0_ref, kk_ref, ka_ref, e_ref, wsb_ref) = params
    r_out, lw_out, k_out, a_out, b_out, v_out, g_out, q_sb, k_sb, v_sb, kb_sb, vb_sb = outs
    d_r = w0_ref.shape[-1]
    d_s = kb_sb.shape[-1]
    psb = jnp.dot(h.astype(BF16), wsb_ref[...], preferred_element_type=F32)
    q_sb[...] = (psb[:, :d_s] * SB_SCALE).astype(BF16)
    ksb = psb[:, d_s:2 * d_s]
    vsb = psb[:, 2 * d_s:]
    if head_major:
        k_sb[...] = ksb.T.reshape(k_sb.shape)
        v_sb[...] = vsb.T.reshape(v_sb.shape)
    else:
        k_sb[...] = ksb
        v_sb[...] = vsb
    kb_sb[...] = ksb.astype(BF16)
    vb_sb[...] = vsb.astype(BF16)
    rkv = p + (p_prev - p) * mu_rkv_ref[...]
    r = rkv[:, :d_r]
    k = rkv[:, d_r:2 * d_r]
    v = rkv[:, 2 * d_r:]
    dx = h_prev - h
    xw = h + dx * mu_wag_ref[0:1, :]
    xa = h + dx * mu_wag_ref[1:2, :]
    xg = h + dx * mu_wag_ref[2:3, :]
    lw = _dot(jnp.tanh(_dot(xw, ww1_ref[...])), ww2_ref[...])
    w_log = -_softplus(-(w0_ref[...] + lw)) - 0.5
    a_gate = jax.nn.sigmoid(a0_ref[...] + _dot(_dot(xa, aw1_ref[...]), aw2_ref[...]))
    g = _dot(jax.nn.sigmoid(_dot(xg, gw1_ref[...])), gw2_ref[...])
    kk = k * kk_ref[...]
    ss = _dot(kk * kk, e_ref[...])
    kk = kk / jnp.maximum(jnp.sqrt(ss), 1e-12)
    r_out[...] = r
    lw_out[...] = -jnp.exp(w_log)
    k_out[...] = k * (1.0 + (a_gate - 1.0) * ka_ref[...])
    a_out[...] = -kk
    b_out[...] = kk * a_gate
    v_out[...] = v
    g_out[...] = g


def _rwkv_prep_prompt_kernel(h_ref, *refs, tiles_per_seq):
    params, outs = refs[:N_PREP_PARAMS], refs[N_PREP_PARAMS:N_PREP_PARAMS + N_PREP_OUTS]
    hcarry_ref, pcarry_ref = refs[-2:]
    i = pl.program_id(0)
    h = h_ref[...]
    tm = h.shape[0]

    @pl.when(i % tiles_per_seq == 0)
    def _():
        hcarry_ref[...] = jnp.zeros_like(hcarry_ref)
        pcarry_ref[...] = jnp.zeros_like(pcarry_ref)

    row = lax.broadcasted_iota(jnp.int32, (tm, 1), 0)
    p = jnp.dot(h.astype(BF16), params[0][...], preferred_element_type=F32)
    h_prev = jnp.where(row == 0, hcarry_ref[7:8, :], pltpu.roll(h, 1, 0))
    p_prev = jnp.where(row == 0, pcarry_ref[7:8, :], pltpu.roll(p, 1, 0))
    _rwkv_prep_math(h, h_prev, p, p_prev, params, outs, True)
    hcarry_ref[...] = h[tm - 8:, :]
    pcarry_ref[...] = p[tm - 8:, :]


def _rwkv_prep_sample_kernel(h_ref, hl_ref, *refs, seq):
    params, outs = refs[:N_PREP_PARAMS], refs[N_PREP_PARAMS:N_PREP_PARAMS + N_PREP_OUTS]
    h = h_ref[...]
    tm = h.shape[0]
    row = lax.broadcasted_iota(jnp.int32, (tm, 1), 0)
    first = row % seq == 0
    p = jnp.dot(h.astype(BF16), params[0][...], preferred_element_type=F32)
    p_last = jnp.dot(hl_ref[...].astype(BF16), params[0][...], preferred_element_type=F32)
    h_prev = jnp.where(first, hl_ref[...], pltpu.roll(h, 1, 0))
    p_prev = jnp.where(first, p_last, pltpu.roll(p, 1, 0))
    _rwkv_prep_math(h, h_prev, p, p_prev, params, outs, False)


def mixer_prep(h, h_last_rows, batch, seq, prm):
    n, d = h.shape
    d_r = prm["w0"].shape[-1]
    d_s = prm["w_sb"].shape[1] // 3
    n_h = d_s // HEAD_DIM
    plist = [prm["w_rkv"], prm["w_w1"], prm["w_w2"], prm["a_w1"], prm["a_w2"], prm["g_w1"], prm["g_w2"],
             prm["mu_rkv"], prm["mu_wag"], prm["w0"], prm["a0"], prm["k_k"], prm["k_a"], prm["e_head"],
             prm["w_sb"]]
    assert len(plist) == N_PREP_PARAMS
    if h_last_rows is None:
        tm = min(ROW_TILE, seq)
        kern = functools.partial(_rwkv_prep_prompt_kernel, tiles_per_seq=seq // tm)
        args = [h] + plist
        in_specs = [pl.BlockSpec((tm, d), lambda i: (i, 0))] + [_full_spec(a, 1) for a in plist]
        scratch = [pltpu.VMEM((8, d), F32), pltpu.VMEM((8, 3 * d_r), F32)]
    else:
        tm = n
        kern = functools.partial(_rwkv_prep_sample_kernel, seq=seq)
        args = [h, h_last_rows] + plist
        in_specs = [pl.BlockSpec((tm, d), lambda i: (i, 0))] * 2 + [_full_spec(a, 1) for a in plist]
        scratch = []
    row_spec = lambda w: pl.BlockSpec((tm, w), lambda i: (i, 0))
    if h_last_rows is None:
        tps = seq // tm
        kv_spec = pl.BlockSpec((None, n_h, HEAD_DIM, tm), lambda i: (i // tps, 0, 0, i % tps))
        kv_shape = jax.ShapeDtypeStruct((batch, n_h, HEAD_DIM, seq), F32)
    else:
        kv_spec = row_spec(d_s)
        kv_shape = jax.ShapeDtypeStruct((n, d_s), F32)
    return pl.pallas_call(
        kern,
        grid=(n // tm,),
        in_specs=in_specs,
        out_specs=[row_spec(d_r)] * 7 + [row_spec(d_s), kv_spec, kv_spec, row_spec(d_s), row_spec(d_s)],
        out_shape=[jax.ShapeDtypeStruct((n, d_r), F32)] * 7
                  + [jax.ShapeDtypeStruct((n, d_s), BF16), kv_shape, kv_shape,
                     jax.ShapeDtypeStruct((n, d_s), BF16), jax.ShapeDtypeStruct((n, d_s), BF16)],
        scratch_shapes=scratch,
        compiler_params=pltpu.CompilerParams(
            dimension_semantics=("arbitrary",), vmem_limit_bytes=VMEM_LIMIT),
        name="mixer_prep",
    )(*args)


def _scan_group(n_chunks):
    for g in (SCAN_GROUP, 2, 1):
        if n_chunks % g == 0:
            return g


def _stack_heads(x, m0):
    return jnp.concatenate([jnp.where(m0, x, 0.0), jnp.where(m0, 0.0, x)], axis=0)


def _rwkv_scan_kernel(r_ref, lw_ref, k_ref, a_ref, b_ref, v_ref, g_ref, s0_ref,
                      rk_ref, lg_ref, lb_ref, tri_ref, emean_ref, eones_ref,
                      o_ref, sfin_ref, s_ref, ta_ref, tv_ref, mr_ref, cl_ref, *, n_chunks, n_pairs, n_seq):
    i = pl.program_id(1)
    c2 = 2 * CHUNK
    seq_chunks = n_chunks // n_seq

    @pl.when(i == 0)
    def _():
        zero = jnp.zeros((HEAD_DIM, HEAD_DIM), F32)
        for q in range(n_seq):
            for p in range(n_pairs):
                top = jnp.concatenate([s0_ref[q, 2 * p], zero], axis=1)
                bot = jnp.concatenate([zero, s0_ref[q, 2 * p + 1]], axis=1)
                s_ref[q, p] = jnp.concatenate([top, bot], axis=0)

    lane = lax.broadcasted_iota(jnp.int32, (1, PAIR), 1)
    m0 = lane < HEAD_DIM
    rr = lax.broadcasted_iota(jnp.int32, (c2, c2), 0)
    cc = lax.broadcasted_iota(jnp.int32, (c2, c2), 1)
    strict = cc < rr
    incl = cc <= rr
    eye = (cc == rr).astype(F32)
    tri = tri_ref[...]
    mm = lambda x, y: jnp.dot(x, y, preferred_element_type=F32)
    mm_nt = lambda x, y: lax.dot_general(x, y, (((1,), (1,)), ((), ())), preferred_element_type=F32)

    emean = emean_ref[...]
    eones = eones_ref[...]
    r_k = rk_ref[...]
    lnx_g = lg_ref[...]
    lnx_b = lb_ref[...]

    pairs = range(n_pairs)
    cat0 = lambda x, y: jnp.concatenate([x, y], axis=0)
    stack = lambda x: _stack_heads(x, m0).astype(BF16)
    in_refs = (r_ref, lw_ref, k_ref, a_ref, b_ref, v_ref, g_ref)

    def load_rows(c):
        rows = pl.ds(pl.multiple_of(c * CHUNK, CHUNK), CHUNK)
        return rows, [ref[rows, :] for ref in in_refs]

    def pre_prep(ins, p):
        lanes = slice(p * PAIR, (p + 1) * PAIR)
        r, lw, k, a, b, v, g = (x[:, lanes] for x in ins)
        hi, lo = _split(lw)
        lo2 = (lw - hi.astype(F32) - lo.astype(F32)).astype(BF16)
        cl = mm(tri, hi) + mm(tri, lo) + mm(tri, lo2)
        w_inv = jnp.exp(-cl)
        return dict(cl=cl, vst=stack(v), ast=stack(a * jnp.exp(cl - lw)), rst=stack(r * jnp.exp(cl)),
                    bst=stack(b * w_inv), kst=stack(k * w_inv))

    def state_prep(ins, cl, p):
        lanes = slice(p * PAIR, (p + 1) * PAIR)
        r, lw, k, a, b, v, g = (x[:, lanes] for x in ins)
        cl_end = cl[CHUNK - 1:CHUNK, :]
        w_tail = jnp.exp(cl_end - cl)
        vst_f = _stack_heads(v, m0)
        return dict(r=r, k=k, v=v, g=g, lanes=lanes, w_end=jnp.exp(cl_end), vst_f=vst_f, vst=vst_f.astype(BF16),
                    rst=stack(r * jnp.exp(cl)), btl=stack(b * w_tail), ktl=stack(k * w_tail))

    grp = _scan_group(n_chunks)
    n_grp = n_chunks // grp
    assert n_seq == 1 or n_grp == 1

    def step(g_state, g_pre, slot):
        todo = []
        if g_state is not None:
            seq_of = [0 if n_seq == 1 else j // seq_chunks for j in range(grp)]
            s_all = [[s_ref[q, p] for p in pairs] for q in range(n_seq)]
            done = []
            for j in range(grp):
                s_cur = s_all[seq_of[j]]

                def first(j=j, s_cur=s_cur):
                    rows, ins = load_rows(g_state * grp + j)
                    s = list(s_cur)
                    sb = [x.astype(BF16) for x in s]
                    u = [mm_nt(ta_ref[slot, j, p], sb[p]) + tv_ref[slot, j, p] for p in pairs]
                    return rows, ins, s, sb, u

                def second(st, j=j, s_cur=s_cur):
                    rows, ins, s, sb, u = st
                    e = [state_prep(ins, cl_ref[slot, j, p], p) for p in pairs]
                    uv_t = [cat0(u[p], e[p]["vst_f"]).T.astype(BF16) for p in pairs]
                    for p in pairs:
                        s_cur[p] = s[p] * e[p]["w_end"] + mm(uv_t[p], cat0(e[p]["btl"], e[p]["ktl"]))
                    y = [mm_nt(e[p]["rst"], sb[p])
                         + mm(mr_ref[slot, j, p], cat0(u[p].astype(BF16), e[p]["vst"])) for p in pairs]
                    done.append((rows, e, y))
                todo.append((first, second))

        def run_state_piece(state):
            idx, pending = state
            if idx >= len(todo):
                return state
            if pending is None:
                return idx, todo[idx][0]()
            todo[idx][1](pending)
            return idx + 1, None

        cursor = (0, None)
        if g_pre is not None:
            chains = [(j, p) for j in range(grp) for p in pairs]
            ins_p = [load_rows(g_pre * grp + j)[1] for j in range(grp)]
            cursor = run_state_piece(cursor)
            d = [pre_prep(ins_p[j], p) for j, p in chains]
            m4 = [mm_nt(cat0(x["ast"], x["rst"]), cat0(x["bst"], x["kst"])) for x in d]
            cursor = run_state_piece(cursor)
            nab = [jnp.where(strict, m[:c2, :c2], 0.0) for m in m4]
            mak = [jnp.where(strict, m[:c2, c2:], 0.0).astype(BF16) for m in m4]
            mrbk = [jnp.concatenate([jnp.where(incl, m[c2:, :c2], 0.0), jnp.where(incl, m[c2:, c2:], 0.0)],
                                    axis=1).astype(BF16) for m in m4]
            t = [eye + n for n in nab]
            pw = [n.astype(BF16) for n in nab]
            for _ in range(5):
                pw = [mm(x, x).astype(BF16) for x in pw]
                cursor = run_state_piece(cursor)
                t = [tt + mm(x, tt.astype(BF16)) for x, tt in zip(pw, t)]
                cursor = run_state_piece(cursor)
            mv = [mm(mk, x["vst"]) for mk, x in zip(mak, d)]
            tx = [mm(tt.astype(BF16), jnp.concatenate([x["ast"], m.astype(BF16)], axis=1))
                  for tt, x, m in zip(t, d, mv)]
        while cursor[0] < len(todo):
            cursor = run_state_piece(cursor)
        if g_state is not None:
            for rows, e, y in done:
                yp = [x[:CHUNK, :] + x[CHUNK:, :] for x in y]
                mu = [_dot(x, emean) for x in yp]
                yc = [x - m for x, m in zip(yp, mu)]
                var = [_dot(x * x, emean) for x in yc]
                bonus = [_dot(x["r"] * x["k"] * r_k[:, x["lanes"]], eones) * x["v"] for x in e]
                outs = [(yc[p] * lax.rsqrt(var[p] + LNX_EPS) * lnx_g[:, e[p]["lanes"]] + lnx_b[:, e[p]["lanes"]]
                         + bonus[p]) * e[p]["g"] for p in pairs]
                o_ref[rows, :] = jnp.concatenate(outs, axis=1)
            for q in range(n_seq):
                for p in pairs:
                    s_ref[q, p] = s_all[q][p]
        if g_pre is not None:
            for (j, p), x, txx, mr in zip(chains, d, tx, mrbk):
                ta_ref[1 - slot, j, p] = txx[:, :PAIR].astype(BF16)
                tv_ref[1 - slot, j, p] = txx[:, PAIR:]
                mr_ref[1 - slot, j, p] = mr
                cl_ref[1 - slot, j, p] = x["cl"]

    step(None, 0, 1)

    def body(gi, carry):
        step(gi, gi + 1, lax.rem(gi, 2))
        return carry

    lax.fori_loop(0, n_grp - 1, body, 0)
    step(n_grp - 1, None, (n_grp - 1) % 2)
    for q in range(n_seq):
        for p in range(n_pairs):
            sfin_ref[q, 2 * p] = s_ref[q, p, :HEAD_DIM, :HEAD_DIM]
            sfin_ref[q, 2 * p + 1] = s_ref[q, p, HEAD_DIM:, HEAD_DIM:]


def rwkv_scan(r, lw, k, a, b, v, g, s0, r_k, lnx_g, lnx_b, batch, seq):
    n, d_r = r.shape
    n_pairs = d_r // PAIR
    if seq > CHUNK:
        n_seq, rows = 1, min(SCAN_ROWS, seq)
    else:
        n_seq = next(q for q in (SCAN_GROUP, 2, 1) if batch % q == 0)
        rows = n_seq * seq
    steps = max(seq // rows, 1)
    grp = _scan_group(rows // CHUNK)
    tri = jnp.tril(jnp.ones((CHUNK, CHUNK), F32)).astype(BF16)
    blk = jnp.arange(PAIR) // HEAD_DIM
    same = (blk[:, None] == blk[None, :]).astype(F32)
    emean = (same / HEAD_DIM).astype(BF16)
    eones = same.astype(BF16)
    tok = pl.BlockSpec((rows, d_r), lambda bi, i: (bi * steps + i, 0))
    par = pl.BlockSpec((1, d_r), lambda bi, i: (0, 0))
    st = pl.BlockSpec((n_seq, 2 * n_pairs, HEAD_DIM, HEAD_DIM), lambda bi, i: (bi, 0, 0, 0))
    return pl.pallas_call(
        functools.partial(_rwkv_scan_kernel, n_chunks=rows // CHUNK, n_pairs=n_pairs, n_seq=n_seq),
        grid=(batch // n_seq, steps),
        in_specs=[tok] * 7 + [st, par, par, par, _full_spec(tri, 2), _full_spec(emean, 2), _full_spec(eones, 2)],
        out_specs=[tok, st],
        out_shape=[jax.ShapeDtypeStruct((n, d_r), F32),
                   jax.ShapeDtypeStruct((batch, 2 * n_pairs, HEAD_DIM, HEAD_DIM), F32)],
        scratch_shapes=[pltpu.VMEM((n_seq, n_pairs, PAIR, PAIR), F32),
                        pltpu.VMEM((2, grp, n_pairs, PAIR, PAIR), BF16),
                        pltpu.VMEM((2, grp, n_pairs, PAIR, PAIR), F32),
                        pltpu.VMEM((2, grp, n_pairs, PAIR, 2 * PAIR), BF16),
                        pltpu.VMEM((2, grp, n_pairs, CHUNK, PAIR), F32)],
        compiler_params=pltpu.CompilerParams(
            dimension_semantics=("arbitrary", "arbitrary"), vmem_limit_bytes=VMEM_LIMIT),
        name="rwkv_scan",
    )(r, lw, k, a, b, v, g, s0, r_k.reshape(1, d_r), lnx_g.reshape(1, d_r), lnx_b.reshape(1, d_r),
      tri, emean, eones)


def _sb_prompt_kernel(bias_ref, q_ref, k_ref, v_ref, u_ref, o_ref, acc_ref, c_ref, *, tq, tk, n_hp):
    g = pl.program_id(1)
    i = pl.program_id(2)
    n_sub = tq // tk
    n_heads = 2 * n_hp
    lane = lax.broadcasted_iota(jnp.int32, (1, PAIR), 1)
    m0 = lane < HEAD_DIM
    pair_lanes = [slice((h // 2) * PAIR, (h // 2 + 1) * PAIR) for h in range(n_heads)]
    q_heads = []
    for h in range(n_heads):
        qp = q_ref[:, pair_lanes[h]]
        qh = jnp.where(m0 if h % 2 == 0 else ~m0, qp, jnp.zeros_like(qp))
        bias = jnp.full((1, PAIR), bias_ref[n_heads * g + h], F32)
        b1 = bias.astype(BF16).astype(F32)
        b2 = (bias - b1).astype(BF16).astype(F32)
        b3 = (bias - b1) - b2
        ext = jnp.where(lane == 0, b1, jnp.where(lane == 1, b2, jnp.where(lane == 2, b3, 0.0)))
        q_heads.append(jnp.concatenate([qh, jnp.broadcast_to(ext, qh.shape).astype(BF16)], axis=1))
    k_ext = {nk: jnp.broadcast_to(jnp.where(lane < 3, 1.0, 0.0), (nk, PAIR)).astype(BF16)
             for nk in range(tk, tq + 1, tk)}
    acc_ref[...] = jnp.zeros_like(acc_ref)
    c_ref[...] = jnp.zeros_like(c_ref)
    u = u_ref[...]
    rr = lax.broadcasted_iota(jnp.int32, (tk, tk), 0)
    cc = lax.broadcasted_iota(jnp.int32, (tk, tk), 1)
    earlier = cc < rr
    tile = lambda x: jnp.concatenate([x] * (tk // PAIR), axis=1)

    def logits(kb, q0, nq_rows, n_keys_sub):
        nk = n_keys_sub * tk
        rows = pl.ds(pl.multiple_of(kb * tq, tq), nk)
        ks = [jnp.concatenate([k_ref[rows, pair_lanes[h]], k_ext[nk]], axis=1) for h in range(n_heads)]
        return [lax.dot_general(qh[q0:q0 + nq_rows], kh, (((1,), (1,)), ((), ())), preferred_element_type=F32)
                for qh, kh in zip(q_heads, ks)]

    def attend(zs, kb, q0, nq_rows, n_keys_sub, mask_last):
        qrows = slice(q0, q0 + nq_rows)
        nk = n_keys_sub * tk
        rows = pl.ds(pl.multiple_of(kb * tq, tq), nk)
        subs = [slice(j * tk, (j + 1) * tk) for j in range(n_keys_sub)]
        vs = [v_ref[rows, pair_lanes[h]] for h in range(n_heads)]
        ns = [jnp.maximum(z, 0.0) + jnp.log(1.0 + jnp.exp(_neg_abs(z))) for z in zs]
        masks =[earlier if (mask_last and j == n_keys_sub - 1) else None for j in range(n_keys_sub)]
        nm = [[n[:, s] if m is None else jnp.where(m, n[:, s], 0.0) for s, m in zip(subs, masks)] for n in ns]
        cums = [[jnp.dot(x.astype(BF16), u, preferred_element_type=F32) for x in nh] for nh in nm]
        tots = [[jnp.broadcast_to(cum[:, :1] + x[:, :1], (nq_rows, PAIR)) for cum, x in zip(cm, nh)]
                for cm, nh in zip(cums, nm)]
        ws = []
        for h in range(n_heads):
            run = c_ref[h, qrows]
            parts = [None] * n_keys_sub
            for j in range(n_keys_sub - 1, -1, -1):
                parts[j] = cums[h][j] + tile(run)
                run = run + tots[h][j]
            c_ref[h, qrows] = run
            w = jnp.exp((zs[h] - ns[h]) - jnp.concatenate(parts, axis=1))
            if mask_last:
                w = jnp.concatenate([w[:, s] if m is None else jnp.where(m, w[:, s], 0.0)
                                     for s, m in zip(subs, masks)], axis=1)
            ws.append(w)
        for h in range(n_heads):
            acc_ref[h, qrows] += jnp.dot(ws[h].astype(BF16), vs[h], preferred_element_type=F32)

    for r in range(n_sub):
        attend(logits(i, r * tk, tk, r + 1), i, r * tk, tk, r + 1, True)

    def body(it, carry):
        kb = i - 1 - it
        attend(logits(kb, 0, tq, n_sub), kb, 0, tq, n_sub, False)
        return carry

    lax.fori_loop(0, i, body, 0)
    o_ref[...] = jnp.concatenate(
        [jnp.where(m0, acc_ref[2 * hp], acc_ref[2 * hp + 1]) for hp in range(n_hp)], axis=1)


def sb_prompt(q_bf, k_bf, v_bf, bias, batch, seq):
    n, d_s = q_bf.shape
    n_hp = SB_PAIRS
    width = n_hp * PAIR
    n_grp = d_s // width
    tq = min(SB_TILE, seq)
    tk = min(SB_KEYS, seq)
    nq = seq // tq
    j = jnp.arange(tk)
    ux = (j[:, None] > j[None, :]).astype(BF16)
    grid_spec = pltpu.PrefetchScalarGridSpec(
        num_scalar_prefetch=1,
        grid=(batch, n_grp, nq),
        in_specs=[
            pl.BlockSpec((tq, width), lambda b, p, i, bias_ref: (b * nq + i, p)),
            pl.BlockSpec((seq, width), lambda b, p, i, bias_ref: (b, p)),
            pl.BlockSpec((seq, width), lambda b, p, i, bias_ref: (b, p)),
            pl.BlockSpec(ux.shape, lambda b, p, i, bias_ref: (0, 0)),
        ],
        out_specs=pl.BlockSpec((tq, width), lambda b, p, i, bias_ref: (b * nq + i, p)),
        scratch_shapes=[pltpu.VMEM((2 * n_hp, tq, PAIR), F32)] * 2,
    )
    return pl.pallas_call(
        functools.partial(_sb_prompt_kernel, tq=tq, tk=tk, n_hp=n_hp),
        grid_spec=grid_spec,
        out_shape=jax.ShapeDtypeStruct((n, d_s), F32),
        compiler_params=pltpu.CompilerParams(
            dimension_semantics=("arbitrary", "arbitrary", "arbitrary"), vmem_limit_bytes=VMEM_LIMIT),
        name="sb_prompt",
    )(bias, q_bf, k_bf, v_bf, ux)


def _sb_sample_kernel(pt_ref, q_ref, brow_ref, kn_ref, vn_ref, ck_hbm, cv_hbm, ux_ref, o_ref,
                      acc_ref, c_ref, kbuf, vbuf, sem, *, n_pg, n_heads, n_pages, depth):
    nbuf = depth + 1
    steps = pl.num_programs(1)
    total = pl.num_programs(0) * steps
    j = pl.program_id(1)
    t = pl.program_id(0) * steps + j

    def page_copies(step, slot):
        bb = step // steps
        base = n_pages - n_pg * (step - bb * steps + 1)
        cps = []
        for i in range(n_pg):
            page = pt_ref[bb, base + i]
            cps.append(pltpu.make_async_copy(ck_hbm.at[page], kbuf.at[slot, i], sem.at[slot]))
            cps.append(pltpu.make_async_copy(cv_hbm.at[page], vbuf.at[slot, i], sem.at[slot]))
        return cps

    @pl.when(t == 0)
    def _():
        for d in range(depth):
            for cp in page_copies(d, d):
                cp.start()

    slot = lax.rem(t, nbuf)
    for cp in page_copies(t, slot):
        cp.wait()
    kps = [kbuf.at[slot, i] for i in range(n_pg)]
    vps = [vbuf.at[slot, i] for i in range(n_pg)]
    qbd = q_ref[...]
    nrow = qbd.shape[0]
    brow = brow_ref[...]
    ux = ux_ref[...]

    def log_terms(z, valid):
        nb = z.shape[1] // PAIR
        z = z + jnp.concatenate([brow] * nb, axis=1)
        sp = _softplus(z)
        l = -sp
        if valid is not None:
            l = jnp.where(valid, l, 0.0)
        lst = jnp.concatenate([l[:, bi * PAIR:(bi + 1) * PAIR] for bi in range(nb)], axis=0)
        return z - sp, _dot_hl(lst, ux)

    def weights(la, cx, run, valid):
        nb = la.shape[1] // PAIR
        parts = [None] * nb
        for bi in range(nb - 1, -1, -1):
            blk = cx[bi * nrow:(bi + 1) * nrow, :]
            parts[bi] = blk[:, :PAIR] + run
            run = run + blk[:, PAIR:]
        w = jnp.exp(la + jnp.concatenate(parts, axis=1))
        if valid is not None:
            w = jnp.where(valid, w, 0.0)
        return w.astype(BF16), run

    @pl.when(j == 0)
    def _():
        t_of_row = lax.broadcasted_iota(jnp.int32, (nrow, PAIR), 0) // n_heads
        key = lax.broadcasted_iota(jnp.int32, (nrow, PAIR), 1)
        valid = key < t_of_row
        la, cx = log_terms(_dot_nt(qbd, kn_ref[...]), valid)
        w, run = weights(la, cx, jnp.zeros((nrow, PAIR), F32), valid)
        acc_ref[...] = _dot(w, vn_ref[...])
        c_ref[...] = run

    d_s = qbd.shape[1]
    page_t = lambda r: r[...].reshape(d_s, PAIR).astype(BF16)
    n_grp = 2 if n_pg % 2 == 0 else 1
    per = n_pg // n_grp
    groups = [range(gi * per, (gi + 1) * per) for gi in range(n_grp)]
    zs = [jnp.concatenate([_dot(qbd, page_t(kps[pi])) for pi in grp], axis=1) for grp in groups]
    terms = [log_terms(z, None) for z in zs]
    run = c_ref[...]
    ws = [None] * n_grp
    for gi in range(n_grp - 1, -1, -1):
        ws[gi], run = weights(terms[gi][0], terms[gi][1], run, None)
    c_ref[...] = run
    out = None
    for grp, w in zip(groups, ws):
        for li, pi in enumerate(grp):
            o = _dot_nt(w[:, li * PAIR:(li + 1) * PAIR], page_t(vps[pi]))
            out = o if out is None else out + o
    acc_ref[...] += out

    nxt = t + depth
    for cp in page_copies(lax.rem(nxt, total), lax.rem(nxt, nbuf)):
        cp.start()

    @pl.when(j == steps - 1)
    def _():
        d_s = acc_ref.shape[1]
        head_of_row = lax.broadcasted_iota(jnp.int32, (nrow, d_s), 0) % n_heads
        head_of_lane = lax.broadcasted_iota(jnp.int32, (nrow, d_s), 1) // HEAD_DIM
        sel = jnp.where(head_of_row == head_of_lane, acc_ref[...], 0.0)
        o_ref[...] = jnp.sum(sel.reshape(nrow // n_heads, n_heads, d_s), axis=1)

    @pl.when(t == total - 1)
    def _():
        for d in range(1, depth + 1):
            for cp in page_copies(lax.rem(t + d, total), lax.rem(t + d, nbuf)):
                cp.wait()


def sb_sample(q_bd, brow, k_new, v_new, cache_k, cache_v, page_table, n_heads):
    bsz, nrow, d_s = q_bd.shape
    n_pages = page_table.shape[1]
    page = cache_k.shape[3]
    assert page == PAIR
    n_pg = min(SB_PAGES, n_pages)
    steps = n_pages // n_pg
    jj = jnp.arange(PAIR)
    ux = jnp.concatenate([(jj[:, None] > jj[None, :]).astype(F32), jnp.ones((PAIR, PAIR), F32)], axis=1).astype(BF16)

    depth = SB_PREFETCH
    assert bsz * steps >= depth
    per_b = lambda shape: pl.BlockSpec((None,) + shape, lambda b, j, pt: (b, 0, 0))
    page_buf = pltpu.VMEM((depth + 1, n_pg, n_heads, HEAD_DIM, page), F32)
    grid_spec = pltpu.PrefetchScalarGridSpec(
        num_scalar_prefetch=1,
        grid=(bsz, steps),
        in_specs=[per_b((nrow, d_s)), pl.BlockSpec(brow.shape, lambda b, j, pt: (0, 0)),
                  per_b((PAIR, d_s)), per_b((PAIR, d_s)),
                  pl.BlockSpec(memory_space=pl.ANY), pl.BlockSpec(memory_space=pl.ANY),
                  pl.BlockSpec(ux.shape, lambda b, j, pt: (0, 0))],
        out_specs=per_b((nrow // n_heads, d_s)),
        scratch_shapes=[pltpu.VMEM((nrow, d_s), F32), pltpu.VMEM((nrow, PAIR), F32), page_buf, page_buf,
                        pltpu.SemaphoreType.DMA((depth + 1,))],
    )
    return pl.pallas_call(
        functools.partial(_sb_sample_kernel, n_pg=n_pg, n_heads=n_heads, n_pages=n_pages, depth=depth),
        grid_spec=grid_spec,
        out_shape=jax.ShapeDtypeStruct((bsz, nrow // n_heads, d_s), F32),
        compiler_params=pltpu.CompilerParams(
            dimension_semantics=("arbitrary", "arbitrary"), vmem_limit_bytes=VMEM_LIMIT),
        name="sb_sample",
    )(page_table, q_bd, brow, k_new, v_new, cache_k, cache_v, ux)


def _merge_kernel(h_ref, oa_ref, ob_ref, wg_ref, wba_ref, wbb_ref, wo_ref, g_ref, b_ref, o_ref, *, alpha):
    h = h_ref[...]
    d = h.shape[1]
    gates = jax.nn.sigmoid(jnp.dot(h.astype(BF16), wg_ref[...], preferred_element_type=F32))
    ma = jnp.dot(oa_ref[...].astype(BF16), wba_ref[...], preferred_element_type=F32)
    mb = jnp.dot(ob_ref[...].astype(BF16), wbb_ref[...], preferred_element_type=F32)
    merged = gates[:, :d] * ma + gates[:, d:] * mb
    y = alpha * h + jnp.dot(merged.astype(BF16), wo_ref[...], preferred_element_type=F32)
    o_ref[...] = _layer_norm(y, g_ref[...], b_ref[...])


def merge(h, o_a, o_b, w_gate_bf, w_ba_bf, w_bb_bf, w_out_bf, g, b, alpha):
    n, d = h.shape
    tm = min(MERGE_ROWS, n)
    row = lambda a: pl.BlockSpec((tm, a.shape[1]), lambda i: (i, 0))
    ws = [w_gate_bf, w_ba_bf, w_bb_bf, w_out_bf, g.reshape(1, d), b.reshape(1, d)]
    const = lambda a: pl.BlockSpec(a.shape, lambda i: (0,) * a.ndim, pipeline_mode=pl.Buffered(1))
    return pl.pallas_call(
        functools.partial(_merge_kernel, alpha=alpha),
        grid=(n // tm,),
        in_specs=[row(h), row(o_a), row(o_b)] + [const(a) for a in ws],
        out_specs=pl.BlockSpec((tm, d), lambda i: (i, 0)),
        out_shape=jax.ShapeDtypeStruct((n, d), F32),
        compiler_params=pltpu.CompilerParams(
            dimension_semantics=("arbitrary",), vmem_limit_bytes=VMEM_LIMIT),
        name="merge",
    )(h, o_a, o_b, *ws)


def _layer(x, batch, seq, h_last, wkv0, sb_fn, w, alpha):
    n, d = x.shape
    h = ffn_ln(x, w["ffn1_in"], w["ffn1_out"], w["ln1_g"], w["ln1_b"], alpha)
    if h_last is None:
        hl_rows = None
        rows_pad = seq
    else:
        hl_rows = jnp.repeat(h_last, seq, axis=0)
        rows_pad = CHUNK
    r, lw, k, a, b, v, g, q_bf, k_sb, v_sb, k_bf, v_bf = mixer_prep(h, hl_rows, batch, seq, w)
    n_h = w["w0"].shape[-1] // HEAD_DIM
    if wkv0 is None:
        s0 = jnp.zeros((batch, n_h, HEAD_DIM, HEAD_DIM), F32)
    else:
        s0 = wkv0.astype(F32)
    scan_in = [r, lw, k, a, b, v, g]
    if rows_pad != seq:
        pad = lambda t: jnp.pad(t.reshape(batch, seq, -1), ((0, 0), (0, rows_pad - seq), (0, 0))).reshape(
            batch * rows_pad, -1)
        scan_in = [pad(t) for t in scan_in]
    o_a, wkv = rwkv_scan(*scan_in, s0, w["r_k"], w["lnx_g"], w["lnx_b"], batch, rows_pad)
    if rows_pad != seq:
        o_a = o_a.reshape(batch, rows_pad, -1)[:, :seq].reshape(n, -1)
    o_b = sb_fn(q_bf, k_sb, v_sb, k_bf, v_bf)
    x2 = merge(h, o_a, o_b, w["w_gate"], w["w_ba"], w["w_bb"], w["w_out"], w["ln2_g"], w["ln2_b"], alpha)
    x3 = ffn_ln(x2, w["ffn2_in"], w["ffn2_out"], w["ln3_g"], w["ln3_b"], alpha)
    return x3, k_sb, v_sb, wkv, h.reshape(batch, seq, d)[:, -1]


def kernel(x_prompt, x_sample, cache_k, cache_v, state_wkv, state_shift, page_table, ln1_g, ln1_b, ffn1_w_in, ffn1_w_out, w_in, mu_rkv, mu_wag, w0, w_w1, w_w2, a0, a_w1, a_w2, g_w1, g_w2, k_k, k_a, r_k, lnx_g, lnx_b, sb_bias, w_branch, w_out, ln2_g, ln2_b, ffn2_w_in, ffn2_w_out, ln3_g, ln3_b):
    depth = ln1_g.shape[0]
    alpha = (2.0 * depth) ** 0.25
    bp, tp, d = x_prompt.shape
    bs, ts, _ = x_sample.shape
    d_r = w0.shape[-1]
    d_s = (w_in.shape[-1] - 3 * d_r - 2 * d) // 3
    n_hs = d_s // HEAD_DIM
    blk = jnp.arange(d_r) // HEAD_DIM
    e_head = (blk[:, None] == blk[None, :]).astype(BF16)

    xp = x_prompt.reshape(bp * tp, d)
    xs = x_sample.reshape(bs * ts, d)
    outs = [[] for _ in range(8)]
    for l in range(depth):
        row = lambda a: a[l].reshape(1, -1)
        w = dict(
            ffn1_in=ffn1_w_in[l].astype(BF16), ffn1_out=ffn1_w_out[l].astype(BF16),
            ffn2_in=ffn2_w_in[l].astype(BF16), ffn2_out=ffn2_w_out[l].astype(BF16),
            ln1_g=ln1_g[l], ln1_b=ln1_b[l], ln2_g=ln2_g[l], ln2_b=ln2_b[l], ln3_g=ln3_g[l], ln3_b=ln3_b[l],
            w_rkv=w_in[l][:, :3 * d_r].astype(BF16),
            w_sb=w_in[l][:, 3 * d_r:3 * d_r + 3 * d_s].astype(BF16),
            w_gate=w_in[l][:, 3 * d_r + 3 * d_s:].astype(BF16),
            w_w1=w_w1[l].astype(BF16), w_w2=w_w2[l].astype(BF16), a_w1=a_w1[l].astype(BF16),
            a_w2=a_w2[l].astype(BF16), g_w1=g_w1[l].astype(BF16), g_w2=g_w2[l].astype(BF16),
            mu_rkv=row(mu_rkv), mu_wag=mu_wag[l], w0=row(w0), a0=row(a0), k_k=row(k_k), k_a=row(k_a),
            e_head=e_head, r_k=r_k[l].reshape(-1), lnx_g=lnx_g[l], lnx_b=lnx_b[l],
            w_ba=w_branch[l][:d_r].astype(BF16), w_bb=w_branch[l][d_r:].astype(BF16),
            w_out=w_out[l].astype(BF16),
        )
        bias = sb_bias[l].astype(F32)

        def prompt_sb(q_bf, k_sb, v_sb, k_bf, v_bf):
            return sb_prompt(q_bf, k_bf, v_bf, bias, bp, tp)

        def sample_sb(q_bf, k_sb, v_sb, k_bf, v_bf):
            head_of_lane = jnp.arange(d_s) // HEAD_DIM
            onehot = (jnp.arange(n_hs)[:, None] == head_of_lane[None, :])
            q_bd = jnp.where(onehot[None, None], q_bf.reshape(bs, ts, 1, d_s), jnp.zeros((), BF16))
            q_bd = q_bd.reshape(bs, ts * n_hs, d_s)
            brow = jnp.broadcast_to(jnp.tile(bias, ts)[:, None], (ts * n_hs, PAIR))
            padk = lambda t: jnp.pad(t.reshape(bs, ts, d_s), ((0, 0), (0, PAIR - ts), (0, 0)))
            ck = jnp.transpose(cache_k[l], (0, 2, 3, 1))
            cv = jnp.transpose(cache_v[l], (0, 2, 3, 1))
            o = sb_sample(q_bd, brow, padk(k_sb), padk(v_sb), ck, cv, page_table, n_hs)
            return o.reshape(bs * ts, d_s)

        xp, kp, vp, wp, hp = _layer(xp, bp, tp, None, None, prompt_sb, w, alpha)
        xs, ks_, vs_, ws_, hs_ = _layer(xs, bs, ts, state_shift[l].astype(F32), state_wkv[l], sample_sb, w, alpha)
        for lst, val in zip(outs, (jnp.transpose(kp, (0, 3, 1, 2)), jnp.transpose(vp, (0, 3, 1, 2)), wp, hp,
                                   ks_.reshape(bs, ts, n_hs, HEAD_DIM), vs_.reshape(bs, ts, n_hs, HEAD_DIM),
                                   ws_.astype(state_wkv.dtype), hs_)):
            lst.append(val)
    return (xp.reshape(bp, tp, d), xs.reshape(bs, ts, d)) + tuple(jnp.stack(o) for o in outs)
```
